```python
import jax, jax.numpy as jnp
from jax import lax
import numpy as np

D_MODEL = 1024
BATCH = 16
SEQ = 4096
DEPTH = 1

GRID_W = 64
WIN_H_MAX = 8
WIN_W = 16
ATTN_HEADS = 8
ATTN_HEAD_DIM = 64
ATTN_WIDTH = ATTN_HEADS * ATTN_HEAD_DIM
DN_HEADS = 4
DN_HEAD_DIM = 128
DN_WIDTH = DN_HEADS * DN_HEAD_DIM
DN_CONV = 5
DN_CHUNK = 64
D_MIX = ATTN_WIDTH + DN_WIDTH
IN_COLS = 3 * ATTN_WIDTH + 4 * DN_WIDTH + 4 * DN_HEADS
N_GROUPS = 8
EXPERTS_PER_GROUP = 8
N_EXPERTS = N_GROUPS * EXPERTS_PER_GROUP
TOP_K_IN_GROUP = 2
D_EXPERT = D_MODEL // 4
MOE_BLOCK = 256
EPS = 1e-6

kernel_name = 'hybrid_na_gdn_hmoe_block'


def rms_norm(x, g):
    xf = x.astype(jnp.float32)
    y = xf * lax.rsqrt(jnp.mean(xf * xf, axis=-1, keepdims=True) + EPS)
    return (y * g.astype(jnp.float32)).astype(x.dtype)


def l2_norm(x):
    xf = x.astype(jnp.float32)
    return xf * lax.rsqrt(jnp.sum(xf * xf, axis=-1, keepdims=True) + EPS)


def neighbourhood_attention(q, k, v, rpb):
    b_, s_, h_, dh = q.shape
    rows = s_ // GRID_W
    kh = min(WIN_H_MAX, rows)
    to_grid = lambda t: t.reshape(b_, rows, GRID_W, h_, dh).transpose(1, 0, 3, 2, 4)
    qg, kg, vg = to_grid(q), to_grid(k), to_grid(v)
    cols = jnp.arange(GRID_W)
    col_start = jnp.clip(cols - WIN_W // 2, 0, GRID_W - WIN_W)
    col_idx = col_start[:, None] + jnp.arange(WIN_W)[None, :]
    col_off = col_idx - cols[:, None] + (WIN_W - 1)
    scale = dh ** -0.5

    def row_block(args):
        r, q_row = args
        r0 = jnp.clip(r - kh // 2, 0, rows - kh)
        k_band = lax.dynamic_slice_in_dim(kg, r0, kh, axis=0)
        v_band = lax.dynamic_slice_in_dim(vg, r0, kh, axis=0)
        k_win = k_band[:, :, :, col_idx]
        v_win = v_band[:, :, :, col_idx]
        row_off = r0 + jnp.arange(kh) - r + (WIN_H_MAX - 1)
        bias = rpb[:, row_off[None, :, None], col_off[:, None, :]]
        logits = jnp.einsum('bhqd,rbhqcd->bhqrc', q_row, k_win).astype(jnp.float32) * scale
        logits = logits + bias.astype(jnp.float32)[None]
        p = jax.nn.softmax(logits.reshape(b_, h_, GRID_W, kh * WIN_W), axis=-1)
        p = p.reshape(b_, h_, GRID_W, kh, WIN_W).astype(v.dtype)
        return jnp.einsum('bhqrc,rbhqce->bqhe', p, v_win)

    out = lax.map(row_block, (jnp.arange(rows), qg))
    return out.transpose(1, 0, 2, 3, 4).reshape(b_, s_, h_ * dh)


def gated_delta_rule(q, k, v, g, beta):
    f32 = jnp.float32
    bn, hn, t_, dk = q.shape
    dv = v.shape[-1]
    c = DN_CHUNK
    nc = t_ // c
    q = q.astype(f32).reshape(bn, hn, nc, c, dk)
    k = k.astype(f32).reshape(bn, hn, nc, c, dk)
    v = v.astype(f32).reshape(bn, hn, nc, c, dv)
    beta = beta.astype(f32).reshape(bn, hn, nc, c)
    gc = jnp.cumsum(g.astype(f32).reshape(bn, hn, nc, c), axis=-1)
    incl = jnp.tril(jnp.ones((c, c), bool))
    strict = jnp.tril(jnp.ones((c, c), bool), -1)
    diff = gc[..., :, None] - gc[..., None, :]
    decay = jnp.where(incl, jnp.exp(jnp.where(incl, diff, 0.0)), 0.0)
    kb = k * beta[..., None]
    lower = jnp.where(strict, jnp.einsum('bhnid,bhnjd->bhnij', kb, k) * decay, 0.0)
    a_mat = jnp.eye(c, dtype=f32) + lower
    rhs = jnp.concatenate([v * beta[..., None], kb * jnp.exp(gc)[..., None]], axis=-1)
    sol = lax.linalg.triangular_solve(a_mat, rhs, left_side=True, lower=True, unit_diagonal=True)
    u, w = sol[..., :dv], sol[..., dv:]
    qk = jnp.einsum('bhnid,bhnjd->bhnij', q, k) * decay

    def step(state, inp):
        q_c, k_c, u_c, w_c, g_c, qk_c = inp
        v_new = u_c - jnp.einsum('bhck,bhkv->bhcv', w_c, state)
        o_c = (jnp.einsum('bhck,bhkv->bhcv', q_c * jnp.exp(g_c)[..., None], state)
               + jnp.einsum('bhij,bhjv->bhiv', qk_c, v_new))
        g_last = g_c[..., -1:]
        state = (state * jnp.exp(g_last)[..., None]
                 + jnp.einsum('bhck,bhcv->bhkv', k_c * jnp.exp(g_last - g_c)[..., None], v_new))
        return state, o_c

    xs = tuple(jnp.moveaxis(t, 2, 0) for t in (q, k, u, w, gc, qk))
    state0 = jnp.zeros((bn, hn, dk, dv), f32)
    _, o = lax.scan(step, state0, xs)
    return jnp.moveaxis(o, 0, 2).reshape(bn, hn, t_, dv)


def bidirectional_gated_deltanet(qkv_raw, z, a, b, conv_w, a_log, dt_bias, out_norm_g):
    b_, s_, _ = qkv_raw.shape
    qkv = lax.conv_general_dilated(qkv_raw, conv_w[:, None, :].astype(qkv_raw.dtype),
                                   window_strides=(1,), padding=[(DN_CONV // 2, DN_CONV // 2)],
                                   dimension_numbers=('NWC', 'WIO', 'NWC'),
                                   feature_group_count=3 * DN_WIDTH)
    qkv = jax.nn.silu(qkv)
    q, k, v = jnp.split(qkv.reshape(b_, s_, 3, DN_HEADS, DN_HEAD_DIM), 3, axis=2)
    q = l2_norm(q[:, :, 0]) * (DN_HEAD_DIM ** -0.5)
    k = l2_norm(k[:, :, 0])
    v = v[:, :, 0].astype(jnp.float32)
    g = -jnp.exp(a_log.astype(jnp.float32)) * jax.nn.softplus(a.astype(jnp.float32) + dt_bias.astype(jnp.float32))
    beta = jax.nn.sigmoid(b.astype(jnp.float32))
    bht = lambda t: t.transpose(0, 2, 1, 3)
    both = lambda t: jnp.concatenate([t, jnp.flip(t, axis=2)], axis=1)
    q2, k2, v2 = both(bht(q)), both(bht(k)), both(bht(v))
    g2 = jnp.concatenate([g[:, :, 0].transpose(0, 2, 1), jnp.flip(g[:, :, 1].transpose(0, 2, 1), axis=-1)], axis=1)
    beta2 = jnp.concatenate([beta[:, :, 0].transpose(0, 2, 1), jnp.flip(beta[:, :, 1].transpose(0, 2, 1), axis=-1)], axis=1)
    o2 = gated_delta_rule(q2, k2, v2, g2, beta2)
    o = o2[:, :DN_HEADS] + jnp.flip(o2[:, DN_HEADS:], axis=2)
    o = o.transpose(0, 2, 1, 3)
    zf = z.reshape(b_, s_, DN_HEADS, DN_HEAD_DIM).astype(jnp.float32)
    o = rms_norm(o, out_norm_g) * jax.nn.silu(zf)
    return o.reshape(b_, s_, DN_WIDTH).astype(qkv_raw.dtype)


def hierarchical_moe(h, wg_r, bg_r, we_r, be_r, w_gate, w_up, w_down):
    b_, s_, d_ = h.shape
    n = b_ * s_
    hf = h.reshape(n, d_)
    group_logits = (hf @ wg_r + bg_r).astype(jnp.float32)
    group_prob = jax.nn.softmax(group_logits, axis=-1)
    gsel = jnp.argmax(group_logits, axis=-1)
    gp = jnp.take_along_axis(group_prob, gsel[:, None], axis=-1)
    exp_logits = (hf @ we_r + be_r).astype(jnp.float32).reshape(n, N_GROUPS, EXPERTS_PER_GROUP)
    in_group = jnp.take_along_axis(exp_logits, gsel[:, None, None], axis=1)[:, 0]
    top_val, top_idx = lax.top_k(in_group, TOP_K_IN_GROUP)
    gates = gp * jax.nn.softmax(top_val, axis=-1)
    eid = (gsel[:, None] * EXPERTS_PER_GROUP + top_idx).reshape(-1).astype(jnp.int32)
    gate_flat = gates.reshape(-1)
    tok_flat = jnp.repeat(jnp.arange(n, dtype=jnp.int32), TOP_K_IN_GROUP)
    nk = n * TOP_K_IN_GROUP
    p_rows = -(-nk // MOE_BLOCK) * MOE_BLOCK + N_EXPERTS * MOE_BLOCK
    nb = p_rows // MOE_BLOCK
    order = jnp.argsort(eid)
    e_sorted = eid[order]
    counts = jnp.bincount(eid, length=N_EXPERTS)
    starts = jnp.cumsum(counts) - counts
    padded = -(-counts // MOE_BLOCK) * MOE_BLOCK
    pends = jnp.cumsum(padded)
    pstarts = pends - padded
    dest = pstarts[e_sorted] + (jnp.arange(nk) - starts[e_sorted])
    buf_tok = jnp.full((p_rows,), n, jnp.int32).at[dest].set(tok_flat[order])
    buf_gate = jnp.zeros((p_rows,), jnp.float32).at[dest].set(gate_flat[order])
    block_expert = jnp.clip(jnp.searchsorted(pends, jnp.arange(nb) * MOE_BLOCK, side='right'),
                            0, N_EXPERTS - 1).astype(jnp.int32)
    h_pad = jnp.concatenate([hf, jnp.zeros((1, d_), hf.dtype)], axis=0)

    def expert_block(args):
        tok, e = args
        xb = h_pad[tok]
        hid = jax.nn.silu(xb @ w_gate[e]) * (xb @ w_up[e])
        return hid @ w_down[e]

    out = lax.map(expert_block, (buf_tok.reshape(nb, MOE_BLOCK), block_expert))
    out = (out.reshape(p_rows, d_).astype(jnp.float32) * buf_gate[:, None]).astype(h.dtype)
    y = jnp.zeros((n + 1, d_), h.dtype).at[buf_tok].add(out)[:n]
    return y.reshape(b_, s_, d_)


def hybrid_layer(x, ln1_g, w_in, q_norm_g, k_norm_g, rpb, attn_out_g, conv_w, a_log, dt_bias,
                 dn_out_g, w_out, ln2_g, wg_r, bg_r, we_r, be_r, w_gate, w_up, w_down):
    b_, s_, _ = x.shape
    h = rms_norm(x, ln1_g)
    proj = h @ w_in
    a3 = 3 * ATTN_WIDTH
    aq, ak, av, dqkv, dz, dab = jnp.split(
        proj, [ATTN_WIDTH, 2 * ATTN_WIDTH, a3, a3 + 3 * DN_WIDTH, a3 + 4 * DN_WIDTH], axis=-1)
    hs = lambda t: t.reshape(b_, s_, ATTN_HEADS, ATTN_HEAD_DIM)
    aq = rms_norm(hs(aq), q_norm_g)
    ak = rms_norm(hs(ak), k_norm_g)
    attn = rms_norm(neighbourhood_attention(aq, ak, hs(av), rpb), attn_out_g)
    dab = dab.reshape(b_, s_, 2, 2, DN_HEADS)
    dn = bidirectional_gated_deltanet(dqkv, dz, dab[:, :, 0], dab[:, :, 1], conv_w, a_log,
                                      dt_bias, dn_out_g)
    x = x + jnp.concatenate([attn, dn], axis=-1) @ w_out
    x = x + hierarchical_moe(rms_norm(x, ln2_g), wg_r, bg_r, we_r, be_r, w_gate, w_up, w_down)
    return x


def setup_inputs(seed: int = 0) -> dict:
    key = jax.random.key(seed)
    ks = jax.random.split(key, 24)
    f32 = jnp.float32
    nrm = lambda k, shape, s: jax.random.normal(k, shape, f32) * s
    gain = lambda k, shape: 1.0 + 0.01 * jax.random.normal(k, shape, f32)
    L = DEPTH
    dt = jnp.exp(jax.random.uniform(ks[9], (L, 2, DN_HEADS), f32, np.log(1e-3), np.log(1e-1)))
    return {
        'x': nrm(ks[0], (BATCH, SEQ, D_MODEL), 1.0),
        'ln1_g': gain(ks[1], (L, D_MODEL)),
        'w_in': nrm(ks[2], (L, D_MODEL, IN_COLS), D_MODEL ** -0.5),
        'attn_q_norm_g': gain(ks[3], (L, ATTN_HEAD_DIM)),
        'attn_k_norm_g': gain(ks[4], (L, ATTN_HEAD_DIM)),
        'attn_rpb': nrm(ks[5], (L, ATTN_HEADS, 2 * WIN_H_MAX - 1, 2 * WIN_W - 1), 0.1),
        'attn_out_norm_g': gain(ks[6], (L, ATTN_WIDTH)),
        'dn_conv_w': nrm(ks[7], (L, DN_CONV, 3 * DN_WIDTH), DN_CONV ** -0.5),
        'dn_a_log': jnp.log(jax.random.uniform(ks[8], (L, 2, DN_HEADS), f32, 1.0, 16.0)),
        'dn_dt_bias': dt + jnp.log(-jnp.expm1(-dt)),
        'dn_out_norm_g': gain(ks[10], (L, DN_HEAD_DIM)),
        'w_out': nrm(ks[11], (L, D_MIX, D_MODEL), D_MIX ** -0.5),
        'ln2_g': gain(ks[12], (L, D_MODEL)),
        'router_group_w': nrm(ks[13], (L, D_MODEL, N_GROUPS), D_MODEL ** -0.5),
        'router_group_b': nrm(ks[14], (L, N_GROUPS), 0.01),
        'router_expert_w': nrm(ks[15], (L, D_MODEL, N_EXPERTS), D_MODEL ** -0.5),
        'router_expert_b': nrm(ks[16], (L, N_EXPERTS), 0.01),
        'expert_w_gate': nrm(ks[17], (L, N_EXPERTS, D_MODEL, D_EXPERT), D_MODEL ** -0.5),
        'expert_w_up': nrm(ks[18], (L, N_EXPERTS, D_MODEL, D_EXPERT), D_MODEL ** -0.5),
        'expert_w_down': nrm(ks[19], (L, N_EXPERTS, D_EXPERT, D_MODEL), D_EXPERT ** -0.5),
    }


def reference(x, ln1_g, w_in, attn_q_norm_g, attn_k_norm_g, attn_rpb, attn_out_norm_g, dn_conv_w,
              dn_a_log, dn_dt_bias, dn_out_norm_g, w_out, ln2_g, router_group_w, router_group_b,
              router_expert_w, router_expert_b, expert_w_gate, expert_w_up, expert_w_down):
    for l in range(DEPTH):
        x = hybrid_layer(x, ln1_g[l], w_in[l], attn_q_norm_g[l], attn_k_norm_g[l], attn_rpb[l],
                         attn_out_norm_g[l], dn_conv_w[l], dn_a_log[l], dn_dt_bias[l],
                         dn_out_norm_g[l], w_out[l], ln2_g[l], router_group_w[l], router_group_b[l],
                         router_expert_w[l], router_expert_b[l], expert_w_gate[l], expert_w_up[l],
                         expert_w_down[l])
    return x
```

```python
import functools

import jax
import jax.numpy as jnp
from jax import lax
from jax.experimental import pallas as pl
from jax.experimental.pallas import tpu as pltpu

F32 = jnp.float32
BF16 = jnp.bfloat16
I32 = jnp.int32

EPS = 1e-6
GRID_W = 64
WIN_H = 8
WIN_W = 16
ATTN_HEADS = 8
ATTN_HD = 64
ATTN_WIDTH = ATTN_HEADS * ATTN_HD
DN_HEADS = 4
DN_HD = 128
DN_WIDTH = DN_HEADS * DN_HD
DN_CONV = 5
DN_CHUNK = 64
N_GROUPS = 8
EPG = 8
N_EXPERTS = N_GROUPS * EPG
MOE_BLOCK = 256
NEG = -1e30

LANES = 128
VMEM_LIMIT = 56 * 1024 * 1024


def _cparams(sem):
    return pltpu.CompilerParams(dimension_semantics=sem, vmem_limit_bytes=VMEM_LIMIT)


def _dot(a, b):
    return jnp.dot(a, b, preferred_element_type=F32)


def _dot_nt(a, b):
    return lax.dot_general(a, b, (((1,), (1,)), ((), ())), preferred_element_type=F32)


def _split_hi_lo(x):
    hi = x.astype(BF16)
    lo = (x - hi.astype(F32)).astype(BF16)
    return hi, lo


def _inproj_body(x_ref, g_ref, wa_ref, wd_ref, wz_ref, wab_ref, oa_ref, od_ref, oz_ref, oab_ref):
    x = x_ref[...]
    ms = jnp.mean(x * x, axis=-1, keepdims=True)
    h = x * lax.rsqrt(ms + EPS) * g_ref[...]
    hi, lo = _split_hi_lo(h)
    oa_ref[...] = _dot(hi, wa_ref[...]).astype(BF16)
    od_ref[...] = _dot(hi, wd_ref[...]).astype(BF16)
    oz_ref[...] = _dot(hi, wz_ref[...]).astype(BF16)
    nab = oab_ref.shape[-1]
    ab = _dot(hi, wab_ref[...])
    ab_lo = _dot(lo, wab_ref[:, :nab])
    oab_ref[...] = ab[:, :nab] + ab[:, nab:] + ab_lo


def _inproj(x2, ln1_g, w_in, tm=512):
    n, d = x2.shape
    a3 = 3 * ATTN_WIDTH
    d3 = 3 * DN_WIDTH
    wa = w_in[:, :a3].astype(BF16)
    wd = w_in[:, a3:a3 + d3].astype(BF16)
    wz = w_in[:, a3 + d3:a3 + d3 + DN_WIDTH].astype(BF16)
    wab_f = w_in[:, a3 + d3 + DN_WIDTH:]
    nab = wab_f.shape[1]
    wab_hi, wab_lo = _split_hi_lo(wab_f)
    wab = jnp.concatenate([wab_hi, wab_lo], axis=1)
    const = lambda i: (0, 0)
    row = lambda i: (i, 0)
    return pl.pallas_call(
        _inproj_body,
        grid=(n // tm,),
        in_specs=[
            pl.BlockSpec((tm, d), row),
            pl.BlockSpec((1, d), const),
            pl.BlockSpec((d, a3), const),
            pl.BlockSpec((d, d3), const),
            pl.BlockSpec((d, DN_WIDTH), const),
            pl.BlockSpec((d, 2 * nab), const),
        ],
        out_specs=[
            pl.BlockSpec((tm, a3), row),
            pl.BlockSpec((tm, d3), row),
            pl.BlockSpec((tm, DN_WIDTH), row),
            pl.BlockSpec((tm, nab), row),
        ],
        out_shape=[
            jax.ShapeDtypeStruct((n, a3), BF16),
            jax.ShapeDtypeStruct((n, d3), BF16),
            jax.ShapeDtypeStruct((n, DN_WIDTH), BF16),
            jax.ShapeDtypeStruct((n, nab), F32),
        ],
        compiler_params=_cparams(("parallel",)),
    )(x2, ln1_g.reshape(1, d), wa, wd, wz, wab)


def _attn_bias_table(rpb):
    c = jnp.arange(GRID_W)
    c0 = jnp.clip(c - WIN_W // 2, 0, GRID_W - WIN_W)
    kc = jnp.arange(GRID_W)
    inwin = (kc[None, :] >= c0[:, None]) & (kc[None, :] < c0[:, None] + WIN_W)
    coff = jnp.clip(kc[None, :] - c[:, None] + (WIN_W - 1), 0, 2 * WIN_W - 2)
    d = jnp.arange(WIN_H)
    j = jnp.arange(WIN_H)
    roff = j[None, :] - d[:, None] + (WIN_H - 1)
    tab = rpb[:, roff[:, None, :, None], coff[None, :, None, :]]
    tab = jnp.where(inwin[None, None, :, None, :], tab.astype(F32), NEG)
    h = rpb.shape[0]
    tab = tab.reshape(h // 2, 2, WIN_H, GRID_W, WIN_H * GRID_W)
    return tab.transpose(0, 2, 1, 3, 4).reshape(h // 2, WIN_H, 2 * GRID_W, WIN_H * GRID_W)


def _attn_body(q_ref, k_ref, v_ref, qg_ref, kg_ref, bias_ref, o_ref, qlo_s, qhi_s, kn_s, *, rows):
    s_len = rows * GRID_W
    lane = lax.broadcasted_iota(I32, (1, LANES), 1)
    is_lo = lane < ATTN_HD
    blk = 256

    def head_norm(x, gain):
        x2 = x * x
        s_lo = jnp.sum(jnp.where(is_lo, x2, 0.0), axis=-1, keepdims=True)
        s_hi = jnp.sum(jnp.where(is_lo, 0.0, x2), axis=-1, keepdims=True)
        ms = jnp.where(is_lo, s_lo, s_hi) * (1.0 / ATTN_HD)
        return x * lax.rsqrt(ms + EPS) * gain

    def prep(i, c):
        sl = pl.ds(pl.multiple_of(i * blk, blk), blk)
        qn = head_norm(q_ref[0, sl, :].astype(F32), qg_ref[...]) * (ATTN_HD ** -0.5)
        qlo_s[sl, :] = jnp.where(is_lo, qn, 0.0).astype(BF16)
        qhi_s[sl, :] = jnp.where(is_lo, 0.0, qn).astype(BF16)
        kn_s[sl, :] = head_norm(k_ref[0, sl, :].astype(F32), kg_ref[...]).astype(BF16)
        return c

    lax.fori_loop(0, s_len // blk, prep, 0)

    band = WIN_H * GRID_W

    def row_step(r, c):
        r0 = jnp.clip(r - WIN_H // 2, 0, rows - WIN_H)
        d = r - r0
        qsl = pl.ds(pl.multiple_of(r * GRID_W, GRID_W), GRID_W)
        ksl = pl.ds(pl.multiple_of(r0 * GRID_W, GRID_W), band)
        q2 = jnp.concatenate([qlo_s[qsl, :], qhi_s[qsl, :]], axis=0)
        s = _dot_nt(q2, kn_s[ksl, :]) + bias_ref[0, d]
        m = jnp.max(s, axis=-1, keepdims=True)
        p = jnp.exp(s - m)
        l = jnp.sum(p, axis=-1, keepdims=True)
        pv = _dot(p.astype(BF16), v_ref[0, ksl, :]) / l
        o = jnp.where(is_lo, pv[:GRID_W], pv[GRID_W:])
        o_ref[0, qsl, :] = o.astype(o_ref.dtype)
        return c

    lax.fori_loop(0, rows, row_step, 0)


def _attention(aqkv, q_g, k_g, rpb):
    b, s_len, _ = aqkv.shape
    rows = s_len // GRID_W
    assert min(WIN_H, rows) == WIN_H
    pairs = ATTN_HEADS // 2
    bias = _attn_bias_table(rpb)
    qg2 = jnp.tile(q_g, 2).reshape(1, LANES).astype(F32)
    kg2 = jnp.tile(k_g, 2).reshape(1, LANES).astype(F32)
    band = WIN_H * GRID_W
    return pl.pallas_call(
        functools.partial(_attn_body, rows=rows),
        grid=(pairs, b),
        in_specs=[
            pl.BlockSpec((1, s_len, LANES), lambda p, i: (i, 0, p)),
            pl.BlockSpec((1, s_len, LANES), lambda p, i: (i, 0, pairs + p)),
            pl.BlockSpec((1, s_len, LANES), lambda p, i: (i, 0, 2 * pairs + p)),
            pl.BlockSpec((1, LANES), lambda p, i: (0, 0)),
            pl.BlockSpec((1, LANES), lambda p, i: (0, 0)),
            pl.BlockSpec((1, WIN_H, 2 * GRID_W, band), lambda p, i: (p, 0, 0, 0)),
        ],
        out_specs=pl.BlockSpec((1, s_len, LANES), lambda p, i: (i, 0, p)),
        out_shape=jax.ShapeDtypeStruct((b, s_len, ATTN_WIDTH), BF16),
        scratch_shapes=[
            pltpu.VMEM((s_len, LANES), BF16),
            pltpu.VMEM((s_len, LANES), BF16),
            pltpu.VMEM((s_len, LANES), BF16),
        ],
        compiler_params=_cparams(("arbitrary", "arbitrary")),
    )(aqkv, aqkv, aqkv, qg2, kg2, bias)


SUPER = 2 * DN_CHUNK
HALO = 16


def _mmb(a, b):
    return _dot(a.astype(BF16), b.astype(BF16))


def _dn_body(q_ref, k_ref, v_ref, z_ref, ab_ref, cwq_ref, cwk_ref, cwv_ref, alog_ref, dtb_ref, og_ref,
             o_ref, pad_s, qs, ks, vs, of_s, ob_s, *, s_len):
    head = pl.program_id(1)
    n_super = s_len // SUPER

    zeros_halo = jnp.zeros((HALO, LANES), F32)
    for t, src in enumerate((q_ref, k_ref, v_ref)):
        pad_s[t, 0:HALO, :] = zeros_halo
        pad_s[t, HALO + s_len:2 * HALO + s_len, :] = zeros_halo

        def stage(i, c, t=t, src=src):
            off = pl.multiple_of(i * 512, 512)
            pad_s[t, pl.ds(HALO + off, 512), :] = src[0, pl.ds(off, 512), :].astype(F32)
            return c

        lax.fori_loop(0, s_len // 512, stage, 0)

    win = SUPER + HALO

    def conv_step(i, c):
        t0 = pl.multiple_of(i * SUPER, SUPER)
        outs = []
        for t, cw_ref in enumerate((cwq_ref, cwk_ref, cwv_ref)):
            xw = pad_s[t, pl.ds(t0 + HALO // 2, win), :]
            acc = jnp.zeros((SUPER, LANES), F32)
            for j in range(DN_CONV):
                sh = (DN_CONV // 2 - j) % win
                xr = xw if sh == 0 else pltpu.roll(xw, sh, axis=0)
                acc = acc + xr[HALO // 2:HALO // 2 + SUPER] * cw_ref[j:j + 1, :]
            outs.append(acc * jax.nn.sigmoid(acc))
        qc, kc, vc = outs
        sl = pl.ds(t0, SUPER)
        qs[sl, :] = qc * lax.rsqrt(jnp.sum(qc * qc, axis=-1, keepdims=True) + EPS) * (DN_HD ** -0.5)
        ks[sl, :] = kc * lax.rsqrt(jnp.sum(kc * kc, axis=-1, keepdims=True) + EPS)
        vs[sl, :] = vc
        return c

    lax.fori_loop(0, n_super, conv_step, 0)

    ri = lax.broadcasted_iota(I32, (SUPER, SUPER), 0)
    ci = lax.broadcasted_iota(I32, (SUPER, SUPER), 1)
    same = (ri // DN_CHUNK) == (ci // DN_CHUNK)
    eye = (ri == ci).astype(F32)
    pos = ri % DN_CHUNK

    def direction(sc, state, rev):
        t0 = pl.multiple_of(sc * SUPER, SUPER)
        sl = pl.ds(t0, SUPER)
        q = qs[sl, :]
        k = ks[sl, :]
        v = vs[sl, :]
        ab = ab_ref[0, 0, sl, :]
        dcol = 1 if rev else 0
        a_b = jnp.broadcast_to(ab[:, dcol:dcol + 1], (SUPER, LANES))
        b_b = jnp.broadcast_to(ab[:, 2 + dcol:3 + dcol], (SUPER, LANES))
        neg_rate = -jnp.exp(jnp.full((1, LANES), alog_ref[dcol, head], F32))
        g = neg_rate * jax.nn.softplus(a_b + dtb_ref[dcol, head])
        beta = jax.nn.sigmoid(b_b)
        gc = g
        for sft in (1, 2, 4, 8, 16, 32):
            if rev:
                gc = gc + jnp.where(pos < DN_CHUNK - sft, pltpu.roll(gc, SUPER - sft, axis=0), 0.0)
            else:
                gc = gc + jnp.where(pos >= sft, pltpu.roll(gc, sft, axis=0), 0.0)
        gct = gc.T
        diff = gc - gct
        if rev:
            incl = same & (ri <= ci)
            strict = same & (ri < ci)
        else:
            incl = same & (ri >= ci)
            strict = same & (ri > ci)
        decay = jnp.where(incl, jnp.exp(jnp.where(incl, diff, 0.0)), 0.0)
        kb = k * beta
        gram = _dot_nt(jnp.concatenate([q, kb], axis=0).astype(BF16), k.astype(BF16))
        qkd = gram[:SUPER] * decay
        low = jnp.where(strict, gram[SUPER:] * decay, 0.0)
        inv = eye - low
        pw = low
        for _ in range(5):
            pw = _mmb(pw, pw)
            inv = inv + _mmb(inv, pw)
        egc = jnp.exp(gc)
        rhs = jnp.concatenate([v * beta, kb * egc], axis=1)
        sol = _mmb(inv, rhs)
        qo = _mmb(qkd, sol)
        o0 = qo[:, :LANES]
        qt = q * egc - qo[:, LANES:]
        kt = k.T
        outs = [None, None]
        for cidx in ((1, 0) if rev else (0, 1)):
            lo = cidx * DN_CHUNK
            last = lo if rev else lo + DN_CHUNK - 1
            g_last = gc[last:last + 1, :]
            in_c = (ci // DN_CHUNK) == cidx
            kst = kt * jnp.where(in_c, jnp.exp(jnp.where(in_c, g_last - gct, 0.0)), 0.0)
            mn = _mmb(kst, sol)
            lhs = jnp.concatenate([-mn[:, LANES:], qt[lo:lo + DN_CHUNK]], axis=0)
            r = _mmb(lhs, state)
            outs[cidx] = r[LANES:] + o0[lo:lo + DN_CHUNK]
            state = jnp.exp(g_last) * state + r[:LANES] + mn[:, :LANES]
        dst = ob_s if rev else of_s
        dst[sl, :] = jnp.concatenate(outs, axis=0)
        return state

    def scan_step(i, carry):
        s_f, s_b = carry
        s_f = direction(i, s_f, False)
        s_b = direction(n_super - 1 - i, s_b, True)
        return s_f, s_b

    z0 = jnp.zeros((DN_HD, DN_HD), F32)
    lax.fori_loop(0, n_super, scan_step, (z0, z0))

    def fin(i, c):
        sl = pl.ds(pl.multiple_of(i * 512, 512), 512)
        o = of_s[sl, :] + ob_s[sl, :]
        o = o * lax.rsqrt(jnp.mean(o * o, axis=-1, keepdims=True) + EPS) * og_ref[...]
        zf = z_ref[0, sl, :].astype(F32)
        o_ref[0, sl, :] = (o * (zf * jax.nn.sigmoid(zf))).astype(o_ref.dtype)
        return c

    lax.fori_loop(0, s_len // 512, fin, 0)


def _deltanet(dqkv, dz, dab, conv_w, a_log, dt_bias, out_g):
    b, s_len, _ = dqkv.shape
    h = DN_HEADS
    ab = dab.reshape(b, s_len, 2, 2, h).transpose(0, 4, 1, 2, 3).reshape(b, h, s_len, 4)
    cw = conv_w.astype(F32)
    tok = lambda off: (lambda i, j: (i, 0, off + j))
    smem = pl.BlockSpec(memory_space=pltpu.SMEM)
    return pl.pallas_call(
        functools.partial(_dn_body, s_len=s_len),
        grid=(b, h),
        in_specs=[
            pl.BlockSpec((1, s_len, LANES), tok(0)),
            pl.BlockSpec((1, s_len, LANES), tok(h)),
            pl.BlockSpec((1, s_len, LANES), tok(2 * h)),
            pl.BlockSpec((1, s_len, LANES), tok(0)),
            pl.BlockSpec((1, 1, s_len, 4), lambda i, j: (i, j, 0, 0)),
            pl.BlockSpec((DN_CONV, LANES), lambda i, j: (0, j)),
            pl.BlockSpec((DN_CONV, LANES), lambda i, j: (0, h + j)),
            pl.BlockSpec((DN_CONV, LANES), lambda i, j: (0, 2 * h + j)),
            smem,
            smem,
            pl.BlockSpec((1, LANES), lambda i, j: (0, 0)),
        ],
        out_specs=pl.BlockSpec((1, s_len, LANES), tok(0)),
        out_shape=jax.ShapeDtypeStruct((b, s_len, DN_WIDTH), BF16),
        scratch_shapes=[
            pltpu.VMEM((3, s_len + 2 * HALO, LANES), F32),
            pltpu.VMEM((s_len, LANES), F32),
            pltpu.VMEM((s_len, LANES), F32),
            pltpu.VMEM((s_len, LANES), F32),
            pltpu.VMEM((s_len, LANES), F32),
            pltpu.VMEM((s_len, LANES), F32),
        ],
        compiler_params=_cparams(("arbitrary", "arbitrary")),
    )(dqkv, dqkv, dqkv, dz, ab, cw, cw, cw, a_log.astype(F32), dt_bias.astype(F32),
      out_g.reshape(1, LANES).astype(F32))


ROUTER_ROWS = 80


def _outproj_body(x_ref, at_ref, dn_ref, ag_ref, wo_ref, lg_ref, wrh_ref, wrl_ref, rb_ref, su_ref,
                  x1_ref, h2_ref, eid_ref, gate_ref, rank_ref, cnt_ref, base_s):
    step = pl.program_id(0)
    tm = x_ref.shape[0]

    @pl.when(step == 0)
    def _():
        base_s[...] = jnp.zeros_like(base_s)

    a = at_ref[...].astype(F32)
    an = a * lax.rsqrt(jnp.mean(a * a, axis=-1, keepdims=True) + EPS) * ag_ref[...]
    mix = jnp.concatenate([an.astype(BF16), dn_ref[...]], axis=1)
    x1 = x_ref[...] + _dot(mix, wo_ref[...])
    x1_ref[...] = x1
    h2 = x1 * lax.rsqrt(jnp.mean(x1 * x1, axis=-1, keepdims=True) + EPS) * lg_ref[...]
    h2_ref[...] = h2

    hi, lo = _split_hi_lo(h2)
    wrh = wrh_ref[...]
    lt = _dot_nt(wrh, hi) + _dot_nt(wrh, lo) + _dot_nt(wrl_ref[...], hi) + rb_ref[...]

    sub8 = lax.broadcasted_iota(I32, (N_GROUPS, tm), 0)
    gl = lt[0:N_GROUPS]
    gmax = jnp.max(gl, axis=0, keepdims=True)
    gsel = jnp.min(jnp.where(gl == gmax, sub8, N_GROUPS), axis=0, keepdims=True)
    gp = 1.0 / jnp.sum(jnp.exp(gl - gmax), axis=0, keepdims=True)
    in_group = jnp.zeros((EPG, tm), F32)
    for g in range(N_GROUPS):
        in_group = in_group + jnp.where(gsel == g, lt[N_GROUPS + g * EPG:N_GROUPS + (g + 1) * EPG], 0.0)
    v1 = jnp.max(in_group, axis=0, keepdims=True)
    i1 = jnp.min(jnp.where(in_group == v1, sub8, EPG), axis=0, keepdims=True)
    rest = jnp.where(sub8 == i1, -jnp.inf, in_group)
    v2 = jnp.max(rest, axis=0, keepdims=True)
    i2 = jnp.min(jnp.where(rest == v2, sub8, EPG), axis=0, keepdims=True)
    e21 = jnp.exp(v2 - v1)
    den = 1.0 + e21
    eid1 = gsel * EPG + i1
    eid2 = gsel * EPG + i2
    eid_ref[0:1, :] = eid1
    eid_ref[1:2, :] = eid2
    gate_ref[0:1, :] = gp * (1.0 / den)
    gate_ref[1:2, :] = gp * (e21 / den)

    sub64 = lax.broadcasted_iota(I32, (N_EXPERTS, tm), 0)
    oh1 = sub64 == eid1
    oh2 = sub64 == eid2
    ohs = jnp.where(oh1 | oh2, 1.0, 0.0)
    before = base_s[...] + _dot(ohs.astype(BF16), su_ref[...])
    rank_ref[0:1, :] = jnp.sum(jnp.where(oh1, before, 0.0), axis=0, keepdims=True).astype(I32)
    rank_ref[1:2, :] = jnp.sum(jnp.where(oh2, before, 0.0), axis=0, keepdims=True).astype(I32)
    base_new = base_s[...] + jnp.sum(ohs, axis=1, keepdims=True)
    base_s[...] = base_new
    cnt_ref[...] = jnp.broadcast_to(base_new, cnt_ref.shape)


def _outproj_router(x2, attn_raw, dn, attn_g, w_out, ln2_g, wg_r, bg_r, we_r, be_r, tm=512):
    n, d = x2.shape
    wr = jnp.concatenate([wg_r, we_r], axis=1).T.astype(F32)
    wr = jnp.pad(wr, ((0, ROUTER_ROWS - wr.shape[0]), (0, 0)))
    wrh, wrl = _split_hi_lo(wr)
    rb = jnp.pad(jnp.concatenate([bg_r, be_r]).astype(F32), (0, ROUTER_ROWS - N_GROUPS - N_EXPERTS))
    rb = rb.reshape(ROUTER_ROWS, 1)
    t_i = jnp.arange(tm)
    su = (t_i[:, None] < t_i[None, :]).astype(BF16)
    const = lambda i: (0, 0)
    row = lambda i: (i, 0)
    col = lambda i: (0, i)
    return pl.pallas_call(
        _outproj_body,
        grid=(n // tm,),
        in_specs=[
            pl.BlockSpec((tm, d), row),
            pl.BlockSpec((tm, ATTN_WIDTH), row),
            pl.BlockSpec((tm, DN_WIDTH), row),
            pl.BlockSpec((1, ATTN_WIDTH), const),
            pl.BlockSpec((ATTN_WIDTH + DN_WIDTH, d), const),
            pl.BlockSpec((1, d), const),
            pl.BlockSpec((ROUTER_ROWS, d), const),
            pl.BlockSpec((ROUTER_ROWS, d), const),
            pl.BlockSpec((ROUTER_ROWS, 1), const),
            pl.BlockSpec((tm, tm), const),
        ],
        out_specs=[
            pl.BlockSpec((tm, d), row),
            pl.BlockSpec((tm, d), row),
            pl.BlockSpec((2, tm), col),
            pl.BlockSpec((2, tm), col),
            pl.BlockSpec((2, tm), col),
            pl.BlockSpec((N_EXPERTS, LANES), const),
        ],
        out_shape=[
            jax.ShapeDtypeStruct((n, d), F32),
            jax.ShapeDtypeStruct((n, d), F32),
            jax.ShapeDtypeStruct((2, n), I32),
            jax.ShapeDtypeStruct((2, n), F32),
            jax.ShapeDtypeStruct((2, n), I32),
            jax.ShapeDtypeStruct((N_EXPERTS, LANES), F32),
        ],
        scratch_shapes=[pltpu.VMEM((N_EXPERTS, 1), F32)],
        compiler_params=_cparams(("arbitrary",)),
    )(x2, attn_raw, dn, attn_g.reshape(1, -1).astype(F32), w_out.astype(BF16), ln2_g.reshape(1, d).astype(F32),
      wrh, wrl, rb, su)


def _expert_body(be_ref, nu_ref, tok_ref, h_hbm, wg_ref, wu_ref, wd_ref, o_ref, xbuf, sem):
    i = pl.program_id(0)
    used = i < nu_ref[0]

    @pl.when(used)
    def _():
        base = i * MOE_BLOCK

        def issue(j, c):
            t = tok_ref[base + j]
            pltpu.make_async_copy(h_hbm.at[pl.ds(t, 1)], xbuf.at[pl.ds(j, 1)], sem).start()
            return c

        lax.fori_loop(0, MOE_BLOCK, issue, 0)
        pltpu.make_async_copy(h_hbm.at[pl.ds(0, MOE_BLOCK)], xbuf, sem).wait()
        x = xbuf[...].astype(BF16)
        gt = _dot(x, wg_ref[0])
        up = _dot(x, wu_ref[0])
        hid = gt * jax.nn.sigmoid(gt) * up
        o_ref[...] = _dot(hid.astype(BF16), wd_ref[0])

    @pl.when(jnp.logical_not(used))
    def _():
        o_ref[...] = jnp.zeros_like(o_ref)


def _experts(h2, buf_tok, block_expert, n_used, w_gate, w_up, w_down):
    n, d = h2.shape
    nb = block_expert.shape[0]
    de = w_gate.shape[-1]
    wmap = lambda i, be, nu, tok: (be[i], 0, 0)
    grid_spec = pltpu.PrefetchScalarGridSpec(
        num_scalar_prefetch=3,
        grid=(nb,),
        in_specs=[
            pl.BlockSpec(memory_space=pl.ANY),
            pl.BlockSpec((1, d, de), wmap),
            pl.BlockSpec((1, d, de), wmap),
            pl.BlockSpec((1, de, d), wmap),
        ],
        out_specs=pl.BlockSpec((MOE_BLOCK, d), lambda i, be, nu, tok: (i, 0)),
        scratch_shapes=[pltpu.VMEM((MOE_BLOCK, d), F32), pltpu.SemaphoreType.DMA(())],
    )
    return pl.pallas_call(
        _expert_body,
        grid_spec=grid_spec,
        out_shape=jax.ShapeDtypeStruct((nb * MOE_BLOCK, d), F32),
        compiler_params=_cparams(("arbitrary",)),
    )(block_expert, n_used, buf_tok, h2, w_gate.astype(BF16), w_up.astype(BF16), w_down.astype(BF16))


def _combine_body(dest_ref, x1_ref, gt_ref, eo_hbm, out_ref, rbuf, sem, *, n):
    tm = x1_ref.shape[0]
    base = pl.program_id(0) * tm

    def issue(j, c):
        d0 = dest_ref[base + j]
        d1 = dest_ref[n + base + j]
        pltpu.make_async_copy(eo_hbm.at[pl.ds(d0, 1)], rbuf.at[0, pl.ds(j, 1)], sem).start()
        pltpu.make_async_copy(eo_hbm.at[pl.ds(d1, 1)], rbuf.at[1, pl.ds(j, 1)], sem).start()
        return c

    lax.fori_loop(0, tm, issue, 0)
    for k in range(2):
        pltpu.make_async_copy(eo_hbm.at[pl.ds(0, tm)], rbuf.at[k], sem).wait()
    g = gt_ref[...]
    out_ref[...] = x1_ref[...] + (g[:, 0:1] * rbuf[0] + g[:, 1:2] * rbuf[1])


def _combine(x1, gates_t, dest, eo, tm=256):
    n, d = x1.shape
    grid_spec = pltpu.PrefetchScalarGridSpec(
        num_scalar_prefetch=1,
        grid=(n // tm,),
        in_specs=[
            pl.BlockSpec((tm, d), lambda i, dst: (i, 0)),
            pl.BlockSpec((tm, 2), lambda i, dst: (i, 0)),
            pl.BlockSpec(memory_space=pl.ANY),
        ],
        out_specs=pl.BlockSpec((tm, d), lambda i, dst: (i, 0)),
        scratch_shapes=[pltpu.VMEM((2, tm, d), F32), pltpu.SemaphoreType.DMA(())],
    )
    return pl.pallas_call(
        functools.partial(_combine_body, n=n),
        grid_spec=grid_spec,
        out_shape=jax.ShapeDtypeStruct((n, d), F32),
        compiler_params=_cparams(("arbitrary",)),
    )(dest.reshape(-1), x1, gates_t, eo)


def _dispatch_plan(eid, rank, counts, n):
    nk = 2 * n
    p_rows = -(-nk // MOE_BLOCK) * MOE_BLOCK + N_EXPERTS * MOE_BLOCK
    nb = p_rows // MOE_BLOCK
    padded = -(-counts // MOE_BLOCK) * MOE_BLOCK
    pends = jnp.cumsum(padded)
    pstarts = pends - padded
    dest = pstarts[eid] + rank
    tok = jnp.broadcast_to(jnp.arange(n, dtype=I32)[None, :], (2, n))
    buf_tok = jnp.zeros((p_rows,), I32).at[dest.reshape(-1)].set(tok.reshape(-1))
    block_expert = jnp.clip(jnp.searchsorted(pends, jnp.arange(nb, dtype=I32) * MOE_BLOCK, side='right'),
                            0, N_EXPERTS - 1).astype(I32)
    n_used = (pends[-1] // MOE_BLOCK).astype(I32).reshape(1)
    return dest.astype(I32), buf_tok, block_expert, n_used


def _layer(x, ln1_g, w_in, q_g, k_g, rpb, attn_g, conv_w, a_log, dt_bias, dn_g, w_out, ln2_g,
           wg_r, bg_r, we_r, be_r, w_gate, w_up, w_down):
    b, s_len, d = x.shape
    n = b * s_len
    x2 = x.reshape(n, d)
    aqkv, dqkv, dz, dab = _inproj(x2, ln1_g, w_in)
    attn_raw = _attention(aqkv.reshape(b, s_len, -1), q_g, k_g, rpb)
    dn = _deltanet(dqkv.reshape(b, s_len, -1), dz.reshape(b, s_len, -1), dab.reshape(b, s_len, -1),
                   conv_w, a_log, dt_bias, dn_g)
    x1, h2, eid, gates, rank, cnt = _outproj_router(
        x2, attn_raw.reshape(n, -1), dn.reshape(n, -1), attn_g, w_out, ln2_g, wg_r, bg_r, we_r, be_r)
    counts = cnt[:, 0].astype(I32)
    dest, buf_tok, block_expert, n_used = _dispatch_plan(eid, rank, counts, n)
    eo = _experts(h2, buf_tok, block_expert, n_used, w_gate, w_up, w_down)
    out = _combine(x1, gates.T, dest, eo)
    return out.reshape(b, s_len, d)


def kernel(x, ln1_g, w_in, attn_q_norm_g, attn_k_norm_g, attn_rpb, attn_out_norm_g, dn_conv_w, dn_a_log,
           dn_dt_bias, dn_out_norm_g, w_out, ln2_g, router_group_w, router_group_b, router_expert_w,
           router_expert_b, expert_w_gate, expert_w_up, expert_w_down):
    for l in range(ln1_g.shape[0]):
        x = _layer(x, ln1_g[l], w_in[l], attn_q_norm_g[l], attn_k_norm_g[l], attn_rpb[l],
                   attn_out_norm_g[l], dn_conv_w[l], dn_a_log[l], dn_dt_bias[l], dn_out_norm_g[l],
                   w_out[l], ln2_g[l], router_group_w[l], router_group_b[l], router_expert_w[l],
                   router_expert_b[l], expert_w_gate[l], expert_w_up[l], expert_w_down[l])
    return x
```

```python
import functools

import jax
import jax.numpy as jnp
from jax import lax
from jax.experimental import pallas as pl
from jax.experimental.pallas import tpu as pltpu

F32 = jnp.float32
BF16 = jnp.bfloat16
I32 = jnp.int32

EPS = 1e-6
GRID_W = 64
WIN_H = 8
WIN_W = 16
ATTN_HEADS = 8
ATTN_HD = 64
ATTN_WIDTH = ATTN_HEADS * ATTN_HD
DN_HEADS = 4
DN_HD = 128
DN_WIDTH = DN_HEADS * DN_HD
DN_CONV = 5
DN_CHUNK = 64
N_GROUPS = 8
EPG = 8
N_EXPERTS = N_GROUPS * EPG
MOE_BLOCK = 256
NEG = -1e30

LANES = 128
VMEM_LIMIT = 56 * 1024 * 1024


def _cparams(sem):
    return pltpu.CompilerParams(dimension_semantics=sem, vmem_limit_bytes=VMEM_LIMIT)


def _dot(a, b):
    return jnp.dot(a, b, preferred_element_type=F32)


def _dot_nt(a, b):
    return lax.dot_general(a, b, (((1,), (1,)), ((), ())), preferred_element_type=F32)


def _split_hi_lo(x):
    hi = x.astype(BF16)
    lo = (x - hi.astype(F32)).astype(BF16)
    return hi, lo


def _inproj_body(x_ref, g_ref, wa_ref, wd_ref, wz_ref, wab_ref, oa_ref, od_ref, oz_ref, oab_ref):
    x = x_ref[...]
    ms = jnp.mean(x * x, axis=-1, keepdims=True)
    h = x * lax.rsqrt(ms + EPS) * g_ref[...]
    hi, lo = _split_hi_lo(h)
    oa_ref[...] = _dot(hi, wa_ref[...]).astype(BF16)
    od_ref[...] = _dot(hi, wd_ref[...]).astype(BF16)
    oz_ref[...] = _dot(hi, wz_ref[...]).astype(BF16)
    nab = oab_ref.shape[-1]
    ab = _dot(hi, wab_ref[...])
    ab_lo = _dot(lo, wab_ref[:, :nab])
    oab_ref[...] = ab[:, :nab] + ab[:, nab:] + ab_lo


def _inproj(x2, ln1_g, w_in, tm=512):
    n, d = x2.shape
    a3 = 3 * ATTN_WIDTH
    d3 = 3 * DN_WIDTH
    wa = w_in[:, :a3].astype(BF16)
    wd = w_in[:, a3:a3 + d3].astype(BF16)
    wz = w_in[:, a3 + d3:a3 + d3 + DN_WIDTH].astype(BF16)
    wab_f = w_in[:, a3 + d3 + DN_WIDTH:]
    nab = wab_f.shape[1]
    wab_hi, wab_lo = _split_hi_lo(wab_f)
    wab = jnp.concatenate([wab_hi, wab_lo], axis=1)
    const = lambda i: (0, 0)
    row = lambda i: (i, 0)
    return pl.pallas_call(
        _inproj_body,
        grid=(n // tm,),
        in_specs=[
            pl.BlockSpec((tm, d), row),
            pl.BlockSpec((1, d), const),
            pl.BlockSpec((d, a3), const),
            pl.BlockSpec((d, d3), const),
            pl.BlockSpec((d, DN_WIDTH), const),
            pl.BlockSpec((d, 2 * nab), const),
        ],
        out_specs=[
            pl.BlockSpec((tm, a3), row),
            pl.BlockSpec((tm, d3), row),
            pl.BlockSpec((tm, DN_WIDTH), row),
            pl.BlockSpec((tm, nab), row),
        ],
        out_shape=[
            jax.ShapeDtypeStruct((n, a3), BF16),
            jax.ShapeDtypeStruct((n, d3), BF16),
            jax.ShapeDtypeStruct((n, DN_WIDTH), BF16),
            jax.ShapeDtypeStruct((n, nab), F32),
        ],
        compiler_params=_cparams(("parallel",)),
    )(x2, ln1_g.reshape(1, d), wa, wd, wz, wab)


def _attn_bias_table(rpb):
    c = jnp.arange(GRID_W)
    c0 = jnp.clip(c - WIN_W // 2, 0, GRID_W - WIN_W)
    kc = jnp.arange(GRID_W)
    inwin = (kc[None, :] >= c0[:, None]) & (kc[None, :] < c0[:, None] + WIN_W)
    h = rpb.shape[0]
    padw = GRID_W - WIN_W
    rp = jnp.pad(rpb.astype(F32), ((0, 0), (0, 0), (padw, padw)))
    by_col = jnp.stack([rp[:, :, GRID_W - 1 - cc:2 * GRID_W - 1 - cc] for cc in range(GRID_W)], axis=2)
    tab = jnp.stack([by_col[:, WIN_H - 1 - dd:2 * WIN_H - 1 - dd] for dd in range(WIN_H)], axis=1)
    tab = tab.transpose(0, 1, 3, 2, 4)
    tab = jnp.where(inwin[None, None, :, None, :], tab, NEG)
    tab = tab.reshape(h // 2, 2, WIN_H, GRID_W, WIN_H * GRID_W)
    return tab.transpose(0, 2, 1, 3, 4).reshape(h // 2, WIN_H, 2 * GRID_W, WIN_H * GRID_W)


def _attn_body(q_ref, k_ref, v_ref, qg_ref, kg_ref, bias_ref, o_ref, qlo_s, qhi_s, kn_s, *, rows):
    s_len = rows * GRID_W
    lane = lax.broadcasted_iota(I32, (1, LANES), 1)
    is_lo = lane < ATTN_HD
    blk = 256

    def head_norm(x, gain):
        x2 = x * x
        s_lo = jnp.sum(jnp.where(is_lo, x2, 0.0), axis=-1, keepdims=True)
        s_hi = jnp.sum(jnp.where(is_lo, 0.0, x2), axis=-1, keepdims=True)
        ms = jnp.where(is_lo, s_lo, s_hi) * (1.0 / ATTN_HD)
        return x * lax.rsqrt(ms + EPS) * gain

    def prep(i, c):
        sl = pl.ds(pl.multiple_of(i * blk, blk), blk)
        qn = head_norm(q_ref[0, sl, :].astype(F32), qg_ref[...]) * (ATTN_HD ** -0.5)
        qlo_s[sl, :] = jnp.where(is_lo, qn, 0.0).astype(BF16)
        qhi_s[sl, :] = jnp.where(is_lo, 0.0, qn).astype(BF16)
        kn_s[sl, :] = head_norm(k_ref[0, sl, :].astype(F32), kg_ref[...]).astype(BF16)
        return c

    lax.fori_loop(0, s_len // blk, prep, 0)

    band = WIN_H * GRID_W

    def row_step(r, c):
        r0 = jnp.clip(r - WIN_H // 2, 0, rows - WIN_H)
        d = r - r0
        qsl = pl.ds(pl.multiple_of(r * GRID_W, GRID_W), GRID_W)
        ksl = pl.ds(pl.multiple_of(r0 * GRID_W, GRID_W), band)
        q2 = jnp.concatenate([qlo_s[qsl, :], qhi_s[qsl, :]], axis=0)
        s = _dot_nt(q2, kn_s[ksl, :]) + bias_ref[0, d]
        m = jnp.max(s, axis=-1, keepdims=True)
        p = jnp.exp(s - m)
        l = jnp.sum(p, axis=-1, keepdims=True)
        pv = _dot(p.astype(BF16), v_ref[0, ksl, :]) / l
        o = jnp.where(is_lo, pv[:GRID_W], pv[GRID_W:])
        o_ref[0, qsl, :] = o.astype(o_ref.dtype)
        return c

    lax.fori_loop(0, rows, row_step, 0)


def _attention(aqkv, q_g, k_g, rpb):
    b, s_len, _ = aqkv.shape
    rows = s_len // GRID_W
    assert min(WIN_H, rows) == WIN_H
    pairs = ATTN_HEADS // 2
    bias = _attn_bias_table(rpb)
    qg2 = jnp.tile(q_g, 2).reshape(1, LANES).astype(F32)
    kg2 = jnp.tile(k_g, 2).reshape(1, LANES).astype(F32)
    band = WIN_H * GRID_W
    return pl.pallas_call(
        functools.partial(_attn_body, rows=rows),
        grid=(pairs, b),
        in_specs=[
            pl.BlockSpec((1, s_len, LANES), lambda p, i: (i, 0, p)),
            pl.BlockSpec((1, s_len, LANES), lambda p, i: (i, 0, pairs + p)),
            pl.BlockSpec((1, s_len, LANES), lambda p, i: (i, 0, 2 * pairs + p)),
            pl.BlockSpec((1, LANES), lambda p, i: (0, 0)),
            pl.BlockSpec((1, LANES), lambda p, i: (0, 0)),
            pl.BlockSpec((1, WIN_H, 2 * GRID_W, band), lambda p, i: (p, 0, 0, 0)),
        ],
        out_specs=pl.BlockSpec((1, s_len, LANES), lambda p, i: (i, 0, p)),
        out_shape=jax.ShapeDtypeStruct((b, s_len, ATTN_WIDTH), BF16),
        scratch_shapes=[
            pltpu.VMEM((s_len, LANES), BF16),
            pltpu.VMEM((s_len, LANES), BF16),
            pltpu.VMEM((s_len, LANES), BF16),
        ],
        compiler_params=_cparams(("arbitrary", "arbitrary")),
    )(aqkv, aqkv, aqkv, qg2, kg2, bias)


SUPER = 2 * DN_CHUNK
HALO = 16
DN_UNROLL = 4


def _mmb(a, b):
    return _dot(a.astype(BF16), b.astype(BF16))


def _dn_body(q_ref, k_ref, v_ref, z_ref, ab_ref, cwq_ref, cwk_ref, cwv_ref, alog_ref, dtb_ref, og_ref,
             o_ref, pad_s, qs, ks, vs, of_s, ob_s, *, s_len):
    head = pl.program_id(1)
    n_super = s_len // SUPER

    zeros_halo = jnp.zeros((HALO, LANES), F32)
    for t, src in enumerate((q_ref, k_ref, v_ref)):
        pad_s[t, 0:HALO, :] = zeros_halo
        pad_s[t, HALO + s_len:2 * HALO + s_len, :] = zeros_halo

        def stage(i, c, t=t, src=src):
            off = pl.multiple_of(i * 512, 512)
            pad_s[t, pl.ds(HALO + off, 512), :] = src[0, pl.ds(off, 512), :].astype(F32)
            return c

        lax.fori_loop(0, s_len // 512, stage, 0)

    win = SUPER + HALO

    def conv_step(i, c):
        t0 = pl.multiple_of(i * SUPER, SUPER)
        outs = []
        for t, cw_ref in enumerate((cwq_ref, cwk_ref, cwv_ref)):
            xw = pad_s[t, pl.ds(t0 + HALO // 2, win), :]
            acc = jnp.zeros((SUPER, LANES), F32)
            for j in range(DN_CONV):
                sh = (DN_CONV // 2 - j) % win
                xr = xw if sh == 0 else pltpu.roll(xw, sh, axis=0)
                acc = acc + xr[HALO // 2:HALO // 2 + SUPER] * cw_ref[j:j + 1, :]
            outs.append(acc * jax.nn.sigmoid(acc))
        qc, kc, vc = outs
        sl = pl.ds(t0, SUPER)
        qs[sl, :] = qc * lax.rsqrt(jnp.sum(qc * qc, axis=-1, keepdims=True) + EPS) * (DN_HD ** -0.5)
        ks[sl, :] = kc * lax.rsqrt(jnp.sum(kc * kc, axis=-1, keepdims=True) + EPS)
        vs[sl, :] = vc
        return c

    lax.fori_loop(0, n_super, conv_step, 0)

    ri = lax.broadcasted_iota(I32, (SUPER, SUPER), 0)
    ci = lax.broadcasted_iota(I32, (SUPER, SUPER), 1)
    same = (ri // DN_CHUNK) == (ci // DN_CHUNK)
    eye = (ri == ci).astype(F32)
    pos = ri % DN_CHUNK

    def prep(sc, rev):
        t0 = pl.multiple_of(sc * SUPER, SUPER)
        sl = pl.ds(t0, SUPER)
        q = qs[sl, :]
        k = ks[sl, :]
        v = vs[sl, :]
        ab = ab_ref[0, 0, sl, :]
        dcol = 1 if rev else 0
        a_b = jnp.broadcast_to(ab[:, dcol:dcol + 1], (SUPER, LANES))
        b_b = jnp.broadcast_to(ab[:, 2 + dcol:3 + dcol], (SUPER, LANES))
        neg_rate = -jnp.exp(jnp.full((1, LANES), alog_ref[dcol, head], F32))
        g = neg_rate * jax.nn.softplus(a_b + dtb_ref[dcol, head])
        beta = jax.nn.sigmoid(b_b)
        gc = g
        for sft in (1, 2, 4, 8, 16, 32):
            if rev:
                gc = gc + jnp.where(pos < DN_CHUNK - sft, pltpu.roll(gc, SUPER - sft, axis=0), 0.0)
            else:
                gc = gc + jnp.where(pos >= sft, pltpu.roll(gc, sft, axis=0), 0.0)
        gct = gc.T
        diff = gc - gct
        if rev:
            incl = same & (ri <= ci)
            strict = same & (ri < ci)
        else:
            incl = same & (ri >= ci)
            strict = same & (ri > ci)
        decay = jnp.where(incl, jnp.exp(jnp.where(incl, diff, 0.0)), 0.0)
        kb = k * beta
        gram = _dot_nt(jnp.concatenate([q, kb], axis=0).astype(BF16), k.astype(BF16))
        qkd = (gram[:SUPER] * decay).astype(BF16)
        low = jnp.where(strict, gram[SUPER:] * decay, 0.0)
        egc = jnp.exp(gc)
        rhs = jnp.concatenate([v * beta, kb * egc], axis=1).astype(BF16)
        return dict(sl=sl, rev=rev, gc=gc, gct=gct, qkd=qkd, rhs=rhs, qe=q * egc, kt=k.T,
                    inv=eye - low, pw=low.astype(BF16))

    def phase1(chains):
        ctx = [prep(sc, rev) for sc, rev in chains]
        for _ in range(5):
            for c in ctx:
                c['pw'] = _dot(c['pw'], c['pw']).astype(BF16)
            for c in ctx:
                c['inv'] = c['inv'] + _dot(c['inv'].astype(BF16), c['pw'])
        for c in ctx:
            c['sol'] = _dot(c['inv'].astype(BF16), c['rhs']).astype(BF16)
        for c in ctx:
            qo = _dot(c['qkd'], c['sol'])
            c['o0'] = qo[:, :LANES]
            c['qt'] = c['qe'] - qo[:, LANES:]
        out = []
        for c in ctx:
            rev, gc, gct = c['rev'], c['gc'], c['gct']
            per_chunk = []
            for cidx in ((1, 0) if rev else (0, 1)):
                lo = cidx * DN_CHUNK
                last = lo if rev else lo + DN_CHUNK - 1
                g_last = gc[last:last + 1, :]
                in_c = (ci // DN_CHUNK) == cidx
                kst = c['kt'] * jnp.where(in_c, jnp.exp(jnp.where(in_c, g_last - gct, 0.0)), 0.0)
                mn = _dot(kst.astype(BF16), c['sol'])
                lhs = jnp.concatenate([-mn[:, LANES:], c['qt'][lo:lo + DN_CHUNK]], axis=0).astype(BF16)
                per_chunk.append((cidx, lhs, mn[:, :LANES], jnp.exp(g_last), c['o0'][lo:lo + DN_CHUNK]))
            out.append((c['sl'], per_chunk))
        return out

    def scan(state, p1, rev):
        sl, per_chunk = p1
        outs = [None, None]
        for cidx, lhs, ku, e_last, o0_c in per_chunk:
            r = _dot(lhs, state.astype(BF16))
            outs[cidx] = r[LANES:] + o0_c
            state = e_last * state + r[:LANES] + ku
        dst = ob_s if rev else of_s
        dst[sl, :] = jnp.concatenate(outs, axis=0)
        return state

    def scan_step(i, carry):
        s_f, s_b = carry
        chains = []
        for u in range(DN_UNROLL):
            chains += [(i * DN_UNROLL + u, False), (n_super - 1 - (i * DN_UNROLL + u), True)]
        p1 = phase1(chains)
        p_f, p_b = p1[0::2], p1[1::2]
        for u in range(DN_UNROLL):
            s_f = scan(s_f, p_f[u], False)
            s_b = scan(s_b, p_b[u], True)
        return s_f, s_b

    z0 = jnp.zeros((DN_HD, DN_HD), F32)
    lax.fori_loop(0, n_super // DN_UNROLL, scan_step, (z0, z0))

    def fin(i, c):
        sl = pl.ds(pl.multiple_of(i * 512, 512), 512)
        o = of_s[sl, :] + ob_s[sl, :]
        o = o * lax.rsqrt(jnp.mean(o * o, axis=-1, keepdims=True) + EPS) * og_ref[...]
        zf = z_ref[0, sl, :].astype(F32)
        o_ref[0, sl, :] = (o * (zf * jax.nn.sigmoid(zf))).astype(o_ref.dtype)
        return c

    lax.fori_loop(0, s_len // 512, fin, 0)


def _deltanet(dqkv, dz, dab, conv_w, a_log, dt_bias, out_g):
    b, s_len, _ = dqkv.shape
    h = DN_HEADS
    ab = dab.reshape(b, s_len, 2, 2, h).transpose(0, 4, 1, 2, 3).reshape(b, h, s_len, 4)
    cw = conv_w.astype(F32)
    tok = lambda off: (lambda i, j: (i, 0, off + j))
    smem = pl.BlockSpec(memory_space=pltpu.SMEM)
    return pl.pallas_call(
        functools.partial(_dn_body, s_len=s_len),
        grid=(b, h),
        in_specs=[
            pl.BlockSpec((1, s_len, LANES), tok(0)),
            pl.BlockSpec((1, s_len, LANES), tok(h)),
            pl.BlockSpec((1, s_len, LANES), tok(2 * h)),
            pl.BlockSpec((1, s_len, LANES), tok(0)),
            pl.BlockSpec((1, 1, s_len, 4), lambda i, j: (i, j, 0, 0)),
            pl.BlockSpec((DN_CONV, LANES), lambda i, j: (0, j)),
            pl.BlockSpec((DN_CONV, LANES), lambda i, j: (0, h + j)),
            pl.BlockSpec((DN_CONV, LANES), lambda i, j: (0, 2 * h + j)),
            smem,
            smem,
            pl.BlockSpec((1, LANES), lambda i, j: (0, 0)),
        ],
        out_specs=pl.BlockSpec((1, s_len, LANES), tok(0)),
        out_shape=jax.ShapeDtypeStruct((b, s_len, DN_WIDTH), BF16),
        scratch_shapes=[
            pltpu.VMEM((3, s_len + 2 * HALO, LANES), F32),
            pltpu.VMEM((s_len, LANES), F32),
            pltpu.VMEM((s_len, LANES), F32),
            pltpu.VMEM((s_len, LANES), F32),
            pltpu.VMEM((s_len, LANES), F32),
            pltpu.VMEM((s_len, LANES), F32),
        ],
        compiler_params=_cparams(("arbitrary", "arbitrary")),
    )(dqkv, dqkv, dqkv, dz, ab, cw, cw, cw, a_log.astype(F32), dt_bias.astype(F32),
      out_g.reshape(1, LANES).astype(F32))


ROUTER_ROWS = 80


def _outproj_body(x_ref, at_ref, dn_ref, ag_ref, wo_ref, lg_ref, wrh_ref, wrl_ref, rb_ref, su_ref,
                  x1_ref, h2_ref, eid_ref, gate_ref, rank_ref, cnt_ref, base_s):
    step = pl.program_id(0)
    tm = x_ref.shape[0]

    @pl.when(step == 0)
    def _():
        base_s[...] = jnp.zeros_like(base_s)

    a = at_ref[...].astype(F32)
    an = a * lax.rsqrt(jnp.mean(a * a, axis=-1, keepdims=True) + EPS) * ag_ref[...]
    mix = jnp.concatenate([an.astype(BF16), dn_ref[...]], axis=1)
    x1 = x_ref[...] + _dot(mix, wo_ref[...])
    x1_ref[...] = x1
    h2 = x1 * lax.rsqrt(jnp.mean(x1 * x1, axis=-1, keepdims=True) + EPS) * lg_ref[...]
    h2_ref[...] = h2

    hi, lo = _split_hi_lo(h2)
    wrh = wrh_ref[...]
    lt = _dot_nt(wrh, hi) + _dot_nt(wrh, lo) + _dot_nt(wrl_ref[...], hi) + rb_ref[...]

    sub8 = lax.broadcasted_iota(I32, (N_GROUPS, tm), 0)
    gl = lt[0:N_GROUPS]
    gmax = jnp.max(gl, axis=0, keepdims=True)
    gsel = jnp.min(jnp.where(gl == gmax, sub8, N_GROUPS), axis=0, keepdims=True)
    gp = 1.0 / jnp.sum(jnp.exp(gl - gmax), axis=0, keepdims=True)
    in_group = jnp.zeros((EPG, tm), F32)
    for g in range(N_GROUPS):
        in_group = in_group + jnp.where(gsel == g, lt[N_GROUPS + g * EPG:N_GROUPS + (g + 1) * EPG], 0.0)
    v1 = jnp.max(in_group, axis=0, keepdims=True)
    i1 = jnp.min(jnp.where(in_group == v1, sub8, EPG), axis=0, keepdims=True)
    rest = jnp.where(sub8 == i1, -jnp.inf, in_group)
    v2 = jnp.max(rest, axis=0, keepdims=True)
    i2 = jnp.min(jnp.where(rest == v2, sub8, EPG), axis=0, keepdims=True)
    e21 = jnp.exp(v2 - v1)
    den = 1.0 + e21
    eid1 = gsel * EPG + i1
    eid2 = gsel * EPG + i2
    eid_ref[0:1, :] = eid1
    eid_ref[1:2, :] = eid2
    gate_ref[0:1, :] = gp * (1.0 / den)
    gate_ref[1:2, :] = gp * (e21 / den)

    sub64 = lax.broadcasted_iota(I32, (N_EXPERTS, tm), 0)
    oh1 = sub64 == eid1
    oh2 = sub64 == eid2
    ohs = jnp.where(oh1 | oh2, 1.0, 0.0)
    before = base_s[...] + _dot(ohs.astype(BF16), su_ref[...])
    rank_ref[0:1, :] = jnp.sum(jnp.where(oh1, before, 0.0), axis=0, keepdims=True).astype(I32)
    rank_ref[1:2, :] = jnp.sum(jnp.where(oh2, before, 0.0), axis=0, keepdims=True).astype(I32)
    base_new = base_s[...] + jnp.sum(ohs, axis=1, keepdims=True)
    base_s[...] = base_new
    cnt_ref[...] = jnp.broadcast_to(base_new, cnt_ref.shape)


def _outproj_router(x2, attn_raw, dn, attn_g, w_out, ln2_g, wg_r, bg_r, we_r, be_r, tm=512):
    n, d = x2.shape
    wr = jnp.concatenate([wg_r, we_r], axis=1).T.astype(F32)
    wr = jnp.pad(wr, ((0, ROUTER_ROWS - wr.shape[0]), (0, 0)))
    wrh, wrl = _split_hi_lo(wr)
    rb = jnp.pad(jnp.concatenate([bg_r, be_r]).astype(F32), (0, ROUTER_ROWS - N_GROUPS - N_EXPERTS))
    rb = rb.reshape(ROUTER_ROWS, 1)
    t_i = jnp.arange(tm)
    su = (t_i[:, None] < t_i[None, :]).astype(BF16)
    const = lambda i: (0, 0)
    row = lambda i: (i, 0)
    col = lambda i: (0, i)
    return pl.pallas_call(
        _outproj_body,
        grid=(n // tm,),
        in_specs=[
            pl.BlockSpec((tm, d), row),
            pl.BlockSpec((tm, ATTN_WIDTH), row),
            pl.BlockSpec((tm, DN_WIDTH), row),
            pl.BlockSpec((1, ATTN_WIDTH), const),
            pl.BlockSpec((ATTN_WIDTH + DN_WIDTH, d), const),
            pl.BlockSpec((1, d), const),
            pl.BlockSpec((ROUTER_ROWS, d), const),
            pl.BlockSpec((ROUTER_ROWS, d), const),
            pl.BlockSpec((ROUTER_ROWS, 1), const),
            pl.BlockSpec((tm, tm), const),
        ],
        out_specs=[
            pl.BlockSpec((tm, d), row),
            pl.BlockSpec((tm, d), row),
            pl.BlockSpec((2, tm), col),
            pl.BlockSpec((2, tm), col),
            pl.BlockSpec((2, tm), col),
            pl.BlockSpec((N_EXPERTS, LANES), const),
        ],
        out_shape=[
            jax.ShapeDtypeStruct((n, d), F32),
            jax.ShapeDtypeStruct((n, d), F32),
            jax.ShapeDtypeStruct((2, n), I32),
            jax.ShapeDtypeStruct((2, n), F32),
            jax.ShapeDtypeStruct((2, n), I32),
            jax.ShapeDtypeStruct((N_EXPERTS, LANES), F32),
        ],
        scratch_shapes=[pltpu.VMEM((N_EXPERTS, 1), F32)],
        compiler_params=_cparams(("arbitrary",)),
    )(x2, attn_raw, dn, attn_g.reshape(1, -1).astype(F32), w_out.astype(BF16), ln2_g.reshape(1, d).astype(F32),
      wrh, wrl, rb, su)


def _expert_body(be_ref, nu_ref, tok_ref, h_hbm, wg_ref, wu_ref, wd_ref, o_ref, xbuf, sem):
    i = pl.program_id(0)
    used = i < nu_ref[0]

    @pl.when(used)
    def _():
        base = i * MOE_BLOCK

        def issue(j, c):
            t = tok_ref[base + j]
            pltpu.make_async_copy(h_hbm.at[pl.ds(t, 1)], xbuf.at[pl.ds(j, 1)], sem).start()
            return c

        lax.fori_loop(0, MOE_BLOCK, issue, 0)
        pltpu.make_async_copy(h_hbm.at[pl.ds(0, MOE_BLOCK)], xbuf, sem).wait()
        x = xbuf[...].astype(BF16)
        gt = _dot(x, wg_ref[0])
        up = _dot(x, wu_ref[0])
        hid = gt * jax.nn.sigmoid(gt) * up
        o_ref[...] = _dot(hid.astype(BF16), wd_ref[0])

    @pl.when(jnp.logical_not(used))
    def _():
        o_ref[...] = jnp.zeros_like(o_ref)


def _experts(h2, buf_tok, block_expert, n_used, w_gate, w_up, w_down):
    n, d = h2.shape
    nb = block_expert.shape[0]
    de = w_gate.shape[-1]
    wmap = lambda i, be, nu, tok: (be[i], 0, 0)
    grid_spec = pltpu.PrefetchScalarGridSpec(
        num_scalar_prefetch=3,
        grid=(nb,),
        in_specs=[
            pl.BlockSpec(memory_space=pl.ANY),
            pl.BlockSpec((1, d, de), wmap),
            pl.BlockSpec((1, d, de), wmap),
            pl.BlockSpec((1, de, d), wmap),
        ],
        out_specs=pl.BlockSpec((MOE_BLOCK, d), lambda i, be, nu, tok: (i, 0)),
        scratch_shapes=[pltpu.VMEM((MOE_BLOCK, d), F32), pltpu.SemaphoreType.DMA(())],
    )
    return pl.pallas_call(
        _expert_body,
        grid_spec=grid_spec,
        out_shape=jax.ShapeDtypeStruct((nb * MOE_BLOCK, d), F32),
        compiler_params=_cparams(("arbitrary",)),
    )(block_expert, n_used, buf_tok, h2, w_gate.astype(BF16), w_up.astype(BF16), w_down.astype(BF16))


def _combine_body(dest_ref, x1_ref, gt_ref, eo_hbm, out_ref, rbuf, sem, *, n):
    tm = x1_ref.shape[0]
    base = pl.program_id(0) * tm

    def issue(j, c):
        d0 = dest_ref[base + j]
        d1 = dest_ref[n + base + j]
        pltpu.make_async_copy(eo_hbm.at[pl.ds(d0, 1)], rbuf.at[0, pl.ds(j, 1)], sem).start()
        pltpu.make_async_copy(eo_hbm.at[pl.ds(d1, 1)], rbuf.at[1, pl.ds(j, 1)], sem).start()
        return c

    lax.fori_loop(0, tm, issue, 0)
    for k in range(2):
        pltpu.make_async_copy(eo_hbm.at[pl.ds(0, tm)], rbuf.at[k], sem).wait()
    g = gt_ref[...]
    out_ref[...] = x1_ref[...] + (g[:, 0:1] * rbuf[0] + g[:, 1:2] * rbuf[1])


def _combine(x1, gates_t, dest, eo, tm=256):
    n, d = x1.shape
    grid_spec = pltpu.PrefetchScalarGridSpec(
        num_scalar_prefetch=1,
        grid=(n // tm,),
        in_specs=[
            pl.BlockSpec((tm, d), lambda i, dst: (i, 0)),
            pl.BlockSpec((tm, 2), lambda i, dst: (i, 0)),
            pl.BlockSpec(memory_space=pl.ANY),
        ],
        out_specs=pl.BlockSpec((tm, d), lambda i, dst: (i, 0)),
        scratch_shapes=[pltpu.VMEM((2, tm, d), F32), pltpu.SemaphoreType.DMA(())],
    )
    return pl.pallas_call(
        functools.partial(_combine_body, n=n),
        grid_spec=grid_spec,
        out_shape=jax.ShapeDtypeStruct((n, d), F32),
        compiler_params=_cparams(("arbitrary",)),
    )(dest.reshape(-1), x1, gates_t, eo)


def _dispatch_plan(eid, rank, counts, n):
    nk = 2 * n
    p_rows = -(-nk // MOE_BLOCK) * MOE_BLOCK + N_EXPERTS * MOE_BLOCK
    nb = p_rows // MOE_BLOCK
    padded = -(-counts // MOE_BLOCK) * MOE_BLOCK
    pends = jnp.cumsum(padded)
    pstarts = pends - padded
    experts = jnp.arange(N_EXPERTS, dtype=I32)
    dest = jnp.sum(jnp.where(eid[..., None] == experts, pstarts.astype(I32), 0), axis=-1) + rank
    tok = jnp.broadcast_to(jnp.arange(n, dtype=I32)[None, :], (2, n))
    buf_tok = jnp.zeros((p_rows,), I32).at[dest.reshape(-1)].set(tok.reshape(-1))
    block_start = jnp.arange(nb, dtype=I32) * MOE_BLOCK
    block_expert = jnp.clip(jnp.sum((pends[None, :] <= block_start[:, None]).astype(I32), axis=-1),
                            0, N_EXPERTS - 1).astype(I32)
    n_used = (pends[-1] // MOE_BLOCK).astype(I32).reshape(1)
    return dest.astype(I32), buf_tok, block_expert, n_used


def _layer(x, ln1_g, w_in, q_g, k_g, rpb, attn_g, conv_w, a_log, dt_bias, dn_g, w_out, ln2_g,
           wg_r, bg_r, we_r, be_r, w_gate, w_up, w_down):
    b, s_len, d = x.shape
    n = b * s_len
    x2 = x.reshape(n, d)
    aqkv, dqkv, dz, dab = _inproj(x2, ln1_g, w_in)
    attn_raw = _attention(aqkv.reshape(b, s_len, -1), q_g, k_g, rpb)
    dn = _deltanet(dqkv.reshape(b, s_len, -1), dz.reshape(b, s_len, -1), dab.reshape(b, s_len, -1),
                   conv_w, a_log, dt_bias, dn_g)
    x1, h2, eid, gates, rank, cnt = _outproj_router(
        x2, attn_raw.reshape(n, -1), dn.reshape(n, -1), attn_g, w_out, ln2_g, wg_r, bg_r, we_r, be_r)
    counts = cnt[:, 0].astype(I32)
    dest, buf_tok, block_expert, n_used = _dispatch_plan(eid, rank, counts, n)
    eo = _experts(h2, buf_tok, block_expert, n_used, w_gate, w_up, w_down)
    out = _combine(x1, gates.T, dest, eo)
    return out.reshape(b, s_len, d)


def kernel(x, ln1_g, w_in, attn_q_norm_g, attn_k_norm_g, attn_rpb, attn_out_norm_g, dn_conv_w, dn_a_log,
           dn_dt_bias, dn_out_norm_g, w_out, ln2_g, router_group_w, router_group_b, router_expert_w,
           router_expert_b, expert_w_gate, expert_w_up, expert_w_down):
    for l in range(ln1_g.shape[0]):
        x = _layer(x, ln1_g[l], w_in[l], attn_q_norm_g[l], attn_k_norm_g[l], attn_rpb[l],
                   attn_out_norm_g[l], dn_conv_w[l], dn_a_log[l], dn_dt_bias[l], dn_out_norm_g[l],
                   w_out[l], ln2_g[l], router_group_w[l], router_group_b[l], router_expert_w[l],
                   router_expert_b[l], expert_w_gate[l], expert_w_up[l], expert_w_down[l])
    return x
```

```python
import functools

import jax
import jax.numpy as jnp
from jax import lax
from jax.experimental import pallas as pl
from jax.experimental.pallas import tpu as pltpu

F32 = jnp.float32
BF16 = jnp.bfloat16
I32 = jnp.int32

EPS = 1e-6
GRID_W = 64
WIN_H = 8
WIN_W = 16
ATTN_HEADS = 8
ATTN_HD = 64
ATTN_WIDTH = ATTN_HEADS * ATTN_HD
DN_HEADS = 4
DN_HD = 128
DN_WIDTH = DN_HEADS * DN_HD
DN_CONV = 5
DN_CHUNK = 64
N_GROUPS = 8
EPG = 8
N_EXPERTS = N_GROUPS * EPG
MOE_BLOCK = 256
NEG = -1e30

LANES = 128
VMEM_LIMIT = 56 * 1024 * 1024


def _cparams(sem):
    return pltpu.CompilerParams(dimension_semantics=sem, vmem_limit_bytes=VMEM_LIMIT)


def _dot(a, b):
    return jnp.dot(a, b, preferred_element_type=F32)


def _dot_nt(a, b):
    return lax.dot_general(a, b, (((1,), (1,)), ((), ())), preferred_element_type=F32)


def _split_hi_lo(x):
    hi = x.astype(BF16)
    lo = (x - hi.astype(F32)).astype(BF16)
    return hi, lo


def _inproj_body(x_ref, g_ref, wa_ref, wd_ref, wz_ref, wab_ref, oa_ref, od_ref, oz_ref, oab_ref):
    x = x_ref[...]
    ms = jnp.mean(x * x, axis=-1, keepdims=True)
    h = x * lax.rsqrt(ms + EPS) * g_ref[...]
    hi, lo = _split_hi_lo(h)
    oa_ref[...] = _dot(hi, wa_ref[...]).astype(BF16)
    od_ref[...] = _dot(hi, wd_ref[...]).astype(BF16)
    oz_ref[...] = _dot(hi, wz_ref[...]).astype(BF16)
    nab = oab_ref.shape[-1]
    ab = _dot(hi, wab_ref[...])
    ab_lo = _dot(lo, wab_ref[:, :nab])
    oab_ref[...] = ab[:, :nab] + ab[:, nab:] + ab_lo


def _inproj(x2, ln1_g, w_in, tm=512):
    n, d = x2.shape
    a3 = 3 * ATTN_WIDTH
    d3 = 3 * DN_WIDTH
    wa = w_in[:, :a3].astype(BF16)
    wd = w_in[:, a3:a3 + d3].astype(BF16)
    wz = w_in[:, a3 + d3:a3 + d3 + DN_WIDTH].astype(BF16)
    wab_f = w_in[:, a3 + d3 + DN_WIDTH:]
    nab = wab_f.shape[1]
    wab_hi, wab_lo = _split_hi_lo(wab_f)
    wab = jnp.concatenate([wab_hi, wab_lo], axis=1)
    const = lambda i: (0, 0)
    row = lambda i: (i, 0)
    return pl.pallas_call(
        _inproj_body,
        grid=(n // tm,),
        in_specs=[
            pl.BlockSpec((tm, d), row),
            pl.BlockSpec((1, d), const),
            pl.BlockSpec((d, a3), const),
            pl.BlockSpec((d, d3), const),
            pl.BlockSpec((d, DN_WIDTH), const),
            pl.BlockSpec((d, 2 * nab), const),
        ],
        out_specs=[
            pl.BlockSpec((tm, a3), row),
            pl.BlockSpec((tm, d3), row),
            pl.BlockSpec((tm, DN_WIDTH), row),
            pl.BlockSpec((tm, nab), row),
        ],
        out_shape=[
            jax.ShapeDtypeStruct((n, a3), BF16),
            jax.ShapeDtypeStruct((n, d3), BF16),
            jax.ShapeDtypeStruct((n, DN_WIDTH), BF16),
            jax.ShapeDtypeStruct((n, nab), F32),
        ],
        compiler_params=_cparams(("parallel",)),
    )(x2, ln1_g.reshape(1, d), wa, wd, wz, wab)


def _attn_bias_table(rpb):
    c = jnp.arange(GRID_W)
    c0 = jnp.clip(c - WIN_W // 2, 0, GRID_W - WIN_W)
    kc = jnp.arange(GRID_W)
    inwin = (kc[None, :] >= c0[:, None]) & (kc[None, :] < c0[:, None] + WIN_W)
    h = rpb.shape[0]
    padw = GRID_W - WIN_W
    rp = jnp.pad(rpb.astype(F32), ((0, 0), (0, 0), (padw, padw)))
    by_col = jnp.stack([rp[:, :, GRID_W - 1 - cc:2 * GRID_W - 1 - cc] for cc in range(GRID_W)], axis=2)
    tab = jnp.stack([by_col[:, WIN_H - 1 - dd:2 * WIN_H - 1 - dd] for dd in range(WIN_H)], axis=1)
    tab = tab.transpose(0, 1, 3, 2, 4)
    tab = jnp.where(inwin[None, None, :, None, :], tab, NEG)
    tab = tab.reshape(h // 2, 2, WIN_H, GRID_W, WIN_H * GRID_W)
    return tab.transpose(0, 2, 1, 3, 4).reshape(h // 2, WIN_H, 2 * GRID_W, WIN_H * GRID_W)


def _attn_body(q_ref, k_ref, v_ref, qg_ref, kg_ref, bias_ref, o_ref, qlo_s, qhi_s, kn_s, *, rows):
    s_len = rows * GRID_W
    lane = lax.broadcasted_iota(I32, (1, LANES), 1)
    is_lo = lane < ATTN_HD
    blk = 256

    def head_norm(x, gain):
        x2 = x * x
        s_lo = jnp.sum(jnp.where(is_lo, x2, 0.0), axis=-1, keepdims=True)
        s_hi = jnp.sum(jnp.where(is_lo, 0.0, x2), axis=-1, keepdims=True)
        ms = jnp.where(is_lo, s_lo, s_hi) * (1.0 / ATTN_HD)
        return x * lax.rsqrt(ms + EPS) * gain

    def prep(i, c):
        sl = pl.ds(pl.multiple_of(i * blk, blk), blk)
        qn = head_norm(q_ref[0, sl, :].astype(F32), qg_ref[...]) * (ATTN_HD ** -0.5)
        qlo_s[sl, :] = jnp.where(is_lo, qn, 0.0).astype(BF16)
        qhi_s[sl, :] = jnp.where(is_lo, 0.0, qn).astype(BF16)
        kn_s[sl, :] = head_norm(k_ref[0, sl, :].astype(F32), kg_ref[...]).astype(BF16)
        return c

    lax.fori_loop(0, s_len // blk, prep, 0)

    band = WIN_H * GRID_W

    def row_step(r, c):
        r0 = jnp.clip(r - WIN_H // 2, 0, rows - WIN_H)
        d = r - r0
        qsl = pl.ds(pl.multiple_of(r * GRID_W, GRID_W), GRID_W)
        ksl = pl.ds(pl.multiple_of(r0 * GRID_W, GRID_W), band)
        q2 = jnp.concatenate([qlo_s[qsl, :], qhi_s[qsl, :]], axis=0)
        s = _dot_nt(q2, kn_s[ksl, :]) + bias_ref[0, d]
        m = jnp.max(s, axis=-1, keepdims=True)
        p = jnp.exp(s - m)
        l = jnp.sum(p, axis=-1, keepdims=True)
        pv = _dot(p.astype(BF16), v_ref[0, ksl, :]) / l
        o = jnp.where(is_lo, pv[:GRID_W], pv[GRID_W:])
        o_ref[0, qsl, :] = o.astype(o_ref.dtype)
        return c

    lax.fori_loop(0, rows, row_step, 0)


def _attention(aqkv, q_g, k_g, rpb):
    b, s_len, _ = aqkv.shape
    rows = s_len // GRID_W
    assert min(WIN_H, rows) == WIN_H
    pairs = ATTN_HEADS // 2
    bias = _attn_bias_table(rpb)
    qg2 = jnp.tile(q_g, 2).reshape(1, LANES).astype(F32)
    kg2 = jnp.tile(k_g, 2).reshape(1, LANES).astype(F32)
    band = WIN_H * GRID_W
    return pl.pallas_call(
        functools.partial(_attn_body, rows=rows),
        grid=(pairs, b),
        in_specs=[
            pl.BlockSpec((1, s_len, LANES), lambda p, i: (i, 0, p)),
            pl.BlockSpec((1, s_len, LANES), lambda p, i: (i, 0, pairs + p)),
            pl.BlockSpec((1, s_len, LANES), lambda p, i: (i, 0, 2 * pairs + p)),
            pl.BlockSpec((1, LANES), lambda p, i: (0, 0)),
            pl.BlockSpec((1, LANES), lambda p, i: (0, 0)),
            pl.BlockSpec((1, WIN_H, 2 * GRID_W, band), lambda p, i: (p, 0, 0, 0)),
        ],
        out_specs=pl.BlockSpec((1, s_len, LANES), lambda p, i: (i, 0, p)),
        out_shape=jax.ShapeDtypeStruct((b, s_len, ATTN_WIDTH), BF16),
        scratch_shapes=[
            pltpu.VMEM((s_len, LANES), BF16),
            pltpu.VMEM((s_len, LANES), BF16),
            pltpu.VMEM((s_len, LANES), BF16),
        ],
        compiler_params=_cparams(("arbitrary", "arbitrary")),
    )(aqkv, aqkv, aqkv, qg2, kg2, bias)


SUPER = 2 * DN_CHUNK
HALO = 16
DN_UNROLL = 4


def _dn_body(q_ref, k_ref, v_ref, z_ref, ab_ref, cwq_ref, cwk_ref, cwv_ref, alog_ref, dtb_ref, og_ref,
             o_ref, pad_s, qs, ks, vs, of_s, ob_s, *, s_len):
    head = pl.program_id(1)
    n_super = s_len // SUPER

    zeros_halo = jnp.zeros((HALO, LANES), F32)
    for t, src in enumerate((q_ref, k_ref, v_ref)):
        pad_s[t, 0:HALO, :] = zeros_halo
        pad_s[t, HALO + s_len:2 * HALO + s_len, :] = zeros_halo

        def stage(i, c, t=t, src=src):
            off = pl.multiple_of(i * 512, 512)
            pad_s[t, pl.ds(HALO + off, 512), :] = src[0, pl.ds(off, 512), :].astype(F32)
            return c

        lax.fori_loop(0, s_len // 512, stage, 0)

    win = SUPER + HALO

    def conv_step(i, c):
        t0 = pl.multiple_of(i * SUPER, SUPER)
        outs = []
        for t, cw_ref in enumerate((cwq_ref, cwk_ref, cwv_ref)):
            xw = pad_s[t, pl.ds(t0 + HALO // 2, win), :]
            acc = jnp.zeros((SUPER, LANES), F32)
            for j in range(DN_CONV):
                sh = (DN_CONV // 2 - j) % win
                xr = xw if sh == 0 else pltpu.roll(xw, sh, axis=0)
                acc = acc + xr[HALO // 2:HALO // 2 + SUPER] * cw_ref[j:j + 1, :]
            outs.append(acc * jax.nn.sigmoid(acc))
        qc, kc, vc = outs
        sl = pl.ds(t0, SUPER)
        qs[sl, :] = qc * lax.rsqrt(jnp.sum(qc * qc, axis=-1, keepdims=True) + EPS) * (DN_HD ** -0.5)
        ks[sl, :] = kc * lax.rsqrt(jnp.sum(kc * kc, axis=-1, keepdims=True) + EPS)
        vs[sl, :] = vc
        return c

    lax.fori_loop(0, n_super, conv_step, 0)

    ri = lax.broadcasted_iota(I32, (SUPER, SUPER), 0)
    ci = lax.broadcasted_iota(I32, (SUPER, SUPER), 1)
    same = (ri // DN_CHUNK) == (ci // DN_CHUNK)
    eye = (ri == ci).astype(F32)
    pos = ri % DN_CHUNK

    def prep(sc, rev):
        t0 = pl.multiple_of(sc * SUPER, SUPER)
        sl = pl.ds(t0, SUPER)
        q = qs[sl, :]
        k = ks[sl, :]
        v = vs[sl, :]
        ab = ab_ref[0, 0, sl, :]
        dcol = 1 if rev else 0
        a_b = jnp.broadcast_to(ab[:, dcol:dcol + 1], (SUPER, LANES))
        b_b = jnp.broadcast_to(ab[:, 2 + dcol:3 + dcol], (SUPER, LANES))
        neg_rate = -jnp.exp(jnp.full((1, LANES), alog_ref[dcol, head], F32))
        g = neg_rate * jax.nn.softplus(a_b + dtb_ref[dcol, head])
        beta = jax.nn.sigmoid(b_b)
        gc = g
        for sft in (1, 2, 4, 8, 16, 32):
            if rev:
                gc = gc + jnp.where(pos < DN_CHUNK - sft, pltpu.roll(gc, SUPER - sft, axis=0), 0.0)
            else:
                gc = gc + jnp.where(pos >= sft, pltpu.roll(gc, sft, axis=0), 0.0)
        gct = gc.T
        diff = gc - gct
        if rev:
            incl = same & (ri <= ci)
            strict = same & (ri < ci)
        else:
            incl = same & (ri >= ci)
            strict = same & (ri > ci)
        decay = jnp.where(incl, jnp.exp(jnp.where(incl, diff, 0.0)), 0.0)
        kb = k * beta
        gram = _dot_nt(jnp.concatenate([q, kb], axis=0).astype(BF16), k.astype(BF16))
        qkd = (gram[:SUPER] * decay).astype(BF16)
        low = jnp.where(strict, gram[SUPER:] * decay, 0.0)
        egc = jnp.exp(gc)
        rhs = jnp.concatenate([v * beta, kb * egc], axis=1).astype(BF16)
        return dict(sl=sl, rev=rev, gc=gc, gct=gct, qkd=qkd, rhs=rhs, qe=q * egc, kt=k.T,
                    inv=eye - low, pw=low.astype(BF16))

    def phase1(chains):
        ctx = [prep(sc, rev) for sc, rev in chains]
        for _ in range(5):
            for c in ctx:
                c['pw'] = _dot(c['pw'], c['pw']).astype(BF16)
            for c in ctx:
                c['inv'] = c['inv'] + _dot(c['inv'].astype(BF16), c['pw'])
        for c in ctx:
            c['sol'] = _dot(c['inv'].astype(BF16), c['rhs']).astype(BF16)
        for c in ctx:
            qo = _dot(c['qkd'], c['sol'])
            c['o0'] = qo[:, :LANES]
            c['qt'] = c['qe'] - qo[:, LANES:]
        out = []
        for c in ctx:
            rev, gc, gct = c['rev'], c['gc'], c['gct']
            per_chunk = []
            for cidx in ((1, 0) if rev else (0, 1)):
                lo = cidx * DN_CHUNK
                last = lo if rev else lo + DN_CHUNK - 1
                g_last = gc[last:last + 1, :]
                in_c = (ci // DN_CHUNK) == cidx
                kst = c['kt'] * jnp.where(in_c, jnp.exp(jnp.where(in_c, g_last - gct, 0.0)), 0.0)
                mn = _dot(kst.astype(BF16), c['sol'])
                lhs = jnp.concatenate([-mn[:, LANES:], c['qt'][lo:lo + DN_CHUNK]], axis=0).astype(BF16)
                per_chunk.append((cidx, lhs, mn[:, :LANES], jnp.exp(g_last), c['o0'][lo:lo + DN_CHUNK]))
            out.append((c['sl'], per_chunk))
        return out

    def scan(state, p1, rev):
        sl, per_chunk = p1
        outs = [None, None]
        for cidx, lhs, ku, e_last, o0_c in per_chunk:
            r = _dot(lhs, state.astype(BF16))
            outs[cidx] = r[LANES:] + o0_c
            state = e_last * state + r[:LANES] + ku
        dst = ob_s if rev else of_s
        dst[sl, :] = jnp.concatenate(outs, axis=0)
        return state

    def scan_step(i, carry):
        s_f, s_b = carry
        chains = []
        for u in range(DN_UNROLL):
            chains += [(i * DN_UNROLL + u, False), (n_super - 1 - (i * DN_UNROLL + u), True)]
        p1 = phase1(chains)
        p_f, p_b = p1[0::2], p1[1::2]
        for u in range(DN_UNROLL):
            s_f = scan(s_f, p_f[u], False)
            s_b = scan(s_b, p_b[u], True)
        return s_f, s_b

    z0 = jnp.zeros((DN_HD, DN_HD), F32)
    lax.fori_loop(0, n_super // DN_UNROLL, scan_step, (z0, z0))

    def fin(i, c):
        sl = pl.ds(pl.multiple_of(i * 512, 512), 512)
        o = of_s[sl, :] + ob_s[sl, :]
        o = o * lax.rsqrt(jnp.mean(o * o, axis=-1, keepdims=True) + EPS) * og_ref[...]
        zf = z_ref[0, sl, :].astype(F32)
        o_ref[0, sl, :] = (o * (zf * jax.nn.sigmoid(zf))).astype(o_ref.dtype)
        return c

    lax.fori_loop(0, s_len // 512, fin, 0)


def _deltanet(dqkv, dz, dab, conv_w, a_log, dt_bias, out_g):
    b, s_len, _ = dqkv.shape
    h = DN_HEADS
    ab = dab.reshape(b, s_len, 2, 2, h).transpose(0, 4, 1, 2, 3).reshape(b, h, s_len, 4)
    cw = conv_w.astype(F32)
    tok = lambda off: (lambda i, j: (i, 0, off + j))
    smem = pl.BlockSpec(memory_space=pltpu.SMEM)
    return pl.pallas_call(
        functools.partial(_dn_body, s_len=s_len),
        grid=(b, h),
        in_specs=[
            pl.BlockSpec((1, s_len, LANES), tok(0)),
            pl.BlockSpec((1, s_len, LANES), tok(h)),
            pl.BlockSpec((1, s_len, LANES), tok(2 * h)),
            pl.BlockSpec((1, s_len, LANES), tok(0)),
            pl.BlockSpec((1, 1, s_len, 4), lambda i, j: (i, j, 0, 0)),
            pl.BlockSpec((DN_CONV, LANES), lambda i, j: (0, j)),
            pl.BlockSpec((DN_CONV, LANES), lambda i, j: (0, h + j)),
            pl.BlockSpec((DN_CONV, LANES), lambda i, j: (0, 2 * h + j)),
            smem,
            smem,
            pl.BlockSpec((1, LANES), lambda i, j: (0, 0)),
        ],
        out_specs=pl.BlockSpec((1, s_len, LANES), tok(0)),
        out_shape=jax.ShapeDtypeStruct((b, s_len, DN_WIDTH), BF16),
        scratch_shapes=[
            pltpu.VMEM((3, s_len + 2 * HALO, LANES), F32),
            pltpu.VMEM((s_len, LANES), F32),
            pltpu.VMEM((s_len, LANES), F32),
            pltpu.VMEM((s_len, LANES), F32),
            pltpu.VMEM((s_len, LANES), F32),
            pltpu.VMEM((s_len, LANES), F32),
        ],
        compiler_params=_cparams(("arbitrary", "arbitrary")),
    )(dqkv, dqkv, dqkv, dz, ab, cw, cw, cw, a_log.astype(F32), dt_bias.astype(F32),
      out_g.reshape(1, LANES).astype(F32))


ROUTER_ROWS = 80
TOK_SUB = 8


def _outproj_body(x_ref, at_ref, dn_ref, ag_ref, wo_ref, lg_ref, wrh_ref, wrl_ref, rb_ref, su_ref,
                  x1_ref, h2_ref, eid_ref, gate_ref, rank_ref, cnt_ref, base_s):
    step = pl.program_id(0)
    tm = x_ref.shape[0]

    @pl.when(step == 0)
    def _():
        base_s[...] = jnp.zeros_like(base_s)

    a = at_ref[...].astype(F32)
    an = a * lax.rsqrt(jnp.mean(a * a, axis=-1, keepdims=True) + EPS) * ag_ref[...]
    mix = jnp.concatenate([an.astype(BF16), dn_ref[...]], axis=1)
    x1 = x_ref[...] + _dot(mix, wo_ref[...])
    x1_ref[...] = x1
    h2 = x1 * lax.rsqrt(jnp.mean(x1 * x1, axis=-1, keepdims=True) + EPS) * lg_ref[...]
    for s in range(TOK_SUB):
        h2_ref[:, s, :] = h2[:, s * LANES:(s + 1) * LANES]

    hi, lo = _split_hi_lo(h2)
    wrh = wrh_ref[...]
    lt = _dot_nt(wrh, hi) + _dot_nt(wrh, lo) + _dot_nt(wrl_ref[...], hi) + rb_ref[...]

    sub8 = lax.broadcasted_iota(I32, (N_GROUPS, tm), 0)
    gl = lt[0:N_GROUPS]
    gmax = jnp.max(gl, axis=0, keepdims=True)
    gsel = jnp.min(jnp.where(gl == gmax, sub8, N_GROUPS), axis=0, keepdims=True)
    gp = 1.0 / jnp.sum(jnp.exp(gl - gmax), axis=0, keepdims=True)
    in_group = jnp.zeros((EPG, tm), F32)
    for g in range(N_GROUPS):
        in_group = in_group + jnp.where(gsel == g, lt[N_GROUPS + g * EPG:N_GROUPS + (g + 1) * EPG], 0.0)
    v1 = jnp.max(in_group, axis=0, keepdims=True)
    i1 = jnp.min(jnp.where(in_group == v1, sub8, EPG), axis=0, keepdims=True)
    rest = jnp.where(sub8 == i1, -jnp.inf, in_group)
    v2 = jnp.max(rest, axis=0, keepdims=True)
    i2 = jnp.min(jnp.where(rest == v2, sub8, EPG), axis=0, keepdims=True)
    e21 = jnp.exp(v2 - v1)
    den = 1.0 + e21
    eid1 = gsel * EPG + i1
    eid2 = gsel * EPG + i2
    eid_ref[0:1, :] = eid1
    eid_ref[1:2, :] = eid2
    gate_ref[0:1, :] = gp * (1.0 / den)
    gate_ref[1:2, :] = gp * (e21 / den)

    sub64 = lax.broadcasted_iota(I32, (N_EXPERTS, tm), 0)
    oh1 = sub64 == eid1
    oh2 = sub64 == eid2
    ohs = jnp.where(oh1 | oh2, 1.0, 0.0)
    before = base_s[...] + _dot(ohs.astype(BF16), su_ref[...])
    rank_ref[0:1, :] = jnp.sum(jnp.where(oh1, before, 0.0), axis=0, keepdims=True).astype(I32)
    rank_ref[1:2, :] = jnp.sum(jnp.where(oh2, before, 0.0), axis=0, keepdims=True).astype(I32)
    base_new = base_s[...] + jnp.sum(ohs, axis=1, keepdims=True)
    base_s[...] = base_new
    cnt_ref[...] = jnp.broadcast_to(base_new, cnt_ref.shape)


def _outproj_router(x2, attn_raw, dn, attn_g, w_out, ln2_g, wg_r, bg_r, we_r, be_r, tm=512):
    n, d = x2.shape
    wr = jnp.concatenate([wg_r, we_r], axis=1).T.astype(F32)
    wr = jnp.pad(wr, ((0, ROUTER_ROWS - wr.shape[0]), (0, 0)))
    wrh, wrl = _split_hi_lo(wr)
    rb = jnp.pad(jnp.concatenate([bg_r, be_r]).astype(F32), (0, ROUTER_ROWS - N_GROUPS - N_EXPERTS))
    rb = rb.reshape(ROUTER_ROWS, 1)
    t_i = jnp.arange(tm)
    su = (t_i[:, None] < t_i[None, :]).astype(BF16)
    const = lambda i: (0, 0)
    row = lambda i: (i, 0)
    col = lambda i: (0, i)
    return pl.pallas_call(
        _outproj_body,
        grid=(n // tm,),
        in_specs=[
            pl.BlockSpec((tm, d), row),
            pl.BlockSpec((tm, ATTN_WIDTH), row),
            pl.BlockSpec((tm, DN_WIDTH), row),
            pl.BlockSpec((1, ATTN_WIDTH), const),
            pl.BlockSpec((ATTN_WIDTH + DN_WIDTH, d), const),
            pl.BlockSpec((1, d), const),
            pl.BlockSpec((ROUTER_ROWS, d), const),
            pl.BlockSpec((ROUTER_ROWS, d), const),
            pl.BlockSpec((ROUTER_ROWS, 1), const),
            pl.BlockSpec((tm, tm), const),
        ],
        out_specs=[
            pl.BlockSpec((tm, d), row),
            pl.BlockSpec((tm, TOK_SUB, LANES), lambda i: (i, 0, 0)),
            pl.BlockSpec((2, tm), col),
            pl.BlockSpec((2, tm), col),
            pl.BlockSpec((2, tm), col),
            pl.BlockSpec((N_EXPERTS, LANES), const),
        ],
        out_shape=[
            jax.ShapeDtypeStruct((n, d), F32),
            jax.ShapeDtypeStruct((n, TOK_SUB, LANES), F32),
            jax.ShapeDtypeStruct((2, n), I32),
            jax.ShapeDtypeStruct((2, n), F32),
            jax.ShapeDtypeStruct((2, n), I32),
            jax.ShapeDtypeStruct((N_EXPERTS, LANES), F32),
        ],
        scratch_shapes=[pltpu.VMEM((N_EXPERTS, 1), F32)],
        compiler_params=_cparams(("arbitrary",)),
    )(x2, attn_raw, dn, attn_g.reshape(1, -1).astype(F32), w_out.astype(BF16), ln2_g.reshape(1, d).astype(F32),
      wrh, wrl, rb, su)


def _dispatch_body(dest_ref, zrow_ref, h_ref, xs_hbm, zbuf, zsem, sem, *, n):
    tm = h_ref.shape[0]
    base = pl.program_id(0) * tm

    @pl.when(pl.program_id(0) == 0)
    def _():
        zbuf[...] = jnp.zeros_like(zbuf)

        def zero_copy(e):
            return pltpu.make_async_copy(zbuf, xs_hbm.at[pl.ds(zrow_ref[e], MOE_BLOCK)], zsem)

        def start(e, c):
            @pl.when(zrow_ref[e] >= 0)
            def _():
                zero_copy(e).start()
            return c

        def finish(e, c):
            @pl.when(zrow_ref[e] >= 0)
            def _():
                zero_copy(e).wait()
            return c

        lax.fori_loop(0, N_EXPERTS, start, 0)
        lax.fori_loop(0, N_EXPERTS, finish, 0)

        def tail_copy(b):
            return pltpu.make_async_copy(zbuf, xs_hbm.at[pl.ds(b * MOE_BLOCK, MOE_BLOCK)], zsem)

        def tail_start(b, c):
            tail_copy(b).start()
            return c

        def tail_finish(b, c):
            tail_copy(b).wait()
            return c

        n_blocks = xs_hbm.shape[0] // MOE_BLOCK
        lax.fori_loop(zrow_ref[N_EXPERTS], n_blocks, tail_start, 0)
        lax.fori_loop(zrow_ref[N_EXPERTS], n_blocks, tail_finish, 0)

    def issue(j, c):
        src = h_ref.at[pl.ds(j, 1)]
        pltpu.make_async_copy(src, xs_hbm.at[pl.ds(dest_ref[base + j], 1)], sem).start()
        pltpu.make_async_copy(src, xs_hbm.at[pl.ds(dest_ref[n + base + j], 1)], sem).start()
        return c

    lax.fori_loop(0, tm, issue, 0)
    for _ in range(2):
        pltpu.make_async_copy(h_ref, xs_hbm.at[pl.ds(0, tm)], sem).wait()


def _dispatch(h2t, dest, zrow, p_rows, tm=512):
    n = h2t.shape[0]
    grid_spec = pltpu.PrefetchScalarGridSpec(
        num_scalar_prefetch=2,
        grid=(n // tm,),
        in_specs=[pl.BlockSpec((tm, TOK_SUB, LANES), lambda i, dst, zr: (i, 0, 0))],
        out_specs=pl.BlockSpec(memory_space=pl.ANY),
        scratch_shapes=[pltpu.VMEM((MOE_BLOCK, TOK_SUB, LANES), F32), pltpu.SemaphoreType.DMA(()),
                        pltpu.SemaphoreType.DMA(())],
    )
    return pl.pallas_call(
        functools.partial(_dispatch_body, n=n),
        grid_spec=grid_spec,
        out_shape=jax.ShapeDtypeStruct((p_rows, TOK_SUB, LANES), F32),
        compiler_params=_cparams(("arbitrary",)),
    )(dest.reshape(-1), zrow, h2t)


def _expert_body(be_ref, nu_ref, xs_ref, wg_ref, wu_ref, wd_ref, o_ref):
    used = pl.program_id(0) < nu_ref[0]

    @pl.when(used)
    def _():
        x = jnp.concatenate([xs_ref[:, s, :] for s in range(TOK_SUB)], axis=1).astype(BF16)
        gt = _dot(x, wg_ref[0])
        up = _dot(x, wu_ref[0])
        hid = gt * jax.nn.sigmoid(gt) * up
        out = _dot(hid.astype(BF16), wd_ref[0])
        for s in range(TOK_SUB):
            o_ref[:, s, :] = out[:, s * LANES:(s + 1) * LANES]

    @pl.when(jnp.logical_not(used))
    def _():
        o_ref[...] = jnp.zeros_like(o_ref)


def _experts(xs, block_expert, n_used, w_gate, w_up, w_down):
    p_rows = xs.shape[0]
    nb = p_rows // MOE_BLOCK
    d, de = w_gate.shape[-2:]
    wmap = lambda i, be, nu: (be[i], 0, 0)
    xmap = lambda i, be, nu: (jnp.minimum(i, nu[0] - 1), 0, 0)
    grid_spec = pltpu.PrefetchScalarGridSpec(
        num_scalar_prefetch=2,
        grid=(nb,),
        in_specs=[
            pl.BlockSpec((MOE_BLOCK, TOK_SUB, LANES), xmap),
            pl.BlockSpec((1, d, de), wmap),
            pl.BlockSpec((1, d, de), wmap),
            pl.BlockSpec((1, de, d), wmap),
        ],
        out_specs=pl.BlockSpec((MOE_BLOCK, TOK_SUB, LANES), lambda i, be, nu: (i, 0, 0)),
    )
    return pl.pallas_call(
        _expert_body,
        grid_spec=grid_spec,
        out_shape=jax.ShapeDtypeStruct((p_rows, TOK_SUB, LANES), F32),
        compiler_params=_cparams(("arbitrary",)),
    )(block_expert, n_used, xs, w_gate.astype(BF16), w_up.astype(BF16), w_down.astype(BF16))


def _combine_body(dest_ref, x1_ref, gt_ref, eo_hbm, out_ref, rbuf, sem, *, n):
    tm = x1_ref.shape[0]
    base = pl.program_id(0) * tm

    def issue(j, c):
        d0 = dest_ref[base + j]
        d1 = dest_ref[n + base + j]
        pltpu.make_async_copy(eo_hbm.at[pl.ds(d0, 1)], rbuf.at[0, pl.ds(j, 1)], sem).start()
        pltpu.make_async_copy(eo_hbm.at[pl.ds(d1, 1)], rbuf.at[1, pl.ds(j, 1)], sem).start()
        return c

    lax.fori_loop(0, tm, issue, 0)
    for k in range(2):
        pltpu.make_async_copy(eo_hbm.at[pl.ds(0, tm)], rbuf.at[k], sem).wait()
    g0 = jnp.broadcast_to(gt_ref[:, 0:1], (tm, LANES))
    g1 = jnp.broadcast_to(gt_ref[:, 1:2], (tm, LANES))
    for s in range(TOK_SUB):
        cols = slice(s * LANES, (s + 1) * LANES)
        out_ref[:, cols] = x1_ref[:, cols] + (g0 * rbuf[0, :, s, :] + g1 * rbuf[1, :, s, :])


def _combine(x1, gates_t, dest, eo, tm=256):
    n, d = x1.shape
    grid_spec = pltpu.PrefetchScalarGridSpec(
        num_scalar_prefetch=1,
        grid=(n // tm,),
        in_specs=[
            pl.BlockSpec((tm, d), lambda i, dst: (i, 0)),
            pl.BlockSpec((tm, 2), lambda i, dst: (i, 0)),
            pl.BlockSpec(memory_space=pl.ANY),
        ],
        out_specs=pl.BlockSpec((tm, d), lambda i, dst: (i, 0)),
        scratch_shapes=[pltpu.VMEM((2, tm, TOK_SUB, LANES), F32), pltpu.SemaphoreType.DMA(())],
    )
    return pl.pallas_call(
        functools.partial(_combine_body, n=n),
        grid_spec=grid_spec,
        out_shape=jax.ShapeDtypeStruct((n, d), F32),
        compiler_params=_cparams(("arbitrary",)),
    )(dest.reshape(-1), x1, gates_t, eo)


def _dispatch_plan(eid, rank, counts, n):
    nk = 2 * n
    p_rows = -(-nk // MOE_BLOCK) * MOE_BLOCK + N_EXPERTS * MOE_BLOCK
    nb = p_rows // MOE_BLOCK
    padded = -(-counts // MOE_BLOCK) * MOE_BLOCK
    pends = jnp.cumsum(padded)
    pstarts = pends - padded
    experts = jnp.arange(N_EXPERTS, dtype=I32)
    dest = jnp.sum(jnp.where(eid[..., None] == experts, pstarts.astype(I32), 0), axis=-1) + rank
    block_start = jnp.arange(nb, dtype=I32) * MOE_BLOCK
    block_expert = jnp.clip(jnp.sum((pends[None, :] <= block_start[:, None]).astype(I32), axis=-1),
                            0, N_EXPERTS - 1).astype(I32)
    n_used = (pends[-1] // MOE_BLOCK).astype(I32).reshape(1)
    zrow = jnp.concatenate([jnp.where(padded > 0, pends - MOE_BLOCK, -1).astype(I32), n_used])
    return dest.astype(I32), zrow, block_expert, n_used, p_rows


def _layer(x, ln1_g, w_in, q_g, k_g, rpb, attn_g, conv_w, a_log, dt_bias, dn_g, w_out, ln2_g,
           wg_r, bg_r, we_r, be_r, w_gate, w_up, w_down):
    b, s_len, d = x.shape
    n = b * s_len
    x2 = x.reshape(n, d)
    aqkv, dqkv, dz, dab = _inproj(x2, ln1_g, w_in)
    attn_raw = _attention(aqkv.reshape(b, s_len, -1), q_g, k_g, rpb)
    dn = _deltanet(dqkv.reshape(b, s_len, -1), dz.reshape(b, s_len, -1), dab.reshape(b, s_len, -1),
                   conv_w, a_log, dt_bias, dn_g)
    x1, h2t, eid, gates, rank, cnt = _outproj_router(
        x2, attn_raw.reshape(n, -1), dn.reshape(n, -1), attn_g, w_out, ln2_g, wg_r, bg_r, we_r, be_r)
    counts = cnt[:, 0].astype(I32)
    dest, zrow, block_expert, n_used, p_rows = _dispatch_plan(eid, rank, counts, n)
    xs = _dispatch(h2t, dest, zrow, p_rows)
    eo = _experts(xs, block_expert, n_used, w_gate, w_up, w_down)
    out = _combine(x1, gates.T, dest, eo)
    return out.reshape(b, s_len, d)


def kernel(x, ln1_g, w_in, attn_q_norm_g, attn_k_norm_g, attn_rpb, attn_out_norm_g, dn_conv_w, dn_a_log,
           dn_dt_bias, dn_out_norm_g, w_out, ln2_g, router_group_w, router_group_b, router_expert_w,
           router_expert_b, expert_w_gate, expert_w_up, expert_w_down):
    for l in range(ln1_g.shape[0]):
        x = _layer(x, ln1_g[l], w_in[l], attn_q_norm_g[l], attn_k_norm_g[l], attn_rpb[l],
                   attn_out_norm_g[l], dn_conv_w[l], dn_a_log[l], dn_dt_bias[l], dn_out_norm_g[l],
                   w_out[l], ln2_g[l], router_group_w[l], router_group_b[l], router_expert_w[l],
                   router_expert_b[l], expert_w_gate[l], expert_w_up[l], expert_w_down[l])
    return x
```

```python
import functools

import jax
import jax.numpy as jnp
from jax import lax
from jax.experimental import pallas as pl
from jax.experimental.pallas import tpu as pltpu

F32 = jnp.float32
BF16 = jnp.bfloat16
I32 = jnp.int32

EPS = 1e-6
GRID_W = 64
WIN_H = 8
WIN_W = 16
ATTN_HEADS = 8
ATTN_HD = 64
ATTN_WIDTH = ATTN_HEADS * ATTN_HD
DN_HEADS = 4
DN_HD = 128
DN_WIDTH = DN_HEADS * DN_HD
DN_CONV = 5
DN_CHUNK = 64
N_GROUPS = 8
EPG = 8
N_EXPERTS = N_GROUPS * EPG
MOE_BLOCK = 256
NEG = -1e30

ATTN_UNROLL = 8
LANES = 128
VMEM_LIMIT = 56 * 1024 * 1024


def _cparams(sem):
    return pltpu.CompilerParams(dimension_semantics=sem, vmem_limit_bytes=VMEM_LIMIT)


def _dot(a, b):
    return jnp.dot(a, b, preferred_element_type=F32)


def _dot_nt(a, b):
    return lax.dot_general(a, b, (((1,), (1,)), ((), ())), preferred_element_type=F32)


def _split_hi_lo(x):
    hi = x.astype(BF16)
    lo = (x - hi.astype(F32)).astype(BF16)
    return hi, lo


def _inproj_body(x_ref, g_ref, wa_ref, wd_ref, wz_ref, wab_ref, oa_ref, od_ref, oz_ref, oab_ref):
    x = x_ref[...]
    ms = jnp.mean(x * x, axis=-1, keepdims=True)
    h = x * lax.rsqrt(ms + EPS) * g_ref[...]
    hi, lo = _split_hi_lo(h)
    oa_ref[...] = _dot(hi, wa_ref[...]).astype(BF16)
    od_ref[...] = _dot(hi, wd_ref[...]).astype(BF16)
    oz_ref[...] = _dot(hi, wz_ref[...]).astype(BF16)
    nab = oab_ref.shape[-1]
    ab = _dot(hi, wab_ref[...])
    ab_lo = _dot(lo, wab_ref[:, :nab])
    oab_ref[...] = ab[:, :nab] + ab[:, nab:] + ab_lo


def _inproj(x2, ln1_g, w_in, tm=512):
    n, d = x2.shape
    a3 = 3 * ATTN_WIDTH
    d3 = 3 * DN_WIDTH
    wa = w_in[:, :a3].astype(BF16)
    wd = w_in[:, a3:a3 + d3].astype(BF16)
    wz = w_in[:, a3 + d3:a3 + d3 + DN_WIDTH].astype(BF16)
    wab_f = w_in[:, a3 + d3 + DN_WIDTH:]
    nab = wab_f.shape[1]
    wab_hi, wab_lo = _split_hi_lo(wab_f)
    wab = jnp.concatenate([wab_hi, wab_lo], axis=1)
    const = lambda i: (0, 0)
    row = lambda i: (i, 0)
    return pl.pallas_call(
        _inproj_body,
        grid=(n // tm,),
        in_specs=[
            pl.BlockSpec((tm, d), row),
            pl.BlockSpec((1, d), const),
            pl.BlockSpec((d, a3), const),
            pl.BlockSpec((d, d3), const),
            pl.BlockSpec((d, DN_WIDTH), const),
            pl.BlockSpec((d, 2 * nab), const),
        ],
        out_specs=[
            pl.BlockSpec((tm, a3), row),
            pl.BlockSpec((tm, d3), row),
            pl.BlockSpec((tm, DN_WIDTH), row),
            pl.BlockSpec((tm, nab), row),
        ],
        out_shape=[
            jax.ShapeDtypeStruct((n, a3), BF16),
            jax.ShapeDtypeStruct((n, d3), BF16),
            jax.ShapeDtypeStruct((n, DN_WIDTH), BF16),
            jax.ShapeDtypeStruct((n, nab), F32),
        ],
        compiler_params=_cparams(("parallel",)),
    )(x2, ln1_g.reshape(1, d), wa, wd, wz, wab)


def _attn_bias_table(rpb):
    c = jnp.arange(GRID_W)
    c0 = jnp.clip(c - WIN_W // 2, 0, GRID_W - WIN_W)
    kc = jnp.arange(GRID_W)
    inwin = (kc[None, :] >= c0[:, None]) & (kc[None, :] < c0[:, None] + WIN_W)
    h = rpb.shape[0]
    padw = GRID_W - WIN_W
    rp = jnp.pad(rpb.astype(F32), ((0, 0), (0, 0), (padw, padw)))
    by_col = jnp.stack([rp[:, :, GRID_W - 1 - cc:2 * GRID_W - 1 - cc] for cc in range(GRID_W)], axis=2)
    tab = jnp.stack([by_col[:, WIN_H - 1 - dd:2 * WIN_H - 1 - dd] for dd in range(WIN_H)], axis=1)
    tab = tab.transpose(0, 1, 3, 2, 4)
    tab = jnp.where(inwin[None, None, :, None, :], tab, NEG)
    tab = tab.reshape(h // 2, 2, WIN_H, GRID_W, WIN_H * GRID_W)
    return tab.transpose(0, 2, 1, 3, 4).reshape(h // 2, WIN_H, 2 * GRID_W, WIN_H * GRID_W)


def _attn_body(q_ref, k_ref, v_ref, qg_ref, kg_ref, bias_ref, o_ref, qlo_s, qhi_s, kn_s, *, rows):
    s_len = rows * GRID_W
    lane = lax.broadcasted_iota(I32, (1, LANES), 1)
    is_lo = lane < ATTN_HD
    blk = 256

    def head_norm(x, gain):
        x2 = x * x
        s_lo = jnp.sum(jnp.where(is_lo, x2, 0.0), axis=-1, keepdims=True)
        s_hi = jnp.sum(jnp.where(is_lo, 0.0, x2), axis=-1, keepdims=True)
        ms = jnp.where(is_lo, s_lo, s_hi) * (1.0 / ATTN_HD)
        return x * lax.rsqrt(ms + EPS) * gain

    def prep(i, c):
        sl = pl.ds(pl.multiple_of(i * blk, blk), blk)
        qn = head_norm(q_ref[0, sl, :].astype(F32), qg_ref[...]) * (ATTN_HD ** -0.5)
        qlo_s[sl, :] = jnp.where(is_lo, qn, 0.0).astype(BF16)
        qhi_s[sl, :] = jnp.where(is_lo, 0.0, qn).astype(BF16)
        kn_s[sl, :] = head_norm(k_ref[0, sl, :].astype(F32), kg_ref[...]).astype(BF16)
        return c

    lax.fori_loop(0, s_len // blk, prep, 0)

    band = WIN_H * GRID_W

    def row_step(i, c):
        ctx = []
        for u in range(ATTN_UNROLL):
            r = i * ATTN_UNROLL + u
            r0 = jnp.clip(r - WIN_H // 2, 0, rows - WIN_H)
            qsl = pl.ds(pl.multiple_of(r * GRID_W, GRID_W), GRID_W)
            ksl = pl.ds(pl.multiple_of(r0 * GRID_W, GRID_W), band)
            q2 = jnp.concatenate([qlo_s[qsl, :], qhi_s[qsl, :]], axis=0)
            ctx.append(dict(d=r - r0, qsl=qsl, ksl=ksl, s=_dot_nt(q2, kn_s[ksl, :])))
        for c_ in ctx:
            s = c_['s'] + bias_ref[0, c_['d']]
            m = jnp.max(s, axis=-1, keepdims=True)
            p = jnp.exp(s - m)
            c_['l'] = jnp.sum(p, axis=-1, keepdims=True)
            c_['p'] = p.astype(BF16)
        for c_ in ctx:
            c_['pv'] = _dot(c_['p'], v_ref[0, c_['ksl'], :])
        for c_ in ctx:
            pv = c_['pv'] / c_['l']
            o = jnp.where(is_lo, pv[:GRID_W], pv[GRID_W:])
            o_ref[0, c_['qsl'], :] = o.astype(o_ref.dtype)
        return c

    lax.fori_loop(0, rows // ATTN_UNROLL, row_step, 0)


def _attention(aqkv, q_g, k_g, rpb):
    b, s_len, _ = aqkv.shape
    rows = s_len // GRID_W
    assert min(WIN_H, rows) == WIN_H
    pairs = ATTN_HEADS // 2
    bias = _attn_bias_table(rpb)
    qg2 = jnp.tile(q_g, 2).reshape(1, LANES).astype(F32)
    kg2 = jnp.tile(k_g, 2).reshape(1, LANES).astype(F32)
    band = WIN_H * GRID_W
    return pl.pallas_call(
        functools.partial(_attn_body, rows=rows),
        grid=(pairs, b),
        in_specs=[
            pl.BlockSpec((1, s_len, LANES), lambda p, i: (i, 0, p)),
            pl.BlockSpec((1, s_len, LANES), lambda p, i: (i, 0, pairs + p)),
            pl.BlockSpec((1, s_len, LANES), lambda p, i: (i, 0, 2 * pairs + p)),
            pl.BlockSpec((1, LANES), lambda p, i: (0, 0)),
            pl.BlockSpec((1, LANES), lambda p, i: (0, 0)),
            pl.BlockSpec((1, WIN_H, 2 * GRID_W, band), lambda p, i: (p, 0, 0, 0)),
        ],
        out_specs=pl.BlockSpec((1, s_len, LANES), lambda p, i: (i, 0, p)),
        out_shape=jax.ShapeDtypeStruct((b, s_len, ATTN_WIDTH), BF16),
        scratch_shapes=[
            pltpu.VMEM((s_len, LANES), BF16),
            pltpu.VMEM((s_len, LANES), BF16),
            pltpu.VMEM((s_len, LANES), BF16),
        ],
        compiler_params=_cparams(("arbitrary", "arbitrary")),
    )(aqkv, aqkv, aqkv, qg2, kg2, bias)


SUPER = 2 * DN_CHUNK
HALO = 16
DN_UNROLL = 8


def _dn_body(q_ref, k_ref, v_ref, z_ref, ab_ref, cwq_ref, cwk_ref, cwv_ref, alog_ref, dtb_ref, og_ref,
             o_ref, pad_s, qs, ks, vs, of_s, ob_s, *, s_len):
    head = pl.program_id(1)
    n_super = s_len // SUPER

    zeros_halo = jnp.zeros((HALO, LANES), F32)
    for t, src in enumerate((q_ref, k_ref, v_ref)):
        pad_s[t, 0:HALO, :] = zeros_halo
        pad_s[t, HALO + s_len:2 * HALO + s_len, :] = zeros_halo

        def stage(i, c, t=t, src=src):
            off = pl.multiple_of(i * 512, 512)
            pad_s[t, pl.ds(HALO + off, 512), :] = src[0, pl.ds(off, 512), :].astype(F32)
            return c

        lax.fori_loop(0, s_len // 512, stage, 0)

    win = SUPER + HALO

    def conv_step(i, c):
        t0 = pl.multiple_of(i * SUPER, SUPER)
        outs = []
        for t, cw_ref in enumerate((cwq_ref, cwk_ref, cwv_ref)):
            xw = pad_s[t, pl.ds(t0 + HALO // 2, win), :]
            acc = jnp.zeros((SUPER, LANES), F32)
            for j in range(DN_CONV):
                sh = (DN_CONV // 2 - j) % win
                xr = xw if sh == 0 else pltpu.roll(xw, sh, axis=0)
                acc = acc + xr[HALO // 2:HALO // 2 + SUPER] * cw_ref[j:j + 1, :]
            outs.append(acc * jax.nn.sigmoid(acc))
        qc, kc, vc = outs
        sl = pl.ds(t0, SUPER)
        qs[sl, :] = qc * lax.rsqrt(jnp.sum(qc * qc, axis=-1, keepdims=True) + EPS) * (DN_HD ** -0.5)
        ks[sl, :] = kc * lax.rsqrt(jnp.sum(kc * kc, axis=-1, keepdims=True) + EPS)
        vs[sl, :] = vc
        return c

    lax.fori_loop(0, n_super, conv_step, 0)

    ri = lax.broadcasted_iota(I32, (SUPER, SUPER), 0)
    ci = lax.broadcasted_iota(I32, (SUPER, SUPER), 1)
    same = (ri // DN_CHUNK) == (ci // DN_CHUNK)
    eye = (ri == ci).astype(F32)
    pos = ri % DN_CHUNK

    def prep(sc, rev):
        t0 = pl.multiple_of(sc * SUPER, SUPER)
        sl = pl.ds(t0, SUPER)
        q = qs[sl, :]
        k = ks[sl, :]
        v = vs[sl, :]
        ab = ab_ref[0, 0, sl, :]
        dcol = 1 if rev else 0
        a_b = jnp.broadcast_to(ab[:, dcol:dcol + 1], (SUPER, LANES))
        b_b = jnp.broadcast_to(ab[:, 2 + dcol:3 + dcol], (SUPER, LANES))
        neg_rate = -jnp.exp(jnp.full((1, LANES), alog_ref[dcol, head], F32))
        g = neg_rate * jax.nn.softplus(a_b + dtb_ref[dcol, head])
        beta = jax.nn.sigmoid(b_b)
        gc = g
        for sft in (1, 2, 4, 8, 16, 32):
            if rev:
                gc = gc + jnp.where(pos < DN_CHUNK - sft, pltpu.roll(gc, SUPER - sft, axis=0), 0.0)
            else:
                gc = gc + jnp.where(pos >= sft, pltpu.roll(gc, sft, axis=0), 0.0)
        gct = gc.T
        diff = gc - gct
        if rev:
            incl = same & (ri <= ci)
            strict = same & (ri < ci)
        else:
            incl = same & (ri >= ci)
            strict = same & (ri > ci)
        decay = jnp.where(incl, jnp.exp(jnp.where(incl, diff, 0.0)), 0.0)
        kb = k * beta
        gram = _dot_nt(jnp.concatenate([q, kb], axis=0).astype(BF16), k.astype(BF16))
        qkd = (gram[:SUPER] * decay).astype(BF16)
        low = jnp.where(strict, gram[SUPER:] * decay, 0.0)
        egc = jnp.exp(gc)
        rhs = jnp.concatenate([v * beta, kb * egc], axis=1).astype(BF16)
        return dict(sl=sl, rev=rev, gc=gc, gct=gct, qkd=qkd, rhs=rhs, qe=q * egc, kt=k.T,
                    inv=eye - low, pw=low.astype(BF16))

    def phase1(chains):
        ctx = [prep(sc, rev) for sc, rev in chains]
        for _ in range(5):
            for c in ctx:
                c['pw'] = _dot(c['pw'], c['pw']).astype(BF16)
            for c in ctx:
                c['inv'] = c['inv'] + _dot(c['inv'].astype(BF16), c['pw'])
        for c in ctx:
            c['sol'] = _dot(c['inv'].astype(BF16), c['rhs']).astype(BF16)
        for c in ctx:
            qo = _dot(c['qkd'], c['sol'])
            c['o0'] = qo[:, :LANES]
            c['qt'] = c['qe'] - qo[:, LANES:]
        out = []
        for c in ctx:
            rev, gc, gct = c['rev'], c['gc'], c['gct']
            per_chunk = []
            for cidx in ((1, 0) if rev else (0, 1)):
                lo = cidx * DN_CHUNK
                last = lo if rev else lo + DN_CHUNK - 1
                g_last = gc[last:last + 1, :]
                in_c = (ci // DN_CHUNK) == cidx
                kst = c['kt'] * jnp.where(in_c, jnp.exp(jnp.where(in_c, g_last - gct, 0.0)), 0.0)
                mn = _dot(kst.astype(BF16), c['sol'])
                lhs = jnp.concatenate([-mn[:, LANES:], c['qt'][lo:lo + DN_CHUNK]], axis=0).astype(BF16)
                per_chunk.append((cidx, lhs, mn[:, :LANES], jnp.exp(g_last), c['o0'][lo:lo + DN_CHUNK]))
            out.append((c['sl'], per_chunk))
        return out

    def scan(state, p1, rev):
        sl, per_chunk = p1
        outs = [None, None]
        for cidx, lhs, ku, e_last, o0_c in per_chunk:
            r = _dot(lhs, state.astype(BF16))
            outs[cidx] = r[LANES:] + o0_c
            state = e_last * state + r[:LANES] + ku
        dst = ob_s if rev else of_s
        dst[sl, :] = jnp.concatenate(outs, axis=0)
        return state

    def scan_step(i, carry):
        s_f, s_b = carry
        chains = []
        for u in range(DN_UNROLL):
            chains += [(i * DN_UNROLL + u, False), (n_super - 1 - (i * DN_UNROLL + u), True)]
        p1 = phase1(chains)
        p_f, p_b = p1[0::2], p1[1::2]
        for u in range(DN_UNROLL):
            s_f = scan(s_f, p_f[u], False)
            s_b = scan(s_b, p_b[u], True)
        return s_f, s_b

    z0 = jnp.zeros((DN_HD, DN_HD), F32)
    lax.fori_loop(0, n_super // DN_UNROLL, scan_step, (z0, z0))

    def fin(i, c):
        sl = pl.ds(pl.multiple_of(i * 512, 512), 512)
        o = of_s[sl, :] + ob_s[sl, :]
        o = o * lax.rsqrt(jnp.mean(o * o, axis=-1, keepdims=True) + EPS) * og_ref[...]
        zf = z_ref[0, sl, :].astype(F32)
        o_ref[0, sl, :] = (o * (zf * jax.nn.sigmoid(zf))).astype(o_ref.dtype)
        return c

    lax.fori_loop(0, s_len // 512, fin, 0)


def _deltanet(dqkv, dz, dab, conv_w, a_log, dt_bias, out_g):
    b, s_len, _ = dqkv.shape
    h = DN_HEADS
    ab = dab.reshape(b, s_len, 2, 2, h).transpose(0, 4, 1, 2, 3).reshape(b, h, s_len, 4)
    cw = conv_w.astype(F32)
    tok = lambda off: (lambda i, j: (i, 0, off + j))
    smem = pl.BlockSpec(memory_space=pltpu.SMEM)
    return pl.pallas_call(
        functools.partial(_dn_body, s_len=s_len),
        grid=(b, h),
        in_specs=[
            pl.BlockSpec((1, s_len, LANES), tok(0)),
            pl.BlockSpec((1, s_len, LANES), tok(h)),
            pl.BlockSpec((1, s_len, LANES), tok(2 * h)),
            pl.BlockSpec((1, s_len, LANES), tok(0)),
            pl.BlockSpec((1, 1, s_len, 4), lambda i, j: (i, j, 0, 0)),
            pl.BlockSpec((DN_CONV, LANES), lambda i, j: (0, j)),
            pl.BlockSpec((DN_CONV, LANES), lambda i, j: (0, h + j)),
            pl.BlockSpec((DN_CONV, LANES), lambda i, j: (0, 2 * h + j)),
            smem,
            smem,
            pl.BlockSpec((1, LANES), lambda i, j: (0, 0)),
        ],
        out_specs=pl.BlockSpec((1, s_len, LANES), tok(0)),
        out_shape=jax.ShapeDtypeStruct((b, s_len, DN_WIDTH), BF16),
        scratch_shapes=[
            pltpu.VMEM((3, s_len + 2 * HALO, LANES), F32),
            pltpu.VMEM((s_len, LANES), F32),
            pltpu.VMEM((s_len, LANES), F32),
            pltpu.VMEM((s_len, LANES), F32),
            pltpu.VMEM((s_len, LANES), F32),
            pltpu.VMEM((s_len, LANES), F32),
        ],
        compiler_params=_cparams(("arbitrary", "arbitrary")),
    )(dqkv, dqkv, dqkv, dz, ab, cw, cw, cw, a_log.astype(F32), dt_bias.astype(F32),
      out_g.reshape(1, LANES).astype(F32))


ROUTER_ROWS = 80
TOK_SUB = 8


def _outproj_body(x_ref, at_ref, dn_ref, ag_ref, wo_ref, lg_ref, wrh_ref, wrl_ref, rb_ref, su_ref,
                  x1_ref, h2_ref, eid_ref, gate_ref, rank_ref, cnt_ref, base_s):
    step = pl.program_id(0)
    tm = x_ref.shape[0]

    @pl.when(step == 0)
    def _():
        base_s[...] = jnp.zeros_like(base_s)

    a = at_ref[...].astype(F32)
    an = a * lax.rsqrt(jnp.mean(a * a, axis=-1, keepdims=True) + EPS) * ag_ref[...]
    mix = jnp.concatenate([an.astype(BF16), dn_ref[...]], axis=1)
    x1 = x_ref[...] + _dot(mix, wo_ref[...])
    x1_ref[...] = x1
    h2 = x1 * lax.rsqrt(jnp.mean(x1 * x1, axis=-1, keepdims=True) + EPS) * lg_ref[...]
    for s in range(TOK_SUB):
        h2_ref[:, s, :] = h2[:, s * LANES:(s + 1) * LANES]

    hi, lo = _split_hi_lo(h2)
    wrh = wrh_ref[...]
    lt = _dot_nt(wrh, hi) + _dot_nt(wrh, lo) + _dot_nt(wrl_ref[...], hi) + rb_ref[...]

    sub8 = lax.broadcasted_iota(I32, (N_GROUPS, tm), 0)
    gl = lt[0:N_GROUPS]
    gmax = jnp.max(gl, axis=0, keepdims=True)
    gsel = jnp.min(jnp.where(gl == gmax, sub8, N_GROUPS), axis=0, keepdims=True)
    gp = 1.0 / jnp.sum(jnp.exp(gl - gmax), axis=0, keepdims=True)
    in_group = jnp.zeros((EPG, tm), F32)
    for g in range(N_GROUPS):
        in_group = in_group + jnp.where(gsel == g, lt[N_GROUPS + g * EPG:N_GROUPS + (g + 1) * EPG], 0.0)
    v1 = jnp.max(in_group, axis=0, keepdims=True)
    i1 = jnp.min(jnp.where(in_group == v1, sub8, EPG), axis=0, keepdims=True)
    rest = jnp.where(sub8 == i1, -jnp.inf, in_group)
    v2 = jnp.max(rest, axis=0, keepdims=True)
    i2 = jnp.min(jnp.where(rest == v2, sub8, EPG), axis=0, keepdims=True)
    e21 = jnp.exp(v2 - v1)
    den = 1.0 + e21
    eid1 = gsel * EPG + i1
    eid2 = gsel * EPG + i2
    eid_ref[0:1, :] = eid1
    eid_ref[1:2, :] = eid2
    gate_ref[0:1, :] = gp * (1.0 / den)
    gate_ref[1:2, :] = gp * (e21 / den)

    sub64 = lax.broadcasted_iota(I32, (N_EXPERTS, tm), 0)
    oh1 = sub64 == eid1
    oh2 = sub64 == eid2
    ohs = jnp.where(oh1 | oh2, 1.0, 0.0)
    before = base_s[...] + _dot(ohs.astype(BF16), su_ref[...])
    rank_ref[0:1, :] = jnp.sum(jnp.where(oh1, before, 0.0), axis=0, keepdims=True).astype(I32)
    rank_ref[1:2, :] = jnp.sum(jnp.where(oh2, before, 0.0), axis=0, keepdims=True).astype(I32)
    base_new = base_s[...] + jnp.sum(ohs, axis=1, keepdims=True)
    base_s[...] = base_new
    cnt_ref[...] = jnp.broadcast_to(base_new, cnt_ref.shape)


def _outproj_router(x2, attn_raw, dn, attn_g, w_out, ln2_g, wg_r, bg_r, we_r, be_r, tm=512):
    n, d = x2.shape
    wr = jnp.concatenate([wg_r, we_r], axis=1).T.astype(F32)
    wr = jnp.pad(wr, ((0, ROUTER_ROWS - wr.shape[0]), (0, 0)))
    wrh, wrl = _split_hi_lo(wr)
    rb = jnp.pad(jnp.concatenate([bg_r, be_r]).astype(F32), (0, ROUTER_ROWS - N_GROUPS - N_EXPERTS))
    rb = rb.reshape(ROUTER_ROWS, 1)
    t_i = jnp.arange(tm)
    su = (t_i[:, None] < t_i[None, :]).astype(BF16)
    const = lambda i: (0, 0)
    row = lambda i: (i, 0)
    col = lambda i: (0, i)
    return pl.pallas_call(
        _outproj_body,
        grid=(n // tm,),
        in_specs=[
            pl.BlockSpec((tm, d), row),
            pl.BlockSpec((tm, ATTN_WIDTH), row),
            pl.BlockSpec((tm, DN_WIDTH), row),
            pl.BlockSpec((1, ATTN_WIDTH), const),
            pl.BlockSpec((ATTN_WIDTH + DN_WIDTH, d), const),
            pl.BlockSpec((1, d), const),
            pl.BlockSpec((ROUTER_ROWS, d), const),
            pl.BlockSpec((ROUTER_ROWS, d), const),
            pl.BlockSpec((ROUTER_ROWS, 1), const),
            pl.BlockSpec((tm, tm), const),
        ],
        out_specs=[
            pl.BlockSpec((tm, d), row),
            pl.BlockSpec((tm, TOK_SUB, LANES), lambda i: (i, 0, 0)),
            pl.BlockSpec((2, tm), col),
            pl.BlockSpec((2, tm), col),
            pl.BlockSpec((2, tm), col),
            pl.BlockSpec((N_EXPERTS, LANES), const),
        ],
        out_shape=[
            jax.ShapeDtypeStruct((n, d), F32),
            jax.ShapeDtypeStruct((n, TOK_SUB, LANES), F32),
            jax.ShapeDtypeStruct((2, n), I32),
            jax.ShapeDtypeStruct((2, n), F32),
            jax.ShapeDtypeStruct((2, n), I32),
            jax.ShapeDtypeStruct((N_EXPERTS, LANES), F32),
        ],
        scratch_shapes=[pltpu.VMEM((N_EXPERTS, 1), F32)],
        compiler_params=_cparams(("arbitrary",)),
    )(x2, attn_raw, dn, attn_g.reshape(1, -1).astype(F32), w_out.astype(BF16), ln2_g.reshape(1, d).astype(F32),
      wrh, wrl, rb, su)


def _dispatch_body(dest_ref, zrow_ref, h_ref, xs_hbm, zbuf, zsem, sem, *, n):
    tm = h_ref.shape[0]
    base = pl.program_id(0) * tm

    @pl.when(pl.program_id(0) == 0)
    def _():
        zbuf[...] = jnp.zeros_like(zbuf)

        def zero_copy(e):
            return pltpu.make_async_copy(zbuf, xs_hbm.at[pl.ds(zrow_ref[e], MOE_BLOCK)], zsem)

        def start(e, c):
            @pl.when(zrow_ref[e] >= 0)
            def _():
                zero_copy(e).start()
            return c

        def finish(e, c):
            @pl.when(zrow_ref[e] >= 0)
            def _():
                zero_copy(e).wait()
            return c

        lax.fori_loop(0, N_EXPERTS, start, 0)
        lax.fori_loop(0, N_EXPERTS, finish, 0)

        def tail_copy(b):
            return pltpu.make_async_copy(zbuf, xs_hbm.at[pl.ds(b * MOE_BLOCK, MOE_BLOCK)], zsem)

        def tail_start(b, c):
            tail_copy(b).start()
            return c

        def tail_finish(b, c):
            tail_copy(b).wait()
            return c

        n_blocks = xs_hbm.shape[0] // MOE_BLOCK
        lax.fori_loop(zrow_ref[N_EXPERTS], n_blocks, tail_start, 0)
        lax.fori_loop(zrow_ref[N_EXPERTS], n_blocks, tail_finish, 0)

    def issue(j, c):
        src = h_ref.at[pl.ds(j, 1)]
        pltpu.make_async_copy(src, xs_hbm.at[pl.ds(dest_ref[base + j], 1)], sem).start()
        pltpu.make_async_copy(src, xs_hbm.at[pl.ds(dest_ref[n + base + j], 1)], sem).start()
        return c

    lax.fori_loop(0, tm, issue, 0, unroll=8)
    for _ in range(2):
        pltpu.make_async_copy(h_ref, xs_hbm.at[pl.ds(0, tm)], sem).wait()


def _dispatch(h2t, dest, zrow, p_rows, tm=512):
    n = h2t.shape[0]
    grid_spec = pltpu.PrefetchScalarGridSpec(
        num_scalar_prefetch=2,
        grid=(n // tm,),
        in_specs=[pl.BlockSpec((tm, TOK_SUB, LANES), lambda i, dst, zr: (i, 0, 0))],
        out_specs=pl.BlockSpec(memory_space=pl.ANY),
        scratch_shapes=[pltpu.VMEM((MOE_BLOCK, TOK_SUB, LANES), F32), pltpu.SemaphoreType.DMA(()),
                        pltpu.SemaphoreType.DMA(())],
    )
    return pl.pallas_call(
        functools.partial(_dispatch_body, n=n),
        grid_spec=grid_spec,
        out_shape=jax.ShapeDtypeStruct((p_rows, TOK_SUB, LANES), F32),
        compiler_params=_cparams(("arbitrary",)),
    )(dest.reshape(-1), zrow, h2t)


def _expert_body(be_ref, nu_ref, xs_ref, wg_ref, wu_ref, wd_ref, o_ref):
    used = pl.program_id(0) < nu_ref[0]

    @pl.when(used)
    def _():
        x = jnp.concatenate([xs_ref[:, s, :] for s in range(TOK_SUB)], axis=1).astype(BF16)
        gt = _dot(x, wg_ref[0])
        up = _dot(x, wu_ref[0])
        hid = gt * jax.nn.sigmoid(gt) * up
        out = _dot(hid.astype(BF16), wd_ref[0])
        for s in range(TOK_SUB):
            o_ref[:, s, :] = out[:, s * LANES:(s + 1) * LANES]

    @pl.when(jnp.logical_not(used))
    def _():
        o_ref[...] = jnp.zeros_like(o_ref)


def _experts(xs, block_expert, n_used, w_gate, w_up, w_down):
    p_rows = xs.shape[0]
    nb = p_rows // MOE_BLOCK
    d, de = w_gate.shape[-2:]
    wmap = lambda i, be, nu: (be[i], 0, 0)
    xmap = lambda i, be, nu: (jnp.minimum(i, nu[0] - 1), 0, 0)
    grid_spec = pltpu.PrefetchScalarGridSpec(
        num_scalar_prefetch=2,
        grid=(nb,),
        in_specs=[
            pl.BlockSpec((MOE_BLOCK, TOK_SUB, LANES), xmap),
            pl.BlockSpec((1, d, de), wmap),
            pl.BlockSpec((1, d, de), wmap),
            pl.BlockSpec((1, de, d), wmap),
        ],
        out_specs=pl.BlockSpec((MOE_BLOCK, TOK_SUB, LANES), lambda i, be, nu: (i, 0, 0)),
    )
    return pl.pallas_call(
        _expert_body,
        grid_spec=grid_spec,
        out_shape=jax.ShapeDtypeStruct((p_rows, TOK_SUB, LANES), F32),
        compiler_params=_cparams(("arbitrary",)),
    )(block_expert, n_used, xs, w_gate.astype(BF16), w_up.astype(BF16), w_down.astype(BF16))


def _combine_body(dest_ref, x1_ref, gt_ref, eo_hbm, out_ref, rbuf, sem, *, n):
    tm = x1_ref.shape[0]
    step = pl.program_id(0)
    slot = step % 2

    def gather(tile, slot_):
        base = tile * tm

        def issue(j, c):
            d0 = dest_ref[base + j]
            d1 = dest_ref[n + base + j]
            pltpu.make_async_copy(eo_hbm.at[pl.ds(d0, 1)], rbuf.at[slot_, 0, pl.ds(j, 1)], sem.at[slot_]).start()
            pltpu.make_async_copy(eo_hbm.at[pl.ds(d1, 1)], rbuf.at[slot_, 1, pl.ds(j, 1)], sem.at[slot_]).start()
            return c

        lax.fori_loop(0, tm, issue, 0, unroll=8)

    @pl.when(step == 0)
    def _():
        gather(0, 0)

    @pl.when(step + 1 < pl.num_programs(0))
    def _():
        gather(step + 1, 1 - slot)

    for k in range(2):
        pltpu.make_async_copy(eo_hbm.at[pl.ds(0, tm)], rbuf.at[slot, k], sem.at[slot]).wait()
    g0 = jnp.broadcast_to(gt_ref[:, 0:1], (tm, LANES))
    g1 = jnp.broadcast_to(gt_ref[:, 1:2], (tm, LANES))
    for s in range(TOK_SUB):
        cols = slice(s * LANES, (s + 1) * LANES)
        out_ref[:, cols] = x1_ref[:, cols] + (g0 * rbuf[slot, 0, :, s, :] + g1 * rbuf[slot, 1, :, s, :])


def _combine(x1, gates_t, dest, eo, tm=256):
    n, d = x1.shape
    grid_spec = pltpu.PrefetchScalarGridSpec(
        num_scalar_prefetch=1,
        grid=(n // tm,),
        in_specs=[
            pl.BlockSpec((tm, d), lambda i, dst: (i, 0)),
            pl.BlockSpec((tm, 2), lambda i, dst: (i, 0)),
            pl.BlockSpec(memory_space=pl.ANY),
        ],
        out_specs=pl.BlockSpec((tm, d), lambda i, dst: (i, 0)),
        scratch_shapes=[pltpu.VMEM((2, 2, tm, TOK_SUB, LANES), F32), pltpu.SemaphoreType.DMA((2,))],
    )
    return pl.pallas_call(
        functools.partial(_combine_body, n=n),
        grid_spec=grid_spec,
        out_shape=jax.ShapeDtypeStruct((n, d), F32),
        compiler_params=_cparams(("arbitrary",)),
    )(dest.reshape(-1), x1, gates_t, eo)


def _dispatch_plan(eid, rank, counts, n):
    nk = 2 * n
    p_rows = -(-nk // MOE_BLOCK) * MOE_BLOCK + N_EXPERTS * MOE_BLOCK
    nb = p_rows // MOE_BLOCK
    padded = -(-counts // MOE_BLOCK) * MOE_BLOCK
    pends = jnp.cumsum(padded)
    pstarts = pends - padded
    experts = jnp.arange(N_EXPERTS, dtype=I32)
    dest = jnp.sum(jnp.where(eid[..., None] == experts, pstarts.astype(I32), 0), axis=-1) + rank
    block_start = jnp.arange(nb, dtype=I32) * MOE_BLOCK
    block_expert = jnp.clip(jnp.sum((pends[None, :] <= block_start[:, None]).astype(I32), axis=-1),
                            0, N_EXPERTS - 1).astype(I32)
    n_used = (pends[-1] // MOE_BLOCK).astype(I32).reshape(1)
    zrow = jnp.concatenate([jnp.where(padded > 0, pends - MOE_BLOCK, -1).astype(I32), n_used])
    return dest.astype(I32), zrow, block_expert, n_used, p_rows


def _layer(x, ln1_g, w_in, q_g, k_g, rpb, attn_g, conv_w, a_log, dt_bias, dn_g, w_out, ln2_g,
           wg_r, bg_r, we_r, be_r, w_gate, w_up, w_down):
    b, s_len, d = x.shape
    n = b * s_len
    x2 = x.reshape(n, d)
    aqkv, dqkv, dz, dab = _inproj(x2, ln1_g, w_in)
    attn_raw = _attention(aqkv.reshape(b, s_len, -1), q_g, k_g, rpb)
    dn = _deltanet(dqkv.reshape(b, s_len, -1), dz.reshape(b, s_len, -1), dab.reshape(b, s_len, -1),
                   conv_w, a_log, dt_bias, dn_g)
    x1, h2t, eid, gates, rank, cnt = _outproj_router(
        x2, attn_raw.reshape(n, -1), dn.reshape(n, -1), attn_g, w_out, ln2_g, wg_r, bg_r, we_r, be_r)
    counts = cnt[:, 0].astype(I32)
    dest, zrow, block_expert, n_used, p_rows = _dispatch_plan(eid, rank, counts, n)
    xs = _dispatch(h2t, dest, zrow, p_rows)
    eo = _experts(xs, block_expert, n_used, w_gate, w_up, w_down)
    out = _combine(x1, gates.T, dest, eo)
    return out.reshape(b, s_len, d)


def kernel(x, ln1_g, w_in, attn_q_norm_g, attn_k_norm_g, attn_rpb, attn_out_norm_g, dn_conv_w, dn_a_log,
           dn_dt_bias, dn_out_norm_g, w_out, ln2_g, router_group_w, router_group_b, router_expert_w,
           router_expert_b, expert_w_gate, expert_w_up, expert_w_down):
    for l in range(ln1_g.shape[0]):
        x = _layer(x, ln1_g[l], w_in[l], attn_q_norm_g[l], attn_k_norm_g[l], attn_rpb[l],
                   attn_out_norm_g[l], dn_conv_w[l], dn_a_log[l], dn_dt_bias[l], dn_out_norm_g[l],
                   w_out[l], ln2_g[l], router_group_w[l], router_group_b[l], router_expert_w[l],
                   router_expert_b[l], expert_w_gate[l], expert_w_up[l], expert_w_down[l])
    return x
```

```python
import functools

import jax
import jax.numpy as jnp
from jax import lax
from jax.experimental import pallas as pl
from jax.experimental.pallas import tpu as pltpu

F32 = jnp.float32
BF16 = jnp.bfloat16
I32 = jnp.int32

EPS = 1e-6
GRID_W = 64
WIN_H = 8
WIN_W = 16
ATTN_HEADS = 8
ATTN_HD = 64
ATTN_WIDTH = ATTN_HEADS * ATTN_HD
DN_HEADS = 4
DN_HD = 128
DN_WIDTH = DN_HEADS * DN_HD
DN_CONV = 5
DN_CHUNK = 64
N_GROUPS = 8
EPG = 8
N_EXPERTS = N_GROUPS * EPG
MOE_BLOCK = 256
NEG = -1e30

ATTN_UNROLL = 8
LANES = 128
VMEM_LIMIT = 56 * 1024 * 1024


def _cparams(sem):
    return pltpu.CompilerParams(dimension_semantics=sem, vmem_limit_bytes=VMEM_LIMIT)


def _dot(a, b):
    return jnp.dot(a, b, preferred_element_type=F32)


def _dot_nt(a, b):
    return lax.dot_general(a, b, (((1,), (1,)), ((), ())), preferred_element_type=F32)


def _split_hi_lo(x):
    hi = x.astype(BF16)
    lo = (x - hi.astype(F32)).astype(BF16)
    return hi, lo


def _inproj_body(x_ref, g_ref, wa_ref, wd_ref, wz_ref, wab_ref, oa_ref, od_ref, oz_ref, oab_ref):
    x = x_ref[...]
    ms = jnp.mean(x * x, axis=-1, keepdims=True)
    h = x * lax.rsqrt(ms + EPS) * g_ref[...]
    hi, lo = _split_hi_lo(h)
    oa_ref[...] = _dot(hi, wa_ref[...]).astype(BF16)
    od_ref[...] = _dot(hi, wd_ref[...]).astype(BF16)
    oz_ref[...] = _dot(hi, wz_ref[...]).astype(BF16)
    nab = oab_ref.shape[-1]
    ab = _dot(hi, wab_ref[...])
    ab_lo = _dot(lo, wab_ref[:, :nab])
    oab_ref[...] = ab[:, :nab] + ab[:, nab:] + ab_lo


def _inproj(x2, ln1_g, w_in, tm=512):
    n, d = x2.shape
    a3 = 3 * ATTN_WIDTH
    d3 = 3 * DN_WIDTH
    wa = w_in[:, :a3].astype(BF16)
    wd = w_in[:, a3:a3 + d3].astype(BF16)
    wz = w_in[:, a3 + d3:a3 + d3 + DN_WIDTH].astype(BF16)
    wab_f = w_in[:, a3 + d3 + DN_WIDTH:]
    nab = wab_f.shape[1]
    wab_hi, wab_lo = _split_hi_lo(wab_f)
    wab = jnp.concatenate([wab_hi, wab_lo], axis=1)
    const = lambda i: (0, 0)
    row = lambda i: (i, 0)
    return pl.pallas_call(
        _inproj_body,
        grid=(n // tm,),
        in_specs=[
            pl.BlockSpec((tm, d), row),
            pl.BlockSpec((1, d), const),
            pl.BlockSpec((d, a3), const),
            pl.BlockSpec((d, d3), const),
            pl.BlockSpec((d, DN_WIDTH), const),
            pl.BlockSpec((d, 2 * nab), const),
        ],
        out_specs=[
            pl.BlockSpec((tm, a3), row),
            pl.BlockSpec((tm, d3), row),
            pl.BlockSpec((tm, DN_WIDTH), row),
            pl.BlockSpec((tm, nab), row),
        ],
        out_shape=[
            jax.ShapeDtypeStruct((n, a3), BF16),
            jax.ShapeDtypeStruct((n, d3), BF16),
            jax.ShapeDtypeStruct((n, DN_WIDTH), BF16),
            jax.ShapeDtypeStruct((n, nab), F32),
        ],
        compiler_params=_cparams(("parallel",)),
    )(x2, ln1_g.reshape(1, d), wa, wd, wz, wab)


def _attn_bias_table(rpb):
    c = jnp.arange(GRID_W)
    c0 = jnp.clip(c - WIN_W // 2, 0, GRID_W - WIN_W)
    kc = jnp.arange(GRID_W)
    inwin = (kc[None, :] >= c0[:, None]) & (kc[None, :] < c0[:, None] + WIN_W)
    h = rpb.shape[0]
    padw = GRID_W - WIN_W
    rp = jnp.pad(rpb.astype(F32), ((0, 0), (0, 0), (padw, padw)))
    by_col = jnp.stack([rp[:, :, GRID_W - 1 - cc:2 * GRID_W - 1 - cc] for cc in range(GRID_W)], axis=2)
    tab = jnp.stack([by_col[:, WIN_H - 1 - dd:2 * WIN_H - 1 - dd] for dd in range(WIN_H)], axis=1)
    tab = tab.transpose(0, 1, 3, 2, 4)
    tab = jnp.where(inwin[None, None, :, None, :], tab, NEG)
    tab = tab.reshape(h // 2, 2, WIN_H, GRID_W, WIN_H * GRID_W)
    return tab.transpose(0, 2, 1, 3, 4).reshape(h // 2, WIN_H, 2 * GRID_W, WIN_H * GRID_W)


def _attn_body(q_ref, k_ref, v_ref, qg_ref, kg_ref, bias_ref, o_ref, qlo_s, qhi_s, kn_s, *, rows):
    s_len = rows * GRID_W
    lane = lax.broadcasted_iota(I32, (1, LANES), 1)
    is_lo = lane < ATTN_HD
    blk = 256

    def head_norm(x, gain):
        x2 = x * x
        s_lo = jnp.sum(jnp.where(is_lo, x2, 0.0), axis=-1, keepdims=True)
        s_hi = jnp.sum(jnp.where(is_lo, 0.0, x2), axis=-1, keepdims=True)
        ms = jnp.where(is_lo, s_lo, s_hi) * (1.0 / ATTN_HD)
        return x * lax.rsqrt(ms + EPS) * gain

    def prep(i, c):
        sl = pl.ds(pl.multiple_of(i * blk, blk), blk)
        qn = head_norm(q_ref[0, sl, :].astype(F32), qg_ref[...]) * (ATTN_HD ** -0.5)
        qlo_s[sl, :] = jnp.where(is_lo, qn, 0.0).astype(BF16)
        qhi_s[sl, :] = jnp.where(is_lo, 0.0, qn).astype(BF16)
        kn_s[sl, :] = head_norm(k_ref[0, sl, :].astype(F32), kg_ref[...]).astype(BF16)
        return c

    lax.fori_loop(0, s_len // blk, prep, 0)

    band = WIN_H * GRID_W

    def row_step(i, c):
        ctx = []
        for u in range(ATTN_UNROLL):
            r = i * ATTN_UNROLL + u
            r0 = jnp.clip(r - WIN_H // 2, 0, rows - WIN_H)
            qsl = pl.ds(pl.multiple_of(r * GRID_W, GRID_W), GRID_W)
            ksl = pl.ds(pl.multiple_of(r0 * GRID_W, GRID_W), band)
            q2 = jnp.concatenate([qlo_s[qsl, :], qhi_s[qsl, :]], axis=0)
            ctx.append(dict(d=r - r0, qsl=qsl, ksl=ksl, s=_dot_nt(q2, kn_s[ksl, :])))
        for c_ in ctx:
            s = c_['s'] + bias_ref[0, c_['d']]
            m = jnp.max(s, axis=-1, keepdims=True)
            p = jnp.exp(s - m)
            c_['l'] = jnp.sum(p, axis=-1, keepdims=True)
            c_['p'] = p.astype(BF16)
        for c_ in ctx:
            c_['pv'] = _dot(c_['p'], v_ref[0, c_['ksl'], :])
        for c_ in ctx:
            pv = c_['pv'] / c_['l']
            o = jnp.where(is_lo, pv[:GRID_W], pv[GRID_W:])
            o_ref[0, c_['qsl'], :] = o.astype(o_ref.dtype)
        return c

    lax.fori_loop(0, rows // ATTN_UNROLL, row_step, 0)


def _attention(aqkv, q_g, k_g, rpb):
    b, s_len, _ = aqkv.shape
    rows = s_len // GRID_W
    assert min(WIN_H, rows) == WIN_H
    pairs = ATTN_HEADS // 2
    bias = _attn_bias_table(rpb)
    qg2 = jnp.tile(q_g, 2).reshape(1, LANES).astype(F32)
    kg2 = jnp.tile(k_g, 2).reshape(1, LANES).astype(F32)
    band = WIN_H * GRID_W
    return pl.pallas_call(
        functools.partial(_attn_body, rows=rows),
        grid=(pairs, b),
        in_specs=[
            pl.BlockSpec((1, s_len, LANES), lambda p, i: (i, 0, p)),
            pl.BlockSpec((1, s_len, LANES), lambda p, i: (i, 0, pairs + p)),
            pl.BlockSpec((1, s_len, LANES), lambda p, i: (i, 0, 2 * pairs + p)),
            pl.BlockSpec((1, LANES), lambda p, i: (0, 0)),
            pl.BlockSpec((1, LANES), lambda p, i: (0, 0)),
            pl.BlockSpec((1, WIN_H, 2 * GRID_W, band), lambda p, i: (p, 0, 0, 0)),
        ],
        out_specs=pl.BlockSpec((1, s_len, LANES), lambda p, i: (i, 0, p)),
        out_shape=jax.ShapeDtypeStruct((b, s_len, ATTN_WIDTH), BF16),
        scratch_shapes=[
            pltpu.VMEM((s_len, LANES), BF16),
            pltpu.VMEM((s_len, LANES), BF16),
            pltpu.VMEM((s_len, LANES), BF16),
        ],
        compiler_params=_cparams(("arbitrary", "arbitrary")),
    )(aqkv, aqkv, aqkv, qg2, kg2, bias)


SUPER = 2 * DN_CHUNK
HALO = 16
DN_UNROLL = 8


def _dn_body(q_ref, k_ref, v_ref, z_ref, ab_ref, cwq_ref, cwk_ref, cwv_ref, alog_ref, dtb_ref, og_ref,
             o_ref, pad_s, qs, ks, vs, of_s, ob_s, *, s_len):
    head = pl.program_id(1)
    n_super = s_len // SUPER

    zeros_halo = jnp.zeros((HALO, LANES), F32)
    for t, src in enumerate((q_ref, k_ref, v_ref)):
        pad_s[t, 0:HALO, :] = zeros_halo
        pad_s[t, HALO + s_len:2 * HALO + s_len, :] = zeros_halo

        def stage(i, c, t=t, src=src):
            off = pl.multiple_of(i * 512, 512)
            pad_s[t, pl.ds(HALO + off, 512), :] = src[0, pl.ds(off, 512), :].astype(F32)
            return c

        lax.fori_loop(0, s_len // 512, stage, 0)

    win = SUPER + HALO

    def conv_step(i, c):
        t0 = pl.multiple_of(i * SUPER, SUPER)
        outs = []
        for t, cw_ref in enumerate((cwq_ref, cwk_ref, cwv_ref)):
            xw = pad_s[t, pl.ds(t0 + HALO // 2, win), :]
            acc = jnp.zeros((SUPER, LANES), F32)
            for j in range(DN_CONV):
                sh = (DN_CONV // 2 - j) % win
                xr = xw if sh == 0 else pltpu.roll(xw, sh, axis=0)
                acc = acc + xr[HALO // 2:HALO // 2 + SUPER] * cw_ref[j:j + 1, :]
            outs.append(acc * jax.nn.sigmoid(acc))
        qc, kc, vc = outs
        sl = pl.ds(t0, SUPER)
        qs[sl, :] = qc * lax.rsqrt(jnp.sum(qc * qc, axis=-1, keepdims=True) + EPS) * (DN_HD ** -0.5)
        ks[sl, :] = kc * lax.rsqrt(jnp.sum(kc * kc, axis=-1, keepdims=True) + EPS)
        vs[sl, :] = vc
        return c

    lax.fori_loop(0, n_super, conv_step, 0)

    ri = lax.broadcasted_iota(I32, (SUPER, SUPER), 0)
    ci = lax.broadcasted_iota(I32, (SUPER, SUPER), 1)
    same = (ri // DN_CHUNK) == (ci // DN_CHUNK)
    eye = (ri == ci).astype(F32)
    pos = ri % DN_CHUNK

    def prep(sc, rev):
        t0 = pl.multiple_of(sc * SUPER, SUPER)
        sl = pl.ds(t0, SUPER)
        q = qs[sl, :]
        k = ks[sl, :]
        v = vs[sl, :]
        ab = ab_ref[0, 0, sl, :]
        dcol = 1 if rev else 0
        a_b = jnp.broadcast_to(ab[:, dcol:dcol + 1], (SUPER, LANES))
        b_b = jnp.broadcast_to(ab[:, 2 + dcol:3 + dcol], (SUPER, LANES))
        neg_rate = -jnp.exp(jnp.full((1, LANES), alog_ref[dcol, head], F32))
        g = neg_rate * jax.nn.softplus(a_b + dtb_ref[dcol, head])
        beta = jax.nn.sigmoid(b_b)
        gc = g
        for sft in (1, 2, 4, 8, 16, 32):
            if rev:
                gc = gc + jnp.where(pos < DN_CHUNK - sft, pltpu.roll(gc, SUPER - sft, axis=0), 0.0)
            else:
                gc = gc + jnp.where(pos >= sft, pltpu.roll(gc, sft, axis=0), 0.0)
        gct = gc.T
        diff = gc - gct
        if rev:
            incl = same & (ri <= ci)
            strict = same & (ri < ci)
        else:
            incl = same & (ri >= ci)
            strict = same & (ri > ci)
        decay = jnp.where(incl, jnp.exp(jnp.where(incl, diff, 0.0)), 0.0)
        kb = k * beta
        gram = _dot_nt(jnp.concatenate([q, kb], axis=0).astype(BF16), k.astype(BF16))
        qkd = (gram[:SUPER] * decay).astype(BF16)
        low = jnp.where(strict, gram[SUPER:] * decay, 0.0)
        egc = jnp.exp(gc)
        rhs = jnp.concatenate([v * beta, kb * egc], axis=1).astype(BF16)
        return dict(sl=sl, rev=rev, gc=gc, gct=gct, qkd=qkd, rhs=rhs, qe=q * egc, kt=k.T,
                    inv=eye - low, pw=low.astype(BF16))

    def phase1(chains):
        ctx = [prep(sc, rev) for sc, rev in chains]
        for _ in range(5):
            for c in ctx:
                c['pw'] = _dot(c['pw'], c['pw']).astype(BF16)
            for c in ctx:
                c['inv'] = c['inv'] + _dot(c['inv'].astype(BF16), c['pw'])
        for c in ctx:
            c['sol'] = _dot(c['inv'].astype(BF16), c['rhs']).astype(BF16)
        for c in ctx:
            qo = _dot(c['qkd'], c['sol'])
            c['o0'] = qo[:, :LANES]
            c['qt'] = c['qe'] - qo[:, LANES:]
        out = []
        for c in ctx:
            rev, gc, gct = c['rev'], c['gc'], c['gct']
            per_chunk = []
            for cidx in ((1, 0) if rev else (0, 1)):
                lo = cidx * DN_CHUNK
                last = lo if rev else lo + DN_CHUNK - 1
                g_last = gc[last:last + 1, :]
                in_c = (ci // DN_CHUNK) == cidx
                kst = c['kt'] * jnp.where(in_c, jnp.exp(jnp.where(in_c, g_last - gct, 0.0)), 0.0)
                mn = _dot(kst.astype(BF16), c['sol'])
                lhs = jnp.concatenate([-mn[:, LANES:], c['qt'][lo:lo + DN_CHUNK]], axis=0).astype(BF16)
                per_chunk.append((cidx, lhs, mn[:, :LANES], jnp.exp(g_last), c['o0'][lo:lo + DN_CHUNK]))
            out.append((c['sl'], per_chunk))
        return out

    def scan(state, p1, rev):
        sl, per_chunk = p1
        outs = [None, None]
        for cidx, lhs, ku, e_last, o0_c in per_chunk:
            r = _dot(lhs, state.astype(BF16))
            outs[cidx] = r[LANES:] + o0_c
            state = e_last * state + r[:LANES] + ku
        dst = ob_s if rev else of_s
        dst[sl, :] = jnp.concatenate(outs, axis=0)
        return state

    def scan_step(i, carry):
        s_f, s_b = carry
        chains = []
        for u in range(DN_UNROLL):
            chains += [(i * DN_UNROLL + u, False), (n_super - 1 - (i * DN_UNROLL + u), True)]
        p1 = phase1(chains)
        p_f, p_b = p1[0::2], p1[1::2]
        for u in range(DN_UNROLL):
            s_f = scan(s_f, p_f[u], False)
            s_b = scan(s_b, p_b[u], True)
        return s_f, s_b

    z0 = jnp.zeros((DN_HD, DN_HD), F32)
    lax.fori_loop(0, n_super // DN_UNROLL, scan_step, (z0, z0))

    def fin(i, c):
        sl = pl.ds(pl.multiple_of(i * 512, 512), 512)
        o = of_s[sl, :] + ob_s[sl, :]
        o = o * lax.rsqrt(jnp.mean(o * o, axis=-1, keepdims=True) + EPS) * og_ref[...]
        zf = z_ref[0, sl, :].astype(F32)
        o_ref[0, sl, :] = (o * (zf * jax.nn.sigmoid(zf))).astype(o_ref.dtype)
        return c

    lax.fori_loop(0, s_len // 512, fin, 0)


def _deltanet(dqkv, dz, dab, conv_w, a_log, dt_bias, out_g):
    b, s_len, _ = dqkv.shape
    h = DN_HEADS
    ab = dab.reshape(b, s_len, 2, 2, h).transpose(0, 4, 1, 2, 3).reshape(b, h, s_len, 4)
    cw = conv_w.astype(F32)
    tok = lambda off: (lambda i, j: (i, 0, off + j))
    smem = pl.BlockSpec(memory_space=pltpu.SMEM)
    return pl.pallas_call(
        functools.partial(_dn_body, s_len=s_len),
        grid=(b, h),
        in_specs=[
            pl.BlockSpec((1, s_len, LANES), tok(0)),
            pl.BlockSpec((1, s_len, LANES), tok(h)),
            pl.BlockSpec((1, s_len, LANES), tok(2 * h)),
            pl.BlockSpec((1, s_len, LANES), tok(0)),
            pl.BlockSpec((1, 1, s_len, 4), lambda i, j: (i, j, 0, 0)),
            pl.BlockSpec((DN_CONV, LANES), lambda i, j: (0, j)),
            pl.BlockSpec((DN_CONV, LANES), lambda i, j: (0, h + j)),
            pl.BlockSpec((DN_CONV, LANES), lambda i, j: (0, 2 * h + j)),
            smem,
            smem,
            pl.BlockSpec((1, LANES), lambda i, j: (0, 0)),
        ],
        out_specs=pl.BlockSpec((1, s_len, LANES), tok(0)),
        out_shape=jax.ShapeDtypeStruct((b, s_len, DN_WIDTH), BF16),
        scratch_shapes=[
            pltpu.VMEM((3, s_len + 2 * HALO, LANES), F32),
            pltpu.VMEM((s_len, LANES), F32),
            pltpu.VMEM((s_len, LANES), F32),
            pltpu.VMEM((s_len, LANES), F32),
            pltpu.VMEM((s_len, LANES), F32),
            pltpu.VMEM((s_len, LANES), F32),
        ],
        compiler_params=_cparams(("arbitrary", "arbitrary")),
    )(dqkv, dqkv, dqkv, dz, ab, cw, cw, cw, a_log.astype(F32), dt_bias.astype(F32),
      out_g.reshape(1, LANES).astype(F32))


ROUTER_ROWS = 80
TOK_SUB = 8


def _to_token_tiles(x):
    blocks = jnp.stack([x[:, s * LANES:(s + 1) * LANES] for s in range(TOK_SUB)], axis=0)
    return pltpu.einshape("spl->psl", blocks)


def _from_token_tiles(t):
    blocks = pltpu.einshape("psl->spl", t)
    return jnp.concatenate([blocks[s] for s in range(TOK_SUB)], axis=1)


def _outproj_body(x_ref, at_ref, dn_ref, ag_ref, wo_ref, lg_ref, wrh_ref, wrl_ref, rb_ref, su_ref,
                  x1_ref, h2_ref, eid_ref, gate_ref, rank_ref, cnt_ref, base_s):
    step = pl.program_id(0)
    tm = x_ref.shape[0]

    @pl.when(step == 0)
    def _():
        base_s[...] = jnp.zeros_like(base_s)

    a = at_ref[...].astype(F32)
    an = a * lax.rsqrt(jnp.mean(a * a, axis=-1, keepdims=True) + EPS) * ag_ref[...]
    mix = jnp.concatenate([an.astype(BF16), dn_ref[...]], axis=1)
    x1 = x_ref[...] + _dot(mix, wo_ref[...])
    x1_ref[...] = x1
    h2 = x1 * lax.rsqrt(jnp.mean(x1 * x1, axis=-1, keepdims=True) + EPS) * lg_ref[...]
    h2_ref[...] = _to_token_tiles(h2)

    hi, lo = _split_hi_lo(h2)
    wrh = wrh_ref[...]
    lt = _dot_nt(wrh, hi) + _dot_nt(wrh, lo) + _dot_nt(wrl_ref[...], hi) + rb_ref[...]

    sub8 = lax.broadcasted_iota(I32, (N_GROUPS, tm), 0)
    gl = lt[0:N_GROUPS]
    gmax = jnp.max(gl, axis=0, keepdims=True)
    gsel = jnp.min(jnp.where(gl == gmax, sub8, N_GROUPS), axis=0, keepdims=True)
    gp = 1.0 / jnp.sum(jnp.exp(gl - gmax), axis=0, keepdims=True)
    in_group = jnp.zeros((EPG, tm), F32)
    for g in range(N_GROUPS):
        in_group = in_group + jnp.where(gsel == g, lt[N_GROUPS + g * EPG:N_GROUPS + (g + 1) * EPG], 0.0)
    v1 = jnp.max(in_group, axis=0, keepdims=True)
    i1 = jnp.min(jnp.where(in_group == v1, sub8, EPG), axis=0, keepdims=True)
    rest = jnp.where(sub8 == i1, -jnp.inf, in_group)
    v2 = jnp.max(rest, axis=0, keepdims=True)
    i2 = jnp.min(jnp.where(rest == v2, sub8, EPG), axis=0, keepdims=True)
    e21 = jnp.exp(v2 - v1)
    den = 1.0 + e21
    eid1 = gsel * EPG + i1
    eid2 = gsel * EPG + i2
    eid_ref[0:1, :] = eid1
    eid_ref[1:2, :] = eid2
    gate_ref[0:1, :] = gp * (1.0 / den)
    gate_ref[1:2, :] = gp * (e21 / den)

    sub64 = lax.broadcasted_iota(I32, (N_EXPERTS, tm), 0)
    oh1 = sub64 == eid1
    oh2 = sub64 == eid2
    ohs = jnp.where(oh1 | oh2, 1.0, 0.0)
    before = base_s[...] + _dot(ohs.astype(BF16), su_ref[...])
    rank_ref[0:1, :] = jnp.sum(jnp.where(oh1, before, 0.0), axis=0, keepdims=True).astype(I32)
    rank_ref[1:2, :] = jnp.sum(jnp.where(oh2, before, 0.0), axis=0, keepdims=True).astype(I32)
    base_new = base_s[...] + jnp.sum(ohs, axis=1, keepdims=True)
    base_s[...] = base_new
    cnt_ref[...] = jnp.broadcast_to(base_new, cnt_ref.shape)


def _outproj_router(x2, attn_raw, dn, attn_g, w_out, ln2_g, wg_r, bg_r, we_r, be_r, tm=512):
    n, d = x2.shape
    wr = jnp.concatenate([wg_r, we_r], axis=1).T.astype(F32)
    wr = jnp.pad(wr, ((0, ROUTER_ROWS - wr.shape[0]), (0, 0)))
    wrh, wrl = _split_hi_lo(wr)
    rb = jnp.pad(jnp.concatenate([bg_r, be_r]).astype(F32), (0, ROUTER_ROWS - N_GROUPS - N_EXPERTS))
    rb = rb.reshape(ROUTER_ROWS, 1)
    t_i = jnp.arange(tm)
    su = (t_i[:, None] < t_i[None, :]).astype(BF16)
    const = lambda i: (0, 0)
    row = lambda i: (i, 0)
    col = lambda i: (0, i)
    return pl.pallas_call(
        _outproj_body,
        grid=(n // tm,),
        in_specs=[
            pl.BlockSpec((tm, d), row),
            pl.BlockSpec((tm, ATTN_WIDTH), row),
            pl.BlockSpec((tm, DN_WIDTH), row),
            pl.BlockSpec((1, ATTN_WIDTH), const),
            pl.BlockSpec((ATTN_WIDTH + DN_WIDTH, d), const),
            pl.BlockSpec((1, d), const),
            pl.BlockSpec((ROUTER_ROWS, d), const),
            pl.BlockSpec((ROUTER_ROWS, d), const),
            pl.BlockSpec((ROUTER_ROWS, 1), const),
            pl.BlockSpec((tm, tm), const),
        ],
        out_specs=[
            pl.BlockSpec((tm, d), row),
            pl.BlockSpec((tm, TOK_SUB, LANES), lambda i: (i, 0, 0)),
            pl.BlockSpec((2, tm), col),
            pl.BlockSpec((2, tm), col),
            pl.BlockSpec((2, tm), col),
            pl.BlockSpec((N_EXPERTS, LANES), const),
        ],
        out_shape=[
            jax.ShapeDtypeStruct((n, d), F32),
            jax.ShapeDtypeStruct((n, TOK_SUB, LANES), F32),
            jax.ShapeDtypeStruct((2, n), I32),
            jax.ShapeDtypeStruct((2, n), F32),
            jax.ShapeDtypeStruct((2, n), I32),
            jax.ShapeDtypeStruct((N_EXPERTS, LANES), F32),
        ],
        scratch_shapes=[pltpu.VMEM((N_EXPERTS, 1), F32)],
        compiler_params=_cparams(("arbitrary",)),
    )(x2, attn_raw, dn, attn_g.reshape(1, -1).astype(F32), w_out.astype(BF16), ln2_g.reshape(1, d).astype(F32),
      wrh, wrl, rb, su)


def _dispatch_body(dest_ref, zrow_ref, h_ref, xs_hbm, zbuf, zsem, sem, *, n):
    tm = h_ref.shape[0]
    base = pl.program_id(0) * tm

    @pl.when(pl.program_id(0) == 0)
    def _():
        zbuf[...] = jnp.zeros_like(zbuf)

        def zero_copy(e):
            return pltpu.make_async_copy(zbuf, xs_hbm.at[pl.ds(zrow_ref[e], MOE_BLOCK)], zsem)

        def start(e, c):
            @pl.when(zrow_ref[e] >= 0)
            def _():
                zero_copy(e).start()
            return c

        def finish(e, c):
            @pl.when(zrow_ref[e] >= 0)
            def _():
                zero_copy(e).wait()
            return c

        lax.fori_loop(0, N_EXPERTS, start, 0)
        lax.fori_loop(0, N_EXPERTS, finish, 0)

        def tail_copy(b):
            return pltpu.make_async_copy(zbuf, xs_hbm.at[pl.ds(b * MOE_BLOCK, MOE_BLOCK)], zsem)

        def tail_start(b, c):
            tail_copy(b).start()
            return c

        def tail_finish(b, c):
            tail_copy(b).wait()
            return c

        n_blocks = xs_hbm.shape[0] // MOE_BLOCK
        lax.fori_loop(zrow_ref[N_EXPERTS], n_blocks, tail_start, 0)
        lax.fori_loop(zrow_ref[N_EXPERTS], n_blocks, tail_finish, 0)

    def issue(j, c):
        src = h_ref.at[pl.ds(j, 1)]
        pltpu.make_async_copy(src, xs_hbm.at[pl.ds(dest_ref[base + j], 1)], sem).start()
        pltpu.make_async_copy(src, xs_hbm.at[pl.ds(dest_ref[n + base + j], 1)], sem).start()
        return c

    lax.fori_loop(0, tm, issue, 0, unroll=8)
    for _ in range(2):
        pltpu.make_async_copy(h_ref, xs_hbm.at[pl.ds(0, tm)], sem).wait()


def _dispatch(h2t, dest, zrow, p_rows, tm=512):
    n = h2t.shape[0]
    grid_spec = pltpu.PrefetchScalarGridSpec(
        num_scalar_prefetch=2,
        grid=(n // tm,),
        in_specs=[pl.BlockSpec((tm, TOK_SUB, LANES), lambda i, dst, zr: (i, 0, 0))],
        out_specs=pl.BlockSpec(memory_space=pl.ANY),
        scratch_shapes=[pltpu.VMEM((MOE_BLOCK, TOK_SUB, LANES), F32), pltpu.SemaphoreType.DMA(()),
                        pltpu.SemaphoreType.DMA(())],
    )
    return pl.pallas_call(
        functools.partial(_dispatch_body, n=n),
        grid_spec=grid_spec,
        out_shape=jax.ShapeDtypeStruct((p_rows, TOK_SUB, LANES), F32),
        compiler_params=_cparams(("arbitrary",)),
    )(dest.reshape(-1), zrow, h2t)


def _expert_body(be_ref, nu_ref, xs_ref, wg_ref, wu_ref, wd_ref, o_ref):
    used = pl.program_id(0) < nu_ref[0]

    @pl.when(used)
    def _():
        x = _from_token_tiles(xs_ref[...]).astype(BF16)
        gt = _dot(x, wg_ref[0])
        up = _dot(x, wu_ref[0])
        hid = gt * jax.nn.sigmoid(gt) * up
        out = _dot(hid.astype(BF16), wd_ref[0])
        o_ref[...] = _to_token_tiles(out)

    @pl.when(jnp.logical_not(used))
    def _():
        o_ref[...] = jnp.zeros_like(o_ref)


def _experts(xs, block_expert, n_used, w_gate, w_up, w_down):
    p_rows = xs.shape[0]
    nb = p_rows // MOE_BLOCK
    d, de = w_gate.shape[-2:]
    wmap = lambda i, be, nu: (be[i], 0, 0)
    xmap = lambda i, be, nu: (jnp.minimum(i, nu[0] - 1), 0, 0)
    grid_spec = pltpu.PrefetchScalarGridSpec(
        num_scalar_prefetch=2,
        grid=(nb,),
        in_specs=[
            pl.BlockSpec((MOE_BLOCK, TOK_SUB, LANES), xmap),
            pl.BlockSpec((1, d, de), wmap),
            pl.BlockSpec((1, d, de), wmap),
            pl.BlockSpec((1, de, d), wmap),
        ],
        out_specs=pl.BlockSpec((MOE_BLOCK, TOK_SUB, LANES), lambda i, be, nu: (i, 0, 0)),
    )
    return pl.pallas_call(
        _expert_body,
        grid_spec=grid_spec,
        out_shape=jax.ShapeDtypeStruct((p_rows, TOK_SUB, LANES), F32),
        compiler_params=_cparams(("arbitrary",)),
    )(block_expert, n_used, xs, w_gate.astype(BF16), w_up.astype(BF16), w_down.astype(BF16))


def _combine_body(dest_ref, x1_ref, gt_ref, eo_hbm, out_ref, rbuf, sem, *, n):
    tm = x1_ref.shape[0]
    step = pl.program_id(0)
    slot = step % 2

    def gather(tile, slot_):
        base = tile * tm

        def issue(j, c):
            d0 = dest_ref[base + j]
            d1 = dest_ref[n + base + j]
            pltpu.make_async_copy(eo_hbm.at[pl.ds(d0, 1)], rbuf.at[slot_, 0, pl.ds(j, 1)], sem.at[slot_]).start()
            pltpu.make_async_copy(eo_hbm.at[pl.ds(d1, 1)], rbuf.at[slot_, 1, pl.ds(j, 1)], sem.at[slot_]).start()
            return c

        lax.fori_loop(0, tm, issue, 0, unroll=8)

    @pl.when(step == 0)
    def _():
        gather(0, 0)

    @pl.when(step + 1 < pl.num_programs(0))
    def _():
        gather(step + 1, 1 - slot)

    for k in range(2):
        pltpu.make_async_copy(eo_hbm.at[pl.ds(0, tm)], rbuf.at[slot, k], sem.at[slot]).wait()
    g0 = jnp.broadcast_to(gt_ref[:, 0:1], (tm, LANES))
    g1 = jnp.broadcast_to(gt_ref[:, 1:2], (tm, LANES))
    r0 = pltpu.einshape("psl->spl", rbuf[slot, 0])
    r1 = pltpu.einshape("psl->spl", rbuf[slot, 1])
    for s in range(TOK_SUB):
        cols = slice(s * LANES, (s + 1) * LANES)
        out_ref[:, cols] = x1_ref[:, cols] + (g0 * r0[s] + g1 * r1[s])


def _combine(x1, gates_t, dest, eo, tm=256):
    n, d = x1.shape
    grid_spec = pltpu.PrefetchScalarGridSpec(
        num_scalar_prefetch=1,
        grid=(n // tm,),
        in_specs=[
            pl.BlockSpec((tm, d), lambda i, dst: (i, 0)),
            pl.BlockSpec((tm, 2), lambda i, dst: (i, 0)),
            pl.BlockSpec(memory_space=pl.ANY),
        ],
        out_specs=pl.BlockSpec((tm, d), lambda i, dst: (i, 0)),
        scratch_shapes=[pltpu.VMEM((2, 2, tm, TOK_SUB, LANES), F32), pltpu.SemaphoreType.DMA((2,))],
    )
    return pl.pallas_call(
        functools.partial(_combine_body, n=n),
        grid_spec=grid_spec,
        out_shape=jax.ShapeDtypeStruct((n, d), F32),
        compiler_params=_cparams(("arbitrary",)),
    )(dest.reshape(-1), x1, gates_t, eo)


def _dispatch_plan(eid, rank, counts, n):
    nk = 2 * n
    p_rows = -(-nk // MOE_BLOCK) * MOE_BLOCK + N_EXPERTS * MOE_BLOCK
    nb = p_rows // MOE_BLOCK
    padded = -(-counts // MOE_BLOCK) * MOE_BLOCK
    pends = jnp.cumsum(padded)
    pstarts = pends - padded
    experts = jnp.arange(N_EXPERTS, dtype=I32)
    dest = jnp.sum(jnp.where(eid[..., None] == experts, pstarts.astype(I32), 0), axis=-1) + rank
    block_start = jnp.arange(nb, dtype=I32) * MOE_BLOCK
    block_expert = jnp.clip(jnp.sum((pends[None, :] <= block_start[:, None]).astype(I32), axis=-1),
                            0, N_EXPERTS - 1).astype(I32)
    n_used = (pends[-1] // MOE_BLOCK).astype(I32).reshape(1)
    zrow = jnp.concatenate([jnp.where(padded > 0, pends - MOE_BLOCK, -1).astype(I32), n_used])
    return dest.astype(I32), zrow, block_expert, n_used, p_rows


def _layer(x, ln1_g, w_in, q_g, k_g, rpb, attn_g, conv_w, a_log, dt_bias, dn_g, w_out, ln2_g,
           wg_r, bg_r, we_r, be_r, w_gate, w_up, w_down):
    b, s_len, d = x.shape
    n = b * s_len
    x2 = x.reshape(n, d)
    aqkv, dqkv, dz, dab = _inproj(x2, ln1_g, w_in)
    attn_raw = _attention(aqkv.reshape(b, s_len, -1), q_g, k_g, rpb)
    dn = _deltanet(dqkv.reshape(b, s_len, -1), dz.reshape(b, s_len, -1), dab.reshape(b, s_len, -1),
                   conv_w, a_log, dt_bias, dn_g)
    x1, h2t, eid, gates, rank, cnt = _outproj_router(
        x2, attn_raw.reshape(n, -1), dn.reshape(n, -1), attn_g, w_out, ln2_g, wg_r, bg_r, we_r, be_r)
    counts = cnt[:, 0].astype(I32)
    dest, zrow, block_expert, n_used, p_rows = _dispatch_plan(eid, rank, counts, n)
    xs = _dispatch(h2t, dest, zrow, p_rows)
    eo = _experts(xs, block_expert, n_used, w_gate, w_up, w_down)
    out = _combine(x1, gates.T, dest, eo)
    return out.reshape(b, s_len, d)


def kernel(x, ln1_g, w_in, attn_q_norm_g, attn_k_norm_g, attn_rpb, attn_out_norm_g, dn_conv_w, dn_a_log,
           dn_dt_bias, dn_out_norm_g, w_out, ln2_g, router_group_w, router_group_b, router_expert_w,
           router_expert_b, expert_w_gate, expert_w_up, expert_w_down):
    for l in range(ln1_g.shape[0]):
        x = _layer(x, ln1_g[l], w_in[l], attn_q_norm_g[l], attn_k_norm_g[l], attn_rpb[l],
                   attn_out_norm_g[l], dn_conv_w[l], dn_a_log[l], dn_dt_bias[l], dn_out_norm_g[l],
                   w_out[l], ln2_g[l], router_group_w[l], router_group_b[l], router_expert_w[l],
                   router_expert_b[l], expert_w_gate[l], expert_w_up[l], expert_w_down[l])
    return x
```

```python
import functools

import jax
import jax.numpy as jnp
from jax import lax
from jax.experimental import pallas as pl
from jax.experimental.pallas import tpu as pltpu

F32 = jnp.float32
BF16 = jnp.bfloat16
I32 = jnp.int32

EPS = 1e-6
GRID_W = 64
WIN_H = 8
WIN_W = 16
ATTN_HEADS = 8
ATTN_HD = 64
ATTN_WIDTH = ATTN_HEADS * ATTN_HD
DN_HEADS = 4
DN_HD = 128
DN_WIDTH = DN_HEADS * DN_HD
DN_CONV = 5
DN_CHUNK = 64
N_GROUPS = 8
EPG = 8
N_EXPERTS = N_GROUPS * EPG
MOE_BLOCK = 256
NEG = -1e30

ATTN_UNROLL = 8
LANES = 128
VMEM_LIMIT = 56 * 1024 * 1024


def _cparams(sem):
    return pltpu.CompilerParams(dimension_semantics=sem, vmem_limit_bytes=VMEM_LIMIT)


def _dot(a, b):
    return jnp.dot(a, b, preferred_element_type=F32)


def _dot_nt(a, b):
    return lax.dot_general(a, b, (((1,), (1,)), ((), ())), preferred_element_type=F32)


def _split_hi_lo(x):
    hi = x.astype(BF16)
    lo = (x - hi.astype(F32)).astype(BF16)
    return hi, lo


def _inproj_body(x_ref, g_ref, wa_ref, wd_ref, wz_ref, wab_ref, oa_ref, od_ref, oz_ref, oab_ref):
    x = x_ref[...]
    ms = jnp.mean(x * x, axis=-1, keepdims=True)
    h = x * lax.rsqrt(ms + EPS) * g_ref[...]
    hi, lo = _split_hi_lo(h)
    oa_ref[...] = _dot(hi, wa_ref[...]).astype(BF16)
    od_ref[...] = _dot(hi, wd_ref[...]).astype(BF16)
    oz_ref[...] = _dot(hi, wz_ref[...]).astype(BF16)
    nab = oab_ref.shape[-1]
    ab = _dot(hi, wab_ref[...])
    ab_lo = _dot(lo, wab_ref[:, :nab])
    oab_ref[...] = ab[:, :nab] + ab[:, nab:] + ab_lo


def _inproj(x2, ln1_g, w_in, tm=512):
    n, d = x2.shape
    a3 = 3 * ATTN_WIDTH
    d3 = 3 * DN_WIDTH
    wa = w_in[:, :a3].astype(BF16)
    wd = w_in[:, a3:a3 + d3].astype(BF16)
    wz = w_in[:, a3 + d3:a3 + d3 + DN_WIDTH].astype(BF16)
    wab_f = w_in[:, a3 + d3 + DN_WIDTH:]
    nab = wab_f.shape[1]
    wab_hi, wab_lo = _split_hi_lo(wab_f)
    wab = jnp.concatenate([wab_hi, wab_lo], axis=1)
    const = lambda i: (0, 0)
    row = lambda i: (i, 0)
    return pl.pallas_call(
        _inproj_body,
        grid=(n // tm,),
        in_specs=[
            pl.BlockSpec((tm, d), row),
            pl.BlockSpec((1, d), const),
            pl.BlockSpec((d, a3), const),
            pl.BlockSpec((d, d3), const),
            pl.BlockSpec((d, DN_WIDTH), const),
            pl.BlockSpec((d, 2 * nab), const),
        ],
        out_specs=[
            pl.BlockSpec((tm, a3), row),
            pl.BlockSpec((tm, d3), row),
            pl.BlockSpec((tm, DN_WIDTH), row),
            pl.BlockSpec((tm, nab), row),
        ],
        out_shape=[
            jax.ShapeDtypeStruct((n, a3), BF16),
            jax.ShapeDtypeStruct((n, d3), BF16),
            jax.ShapeDtypeStruct((n, DN_WIDTH), BF16),
            jax.ShapeDtypeStruct((n, nab), F32),
        ],
        compiler_params=_cparams(("parallel",)),
    )(x2, ln1_g.reshape(1, d), wa, wd, wz, wab)


def _attn_bias_table(rpb):
    c = jnp.arange(GRID_W)
    c0 = jnp.clip(c - WIN_W // 2, 0, GRID_W - WIN_W)
    kc = jnp.arange(GRID_W)
    inwin = (kc[None, :] >= c0[:, None]) & (kc[None, :] < c0[:, None] + WIN_W)
    h = rpb.shape[0]
    padw = GRID_W - WIN_W
    rp = jnp.pad(rpb.astype(F32), ((0, 0), (0, 0), (padw, padw)))
    by_col = jnp.stack([rp[:, :, GRID_W - 1 - cc:2 * GRID_W - 1 - cc] for cc in range(GRID_W)], axis=2)
    tab = jnp.stack([by_col[:, WIN_H - 1 - dd:2 * WIN_H - 1 - dd] for dd in range(WIN_H)], axis=1)
    tab = tab.transpose(0, 1, 3, 2, 4)
    tab = jnp.where(inwin[None, None, :, None, :], tab, NEG)
    tab = tab.reshape(h // 2, 2, WIN_H, GRID_W, WIN_H * GRID_W)
    return tab.transpose(0, 2, 1, 3, 4).reshape(h // 2, WIN_H, 2 * GRID_W, WIN_H * GRID_W)


def _attn_body(q_ref, k_ref, v_ref, qg_ref, kg_ref, bias_ref, o_ref, qlo_s, qhi_s, kn_s, *, rows):
    s_len = rows * GRID_W
    lane = lax.broadcasted_iota(I32, (1, LANES), 1)
    is_lo = lane < ATTN_HD
    blk = 256

    hr = lax.broadcasted_iota(I32, (LANES, LANES), 0) // ATTN_HD
    hc = lax.broadcasted_iota(I32, (LANES, LANES), 1) // ATTN_HD
    same_head = jnp.where(hr == hc, 1.0, 0.0).astype(BF16)

    def head_norm(x, gain):
        ms = _dot((x * x).astype(BF16), same_head) * (1.0 / ATTN_HD)
        return x * lax.rsqrt(ms + EPS) * gain

    def prep(i, c):
        sl = pl.ds(pl.multiple_of(i * blk, blk), blk)
        qn = head_norm(q_ref[0, sl, :].astype(F32), qg_ref[...]) * (ATTN_HD ** -0.5)
        qlo_s[sl, :] = jnp.where(is_lo, qn, 0.0).astype(BF16)
        qhi_s[sl, :] = jnp.where(is_lo, 0.0, qn).astype(BF16)
        kn_s[sl, :] = head_norm(k_ref[0, sl, :].astype(F32), kg_ref[...]).astype(BF16)
        return c

    lax.fori_loop(0, s_len // blk, prep, 0)

    band = WIN_H * GRID_W

    def row_step(i, c):
        ctx = []
        for u in range(ATTN_UNROLL):
            r = i * ATTN_UNROLL + u
            r0 = jnp.clip(r - WIN_H // 2, 0, rows - WIN_H)
            qsl = pl.ds(pl.multiple_of(r * GRID_W, GRID_W), GRID_W)
            ksl = pl.ds(pl.multiple_of(r0 * GRID_W, GRID_W), band)
            q2 = jnp.concatenate([qlo_s[qsl, :], qhi_s[qsl, :]], axis=0)
            ctx.append(dict(d=r - r0, qsl=qsl, ksl=ksl, s=_dot_nt(q2, kn_s[ksl, :])))
        for c_ in ctx:
            s = c_['s'] + bias_ref[0, c_['d']]
            m = jnp.max(s, axis=-1, keepdims=True)
            p = jnp.exp(s - m)
            c_['l'] = jnp.sum(p, axis=-1, keepdims=True)
            c_['p'] = p.astype(BF16)
        for c_ in ctx:
            c_['pv'] = _dot(c_['p'], v_ref[0, c_['ksl'], :])
        for c_ in ctx:
            pv = c_['pv'] / c_['l']
            o = jnp.where(is_lo, pv[:GRID_W], pv[GRID_W:])
            o_ref[0, c_['qsl'], :] = o.astype(o_ref.dtype)
        return c

    lax.fori_loop(0, rows // ATTN_UNROLL, row_step, 0)


def _attention(aqkv, q_g, k_g, rpb):
    b, s_len, _ = aqkv.shape
    rows = s_len // GRID_W
    assert min(WIN_H, rows) == WIN_H
    pairs = ATTN_HEADS // 2
    bias = _attn_bias_table(rpb)
    qg2 = jnp.tile(q_g, 2).reshape(1, LANES).astype(F32)
    kg2 = jnp.tile(k_g, 2).reshape(1, LANES).astype(F32)
    band = WIN_H * GRID_W
    return pl.pallas_call(
        functools.partial(_attn_body, rows=rows),
        grid=(pairs, b),
        in_specs=[
            pl.BlockSpec((1, s_len, LANES), lambda p, i: (i, 0, p)),
            pl.BlockSpec((1, s_len, LANES), lambda p, i: (i, 0, pairs + p)),
            pl.BlockSpec((1, s_len, LANES), lambda p, i: (i, 0, 2 * pairs + p)),
            pl.BlockSpec((1, LANES), lambda p, i: (0, 0)),
            pl.BlockSpec((1, LANES), lambda p, i: (0, 0)),
            pl.BlockSpec((1, WIN_H, 2 * GRID_W, band), lambda p, i: (p, 0, 0, 0)),
        ],
        out_specs=pl.BlockSpec((1, s_len, LANES), lambda p, i: (i, 0, p)),
        out_shape=jax.ShapeDtypeStruct((b, s_len, ATTN_WIDTH), BF16),
        scratch_shapes=[
            pltpu.VMEM((s_len, LANES), BF16),
            pltpu.VMEM((s_len, LANES), BF16),
            pltpu.VMEM((s_len, LANES), BF16),
        ],
        compiler_params=_cparams(("arbitrary", "arbitrary")),
    )(aqkv, aqkv, aqkv, qg2, kg2, bias)


SUPER = 2 * DN_CHUNK
HALO = 16
DN_UNROLL = 8


def _dn_body(q_ref, k_ref, v_ref, z_ref, ab_ref, cwq_ref, cwk_ref, cwv_ref, alog_ref, dtb_ref, og_ref,
             o_ref, pad_s, qs, ks, vs, of_s, ob_s, *, s_len):
    head = pl.program_id(1)
    n_super = s_len // SUPER

    zeros_halo = jnp.zeros((HALO, LANES), F32)
    for t, src in enumerate((q_ref, k_ref, v_ref)):
        pad_s[t, 0:HALO, :] = zeros_halo
        pad_s[t, HALO + s_len:2 * HALO + s_len, :] = zeros_halo

        def stage(i, c, t=t, src=src):
            off = pl.multiple_of(i * 512, 512)
            pad_s[t, pl.ds(HALO + off, 512), :] = src[0, pl.ds(off, 512), :].astype(F32)
            return c

        lax.fori_loop(0, s_len // 512, stage, 0)

    win = SUPER + HALO

    def conv_step(i, c):
        t0 = pl.multiple_of(i * SUPER, SUPER)
        outs = []
        for t, cw_ref in enumerate((cwq_ref, cwk_ref, cwv_ref)):
            xw = pad_s[t, pl.ds(t0 + HALO // 2, win), :]
            acc = jnp.zeros((SUPER, LANES), F32)
            for j in range(DN_CONV):
                sh = (DN_CONV // 2 - j) % win
                xr = xw if sh == 0 else pltpu.roll(xw, sh, axis=0)
                acc = acc + xr[HALO // 2:HALO // 2 + SUPER] * cw_ref[j:j + 1, :]
            outs.append(acc * jax.nn.sigmoid(acc))
        qc, kc, vc = outs
        sl = pl.ds(t0, SUPER)
        qs[sl, :] = qc * lax.rsqrt(jnp.sum(qc * qc, axis=-1, keepdims=True) + EPS) * (DN_HD ** -0.5)
        ks[sl, :] = kc * lax.rsqrt(jnp.sum(kc * kc, axis=-1, keepdims=True) + EPS)
        vs[sl, :] = vc
        return c

    lax.fori_loop(0, n_super, conv_step, 0)

    ri = lax.broadcasted_iota(I32, (SUPER, SUPER), 0)
    ci = lax.broadcasted_iota(I32, (SUPER, SUPER), 1)
    same = (ri // DN_CHUNK) == (ci // DN_CHUNK)
    eye = (ri == ci).astype(F32)
    posl = lax.broadcasted_iota(I32, (8, LANES), 1) % DN_CHUNK

    def prep(sc, rev):
        t0 = pl.multiple_of(sc * SUPER, SUPER)
        sl = pl.ds(t0, SUPER)
        q = qs[sl, :]
        k = ks[sl, :]
        v = vs[sl, :]
        dcol = 1 if rev else 0
        a_r = jnp.broadcast_to(ab_ref[0, 0, dcol:dcol + 1, sl], (8, LANES))
        b_r = jnp.broadcast_to(ab_ref[0, 0, 2 + dcol:3 + dcol, sl], (8, LANES))
        neg_rate = -jnp.exp(jnp.full((1, LANES), alog_ref[dcol, head], F32))
        g_r = neg_rate * jax.nn.softplus(a_r + dtb_ref[dcol, head])
        beta_r = jax.nn.sigmoid(b_r)
        gc_r = g_r
        for sft in (1, 2, 4, 8, 16, 32):
            if rev:
                gc_r = gc_r + jnp.where(posl < DN_CHUNK - sft, pltpu.roll(gc_r, SUPER - sft, axis=1), 0.0)
            else:
                gc_r = gc_r + jnp.where(posl >= sft, pltpu.roll(gc_r, sft, axis=1), 0.0)
        gct = jnp.broadcast_to(gc_r[0:1], (SUPER, LANES))
        gc = gct.T
        beta = jnp.broadcast_to(beta_r[0:1], (SUPER, LANES)).T
        diff = gc - gct
        if rev:
            incl = same & (ri <= ci)
            strict = same & (ri < ci)
        else:
            incl = same & (ri >= ci)
            strict = same & (ri > ci)
        decay = jnp.where(incl, jnp.exp(jnp.where(incl, diff, 0.0)), 0.0)
        kb = k * beta
        gram = _dot_nt(jnp.concatenate([q, kb], axis=0).astype(BF16), k.astype(BF16))
        qkd = (gram[:SUPER] * decay).astype(BF16)
        low = jnp.where(strict, gram[SUPER:] * decay, 0.0)
        egc = jnp.exp(gc)
        rhs = jnp.concatenate([v * beta, kb * egc], axis=1).astype(BF16)
        return dict(sl=sl, rev=rev, gc=gc, gct=gct, qkd=qkd, rhs=rhs, qe=q * egc, kt=k.T,
                    inv=eye - low, pw=low.astype(BF16))

    def phase1(chains):
        ctx = [prep(sc, rev) for sc, rev in chains]
        for _ in range(5):
            for c in ctx:
                c['pw'] = _dot(c['pw'], c['pw']).astype(BF16)
            for c in ctx:
                c['inv'] = c['inv'] + _dot(c['inv'].astype(BF16), c['pw'])
        for c in ctx:
            c['sol'] = _dot(c['inv'].astype(BF16), c['rhs']).astype(BF16)
        for c in ctx:
            qo = _dot(c['qkd'], c['sol'])
            c['o0'] = qo[:, :LANES]
            c['qt'] = c['qe'] - qo[:, LANES:]
        out = []
        for c in ctx:
            rev, gc, gct = c['rev'], c['gc'], c['gct']
            per_chunk = []
            for cidx in ((1, 0) if rev else (0, 1)):
                lo = cidx * DN_CHUNK
                last = lo if rev else lo + DN_CHUNK - 1
                g_last = gc[last:last + 1, :]
                in_c = (ci // DN_CHUNK) == cidx
                kst = c['kt'] * jnp.where(in_c, jnp.exp(jnp.where(in_c, g_last - gct, 0.0)), 0.0)
                mn = _dot(kst.astype(BF16), c['sol'])
                lhs = jnp.concatenate([-mn[:, LANES:], c['qt'][lo:lo + DN_CHUNK]], axis=0).astype(BF16)
                per_chunk.append((cidx, lhs, mn[:, :LANES], jnp.exp(g_last), c['o0'][lo:lo + DN_CHUNK]))
            out.append((c['sl'], per_chunk))
        return out

    def scan(state, p1, rev):
        sl, per_chunk = p1
        outs = [None, None]
        for cidx, lhs, ku, e_last, o0_c in per_chunk:
            r = _dot(lhs, state.astype(BF16))
            outs[cidx] = r[LANES:] + o0_c
            state = e_last * state + r[:LANES] + ku
        dst = ob_s if rev else of_s
        dst[sl, :] = jnp.concatenate(outs, axis=0)
        return state

    def scan_step(i, carry):
        s_f, s_b = carry
        chains = []
        for u in range(DN_UNROLL):
            chains += [(i * DN_UNROLL + u, False), (n_super - 1 - (i * DN_UNROLL + u), True)]
        p1 = phase1(chains)
        p_f, p_b = p1[0::2], p1[1::2]
        for u in range(DN_UNROLL):
            s_f = scan(s_f, p_f[u], False)
            s_b = scan(s_b, p_b[u], True)
        return s_f, s_b

    z0 = jnp.zeros((DN_HD, DN_HD), F32)
    lax.fori_loop(0, n_super // DN_UNROLL, scan_step, (z0, z0))

    def fin(i, c):
        sl = pl.ds(pl.multiple_of(i * 512, 512), 512)
        o = of_s[sl, :] + ob_s[sl, :]
        o = o * lax.rsqrt(jnp.mean(o * o, axis=-1, keepdims=True) + EPS) * og_ref[...]
        zf = z_ref[0, sl, :].astype(F32)
        o_ref[0, sl, :] = (o * (zf * jax.nn.sigmoid(zf))).astype(o_ref.dtype)
        return c

    lax.fori_loop(0, s_len // 512, fin, 0)


def _deltanet(dqkv, dz, dab, conv_w, a_log, dt_bias, out_g):
    b, s_len, _ = dqkv.shape
    h = DN_HEADS
    ab = dab.reshape(b, s_len, 2, 2, h).transpose(0, 4, 2, 3, 1).reshape(b, h, 4, s_len)
    cw = conv_w.astype(F32)
    tok = lambda off: (lambda i, j: (i, 0, off + j))
    smem = pl.BlockSpec(memory_space=pltpu.SMEM)
    return pl.pallas_call(
        functools.partial(_dn_body, s_len=s_len),
        grid=(b, h),
        in_specs=[
            pl.BlockSpec((1, s_len, LANES), tok(0)),
            pl.BlockSpec((1, s_len, LANES), tok(h)),
            pl.BlockSpec((1, s_len, LANES), tok(2 * h)),
            pl.BlockSpec((1, s_len, LANES), tok(0)),
            pl.BlockSpec((1, 1, 4, s_len), lambda i, j: (i, j, 0, 0)),
            pl.BlockSpec((DN_CONV, LANES), lambda i, j: (0, j)),
            pl.BlockSpec((DN_CONV, LANES), lambda i, j: (0, h + j)),
            pl.BlockSpec((DN_CONV, LANES), lambda i, j: (0, 2 * h + j)),
            smem,
            smem,
            pl.BlockSpec((1, LANES), lambda i, j: (0, 0)),
        ],
        out_specs=pl.BlockSpec((1, s_len, LANES), tok(0)),
        out_shape=jax.ShapeDtypeStruct((b, s_len, DN_WIDTH), BF16),
        scratch_shapes=[
            pltpu.VMEM((3, s_len + 2 * HALO, LANES), F32),
            pltpu.VMEM((s_len, LANES), F32),
            pltpu.VMEM((s_len, LANES), F32),
            pltpu.VMEM((s_len, LANES), F32),
            pltpu.VMEM((s_len, LANES), F32),
            pltpu.VMEM((s_len, LANES), F32),
        ],
        compiler_params=_cparams(("arbitrary", "arbitrary")),
    )(dqkv, dqkv, dqkv, dz, ab, cw, cw, cw, a_log.astype(F32), dt_bias.astype(F32),
      out_g.reshape(1, LANES).astype(F32))


ROUTER_ROWS = 80
TOK_SUB = 8


def _to_token_tiles(x):
    blocks = jnp.stack([x[:, s * LANES:(s + 1) * LANES] for s in range(TOK_SUB)], axis=0)
    return pltpu.einshape("spl->psl", blocks)


def _from_token_tiles(t):
    blocks = pltpu.einshape("psl->spl", t)
    return jnp.concatenate([blocks[s] for s in range(TOK_SUB)], axis=1)


def _outproj_body(x_ref, at_ref, dn_ref, ag_ref, wo_ref, lg_ref, wrh_ref, wrl_ref, rb_ref, su_ref,
                  x1_ref, h2_ref, eid_ref, gate_ref, rank_ref, cnt_ref, base_s):
    step = pl.program_id(0)
    tm = x_ref.shape[0]

    @pl.when(step == 0)
    def _():
        base_s[...] = jnp.zeros_like(base_s)

    a = at_ref[...].astype(F32)
    an = a * lax.rsqrt(jnp.mean(a * a, axis=-1, keepdims=True) + EPS) * ag_ref[...]
    mix = jnp.concatenate([an.astype(BF16), dn_ref[...]], axis=1)
    x1 = x_ref[...] + _dot(mix, wo_ref[...])
    x1_ref[...] = x1
    h2 = x1 * lax.rsqrt(jnp.mean(x1 * x1, axis=-1, keepdims=True) + EPS) * lg_ref[...]
    h2_ref[...] = _to_token_tiles(h2)

    hi, lo = _split_hi_lo(h2)
    wrh = wrh_ref[...]
    lt = _dot_nt(wrh, hi) + _dot_nt(wrh, lo) + _dot_nt(wrl_ref[...], hi) + rb_ref[...]

    sub8 = lax.broadcasted_iota(I32, (N_GROUPS, tm), 0)
    gl = lt[0:N_GROUPS]
    gmax = jnp.max(gl, axis=0, keepdims=True)
    gsel = jnp.min(jnp.where(gl == gmax, sub8, N_GROUPS), axis=0, keepdims=True)
    gp = 1.0 / jnp.sum(jnp.exp(gl - gmax), axis=0, keepdims=True)
    in_group = jnp.zeros((EPG, tm), F32)
    for g in range(N_GROUPS):
        in_group = in_group + jnp.where(gsel == g, lt[N_GROUPS + g * EPG:N_GROUPS + (g + 1) * EPG], 0.0)
    v1 = jnp.max(in_group, axis=0, keepdims=True)
    i1 = jnp.min(jnp.where(in_group == v1, sub8, EPG), axis=0, keepdims=True)
    rest = jnp.where(sub8 == i1, -jnp.inf, in_group)
    v2 = jnp.max(rest, axis=0, keepdims=True)
    i2 = jnp.min(jnp.where(rest == v2, sub8, EPG), axis=0, keepdims=True)
    e21 = jnp.exp(v2 - v1)
    den = 1.0 + e21
    eid1 = gsel * EPG + i1
    eid2 = gsel * EPG + i2
    eid_ref[0:1, :] = eid1
    eid_ref[1:2, :] = eid2
    gate_ref[0:1, :] = gp * (1.0 / den)
    gate_ref[1:2, :] = gp * (e21 / den)

    sub64 = lax.broadcasted_iota(I32, (N_EXPERTS, tm), 0)
    oh1 = sub64 == eid1
    oh2 = sub64 == eid2
    ohs = jnp.where(oh1 | oh2, 1.0, 0.0)
    before = base_s[...] + _dot(ohs.astype(BF16), su_ref[...])
    rank_ref[0:1, :] = jnp.sum(jnp.where(oh1, before, 0.0), axis=0, keepdims=True).astype(I32)
    rank_ref[1:2, :] = jnp.sum(jnp.where(oh2, before, 0.0), axis=0, keepdims=True).astype(I32)
    base_new = base_s[...] + jnp.sum(ohs, axis=1, keepdims=True)
    base_s[...] = base_new
    cnt_ref[...] = jnp.broadcast_to(base_new, cnt_ref.shape)


def _outproj_router(x2, attn_raw, dn, attn_g, w_out, ln2_g, wg_r, bg_r, we_r, be_r, tm=512):
    n, d = x2.shape
    wr = jnp.concatenate([wg_r, we_r], axis=1).T.astype(F32)
    wr = jnp.pad(wr, ((0, ROUTER_ROWS - wr.shape[0]), (0, 0)))
    wrh, wrl = _split_hi_lo(wr)
    rb = jnp.pad(jnp.concatenate([bg_r, be_r]).astype(F32), (0, ROUTER_ROWS - N_GROUPS - N_EXPERTS))
    rb = rb.reshape(ROUTER_ROWS, 1)
    t_i = jnp.arange(tm)
    su = (t_i[:, None] < t_i[None, :]).astype(BF16)
    const = lambda i: (0, 0)
    row = lambda i: (i, 0)
    col = lambda i: (0, i)
    return pl.pallas_call(
        _outproj_body,
        grid=(n // tm,),
        in_specs=[
            pl.BlockSpec((tm, d), row),
            pl.BlockSpec((tm, ATTN_WIDTH), row),
            pl.BlockSpec((tm, DN_WIDTH), row),
            pl.BlockSpec((1, ATTN_WIDTH), const),
            pl.BlockSpec((ATTN_WIDTH + DN_WIDTH, d), const),
            pl.BlockSpec((1, d), const),
            pl.BlockSpec((ROUTER_ROWS, d), const),
            pl.BlockSpec((ROUTER_ROWS, d), const),
            pl.BlockSpec((ROUTER_ROWS, 1), const),
            pl.BlockSpec((tm, tm), const),
        ],
        out_specs=[
            pl.BlockSpec((tm, d), row),
            pl.BlockSpec((tm, TOK_SUB, LANES), lambda i: (i, 0, 0)),
            pl.BlockSpec((2, tm), col),
            pl.BlockSpec((2, tm), col),
            pl.BlockSpec((2, tm), col),
            pl.BlockSpec((N_EXPERTS, LANES), const),
        ],
        out_shape=[
            jax.ShapeDtypeStruct((n, d), F32),
            jax.ShapeDtypeStruct((n, TOK_SUB, LANES), F32),
            jax.ShapeDtypeStruct((2, n), I32),
            jax.ShapeDtypeStruct((2, n), F32),
            jax.ShapeDtypeStruct((2, n), I32),
            jax.ShapeDtypeStruct((N_EXPERTS, LANES), F32),
        ],
        scratch_shapes=[pltpu.VMEM((N_EXPERTS, 1), F32)],
        compiler_params=_cparams(("arbitrary",)),
    )(x2, attn_raw, dn, attn_g.reshape(1, -1).astype(F32), w_out.astype(BF16), ln2_g.reshape(1, d).astype(F32),
      wrh, wrl, rb, su)


def _dispatch_body(dest_ref, zrow_ref, h_ref, xs_hbm, zbuf, zsem, sem, *, n):
    tm = h_ref.shape[0]
    base = pl.program_id(0) * tm

    @pl.when(pl.program_id(0) == 0)
    def _():
        zbuf[...] = jnp.zeros_like(zbuf)

        def zero_copy(e):
            return pltpu.make_async_copy(zbuf, xs_hbm.at[pl.ds(zrow_ref[e], MOE_BLOCK)], zsem)

        def start(e, c):
            @pl.when(zrow_ref[e] >= 0)
            def _():
                zero_copy(e).start()
            return c

        def finish(e, c):
            @pl.when(zrow_ref[e] >= 0)
            def _():
                zero_copy(e).wait()
            return c

        lax.fori_loop(0, N_EXPERTS, start, 0)
        lax.fori_loop(0, N_EXPERTS, finish, 0)

        def tail_copy(b):
            return pltpu.make_async_copy(zbuf, xs_hbm.at[pl.ds(b * MOE_BLOCK, MOE_BLOCK)], zsem)

        def tail_start(b, c):
            tail_copy(b).start()
            return c

        def tail_finish(b, c):
            tail_copy(b).wait()
            return c

        n_blocks = xs_hbm.shape[0] // MOE_BLOCK
        lax.fori_loop(zrow_ref[N_EXPERTS], n_blocks, tail_start, 0)
        lax.fori_loop(zrow_ref[N_EXPERTS], n_blocks, tail_finish, 0)

    def issue(j, c):
        src = h_ref.at[pl.ds(j, 1)]
        pltpu.make_async_copy(src, xs_hbm.at[pl.ds(dest_ref[base + j], 1)], sem).start()
        pltpu.make_async_copy(src, xs_hbm.at[pl.ds(dest_ref[n + base + j], 1)], sem).start()
        return c

    lax.fori_loop(0, tm, issue, 0, unroll=8)
    for _ in range(2):
        pltpu.make_async_copy(h_ref, xs_hbm.at[pl.ds(0, tm)], sem).wait()


def _dispatch(h2t, dest, zrow, p_rows, tm=512):
    n = h2t.shape[0]
    grid_spec = pltpu.PrefetchScalarGridSpec(
        num_scalar_prefetch=2,
        grid=(n // tm,),
        in_specs=[pl.BlockSpec((tm, TOK_SUB, LANES), lambda i, dst, zr: (i, 0, 0))],
        out_specs=pl.BlockSpec(memory_space=pl.ANY),
        scratch_shapes=[pltpu.VMEM((MOE_BLOCK, TOK_SUB, LANES), F32), pltpu.SemaphoreType.DMA(()),
                        pltpu.SemaphoreType.DMA(())],
    )
    return pl.pallas_call(
        functools.partial(_dispatch_body, n=n),
        grid_spec=grid_spec,
        out_shape=jax.ShapeDtypeStruct((p_rows, TOK_SUB, LANES), F32),
        compiler_params=_cparams(("arbitrary",)),
    )(dest.reshape(-1), zrow, h2t)


def _expert_body(be_ref, nu_ref, xs_ref, wg_ref, wu_ref, wd_ref, o_ref):
    used = pl.program_id(0) < nu_ref[0]

    @pl.when(used)
    def _():
        x = _from_token_tiles(xs_ref[...]).astype(BF16)
        gt = _dot(x, wg_ref[0])
        up = _dot(x, wu_ref[0])
        hid = gt * jax.nn.sigmoid(gt) * up
        out = _dot(hid.astype(BF16), wd_ref[0])
        o_ref[...] = _to_token_tiles(out)

    @pl.when(jnp.logical_not(used))
    def _():
        o_ref[...] = jnp.zeros_like(o_ref)


def _experts(xs, block_expert, n_used, w_gate, w_up, w_down):
    p_rows = xs.shape[0]
    nb = p_rows // MOE_BLOCK
    d, de = w_gate.shape[-2:]
    wmap = lambda i, be, nu: (be[i], 0, 0)
    xmap = lambda i, be, nu: (jnp.minimum(i, nu[0] - 1), 0, 0)
    grid_spec = pltpu.PrefetchScalarGridSpec(
        num_scalar_prefetch=2,
        grid=(nb,),
        in_specs=[
            pl.BlockSpec((MOE_BLOCK, TOK_SUB, LANES), xmap),
            pl.BlockSpec((1, d, de), wmap),
            pl.BlockSpec((1, d, de), wmap),
            pl.BlockSpec((1, de, d), wmap),
        ],
        out_specs=pl.BlockSpec((MOE_BLOCK, TOK_SUB, LANES), lambda i, be, nu: (i, 0, 0)),
    )
    return pl.pallas_call(
        _expert_body,
        grid_spec=grid_spec,
        out_shape=jax.ShapeDtypeStruct((p_rows, TOK_SUB, LANES), F32),
        compiler_params=_cparams(("arbitrary",)),
    )(block_expert, n_used, xs, w_gate.astype(BF16), w_up.astype(BF16), w_down.astype(BF16))


def _combine_body(dest_ref, x1_ref, gt_ref, eo_hbm, out_ref, rbuf, sem, *, n):
    tm = x1_ref.shape[0]
    step = pl.program_id(0)
    slot = step % 2

    def gather(tile, slot_):
        base = tile * tm

        def issue(j, c):
            d0 = dest_ref[base + j]
            d1 = dest_ref[n + base + j]
            pltpu.make_async_copy(eo_hbm.at[pl.ds(d0, 1)], rbuf.at[slot_, 0, pl.ds(j, 1)], sem.at[slot_]).start()
            pltpu.make_async_copy(eo_hbm.at[pl.ds(d1, 1)], rbuf.at[slot_, 1, pl.ds(j, 1)], sem.at[slot_]).start()
            return c

        lax.fori_loop(0, tm, issue, 0, unroll=8)

    @pl.when(step == 0)
    def _():
        gather(0, 0)

    @pl.when(step + 1 < pl.num_programs(0))
    def _():
        gather(step + 1, 1 - slot)

    for k in range(2):
        pltpu.make_async_copy(eo_hbm.at[pl.ds(0, tm)], rbuf.at[slot, k], sem.at[slot]).wait()
    g0 = jnp.broadcast_to(gt_ref[:, 0:1], (tm, LANES))
    g1 = jnp.broadcast_to(gt_ref[:, 1:2], (tm, LANES))
    r0 = pltpu.einshape("psl->spl", rbuf[slot, 0])
    r1 = pltpu.einshape("psl->spl", rbuf[slot, 1])
    for s in range(TOK_SUB):
        cols = slice(s * LANES, (s + 1) * LANES)
        out_ref[:, cols] = x1_ref[:, cols] + (g0 * r0[s] + g1 * r1[s])


def _combine(x1, gates_t, dest, eo, tm=256):
    n, d = x1.shape
    grid_spec = pltpu.PrefetchScalarGridSpec(
        num_scalar_prefetch=1,
        grid=(n // tm,),
        in_specs=[
            pl.BlockSpec((tm, d), lambda i, dst: (i, 0)),
            pl.BlockSpec((tm, 2), lambda i, dst: (i, 0)),
            pl.BlockSpec(memory_space=pl.ANY),
        ],
        out_specs=pl.BlockSpec((tm, d), lambda i, dst: (i, 0)),
        scratch_shapes=[pltpu.VMEM((2, 2, tm, TOK_SUB, LANES), F32), pltpu.SemaphoreType.DMA((2,))],
    )
    return pl.pallas_call(
        functools.partial(_combine_body, n=n),
        grid_spec=grid_spec,
        out_shape=jax.ShapeDtypeStruct((n, d), F32),
        compiler_params=_cparams(("arbitrary",)),
    )(dest.reshape(-1), x1, gates_t, eo)


def _dispatch_plan(eid, rank, counts, n):
    nk = 2 * n
    p_rows = -(-nk // MOE_BLOCK) * MOE_BLOCK + N_EXPERTS * MOE_BLOCK
    nb = p_rows // MOE_BLOCK
    padded = -(-counts // MOE_BLOCK) * MOE_BLOCK
    pends = jnp.cumsum(padded)
    pstarts = pends - padded
    experts = jnp.arange(N_EXPERTS, dtype=I32)
    dest = jnp.sum(jnp.where(eid[..., None] == experts, pstarts.astype(I32), 0), axis=-1) + rank
    block_start = jnp.arange(nb, dtype=I32) * MOE_BLOCK
    block_expert = jnp.clip(jnp.sum((pends[None, :] <= block_start[:, None]).astype(I32), axis=-1),
                            0, N_EXPERTS - 1).astype(I32)
    n_used = (pends[-1] // MOE_BLOCK).astype(I32).reshape(1)
    zrow = jnp.concatenate([jnp.where(padded > 0, pends - MOE_BLOCK, -1).astype(I32), n_used])
    return dest.astype(I32), zrow, block_expert, n_used, p_rows


def _layer(x, ln1_g, w_in, q_g, k_g, rpb, attn_g, conv_w, a_log, dt_bias, dn_g, w_out, ln2_g,
           wg_r, bg_r, we_r, be_r, w_gate, w_up, w_down):
    b, s_len, d = x.shape
    n = b * s_len
    x2 = x.reshape(n, d)
    aqkv, dqkv, dz, dab = _inproj(x2, ln1_g, w_in)
    attn_raw = _attention(aqkv.reshape(b, s_len, -1), q_g, k_g, rpb)
    dn = _deltanet(dqkv.reshape(b, s_len, -1), dz.reshape(b, s_len, -1), dab.reshape(b, s_len, -1),
                   conv_w, a_log, dt_bias, dn_g)
    x1, h2t, eid, gates, rank, cnt = _outproj_router(
        x2, attn_raw.reshape(n, -1), dn.reshape(n, -1), attn_g, w_out, ln2_g, wg_r, bg_r, we_r, be_r)
    counts = cnt[:, 0].astype(I32)
    dest, zrow, block_expert, n_used, p_rows = _dispatch_plan(eid, rank, counts, n)
    xs = _dispatch(h2t, dest, zrow, p_rows)
    eo = _experts(xs, block_expert, n_used, w_gate, w_up, w_down)
    out = _combine(x1, gates.T, dest, eo)
    return out.reshape(b, s_len, d)


def kernel(x, ln1_g, w_in, attn_q_norm_g, attn_k_norm_g, attn_rpb, attn_out_norm_g, dn_conv_w, dn_a_log,
           dn_dt_bias, dn_out_norm_g, w_out, ln2_g, router_group_w, router_group_b, router_expert_w,
           router_expert_b, expert_w_gate, expert_w_up, expert_w_down):
    for l in range(ln1_g.shape[0]):
        x = _layer(x, ln1_g[l], w_in[l], attn_q_norm_g[l], attn_k_norm_g[l], attn_rpb[l],
                   attn_out_norm_g[l], dn_conv_w[l], dn_a_log[l], dn_dt_bias[l], dn_out_norm_g[l],
                   w_out[l], ln2_g[l], router_group_w[l], router_group_b[l], router_expert_w[l],
                   router_expert_b[l], expert_w_gate[l], expert_w_up[l], expert_w_down[l])
    return x
```

```python
import functools

import jax
import jax.numpy as jnp
from jax import lax
from jax.experimental import pallas as pl
from jax.experimental.pallas import tpu as pltpu

F32 = jnp.float32
BF16 = jnp.bfloat16
I32 = jnp.int32

EPS = 1e-6
GRID_W = 64
WIN_H = 8
WIN_W = 16
ATTN_HEADS = 8
ATTN_HD = 64
ATTN_WIDTH = ATTN_HEADS * ATTN_HD
DN_HEADS = 4
DN_HD = 128
DN_WIDTH = DN_HEADS * DN_HD
DN_CONV = 5
DN_CHUNK = 64
N_GROUPS = 8
EPG = 8
N_EXPERTS = N_GROUPS * EPG
MOE_BLOCK = 256
NEG = -1e30

ATTN_UNROLL = 8
LANES = 128
VMEM_LIMIT = 56 * 1024 * 1024


def _cparams(sem):
    return pltpu.CompilerParams(dimension_semantics=sem, vmem_limit_bytes=VMEM_LIMIT)


def _dot(a, b):
    return jnp.dot(a, b, preferred_element_type=F32)


def _dot_nt(a, b):
    return lax.dot_general(a, b, (((1,), (1,)), ((), ())), preferred_element_type=F32)


def _split_hi_lo(x):
    hi = x.astype(BF16)
    lo = (x - hi.astype(F32)).astype(BF16)
    return hi, lo


def _inproj_body(x_ref, g_ref, wa_ref, wd_ref, wz_ref, wab_ref, oa_ref, od_ref, oz_ref, oab_ref):
    x = x_ref[...]
    ms = jnp.mean(x * x, axis=-1, keepdims=True)
    h = x * lax.rsqrt(ms + EPS) * g_ref[...]
    hi, lo = _split_hi_lo(h)
    oa_ref[...] = _dot(hi, wa_ref[...]).astype(BF16)
    od_ref[...] = _dot(hi, wd_ref[...]).astype(BF16)
    oz_ref[...] = _dot(hi, wz_ref[...]).astype(BF16)
    nab = oab_ref.shape[-1]
    ab = _dot(hi, wab_ref[...])
    ab_lo = _dot(lo, wab_ref[:, :nab])
    oab_ref[...] = ab[:, :nab] + ab[:, nab:] + ab_lo


def _inproj(x2, ln1_g, w_in, tm=512):
    n, d = x2.shape
    a3 = 3 * ATTN_WIDTH
    d3 = 3 * DN_WIDTH
    wa = w_in[:, :a3].astype(BF16)
    wd = w_in[:, a3:a3 + d3].astype(BF16)
    wz = w_in[:, a3 + d3:a3 + d3 + DN_WIDTH].astype(BF16)
    wab_f = w_in[:, a3 + d3 + DN_WIDTH:]
    nab = wab_f.shape[1]
    wab_hi, wab_lo = _split_hi_lo(wab_f)
    wab = jnp.concatenate([wab_hi, wab_lo], axis=1)
    const = lambda i: (0, 0)
    row = lambda i: (i, 0)
    return pl.pallas_call(
        _inproj_body,
        grid=(n // tm,),
        in_specs=[
            pl.BlockSpec((tm, d), row),
            pl.BlockSpec((1, d), const),
            pl.BlockSpec((d, a3), const),
            pl.BlockSpec((d, d3), const),
            pl.BlockSpec((d, DN_WIDTH), const),
            pl.BlockSpec((d, 2 * nab), const),
        ],
        out_specs=[
            pl.BlockSpec((tm, a3), row),
            pl.BlockSpec((tm, d3), row),
            pl.BlockSpec((tm, DN_WIDTH), row),
            pl.BlockSpec((tm, nab), row),
        ],
        out_shape=[
            jax.ShapeDtypeStruct((n, a3), BF16),
            jax.ShapeDtypeStruct((n, d3), BF16),
            jax.ShapeDtypeStruct((n, DN_WIDTH), BF16),
            jax.ShapeDtypeStruct((n, nab), F32),
        ],
        compiler_params=_cparams(("parallel",)),
    )(x2, ln1_g.reshape(1, d), wa, wd, wz, wab)


def _attn_bias_table(rpb):
    c = jnp.arange(GRID_W)
    c0 = jnp.clip(c - WIN_W // 2, 0, GRID_W - WIN_W)
    kc = jnp.arange(GRID_W)
    inwin = (kc[None, :] >= c0[:, None]) & (kc[None, :] < c0[:, None] + WIN_W)
    h = rpb.shape[0]
    padw = GRID_W - WIN_W
    rp = jnp.pad(rpb.astype(F32), ((0, 0), (0, 0), (padw, padw)))
    by_col = jnp.stack([rp[:, :, GRID_W - 1 - cc:2 * GRID_W - 1 - cc] for cc in range(GRID_W)], axis=2)
    tab = jnp.stack([by_col[:, WIN_H - 1 - dd:2 * WIN_H - 1 - dd] for dd in range(WIN_H)], axis=1)
    tab = tab.transpose(0, 1, 3, 2, 4)
    tab = jnp.where(inwin[None, None, :, None, :], tab, NEG)
    tab = tab.reshape(h // 2, 2, WIN_H, GRID_W, WIN_H * GRID_W)
    return tab.transpose(0, 2, 1, 3, 4).reshape(h // 2, WIN_H, 2 * GRID_W, WIN_H * GRID_W)


def _attn_body(q_ref, k_ref, v_ref, qg_ref, kg_ref, bias_ref, o_ref, qlo_s, qhi_s, kn_s, *, rows):
    s_len = rows * GRID_W
    lane = lax.broadcasted_iota(I32, (1, LANES), 1)
    is_lo = lane < ATTN_HD
    blk = 256

    hr = lax.broadcasted_iota(I32, (LANES, LANES), 0) // ATTN_HD
    hc = lax.broadcasted_iota(I32, (LANES, LANES), 1) // ATTN_HD
    same_head = jnp.where(hr == hc, 1.0, 0.0).astype(BF16)

    def head_norm(x, gain):
        ms = _dot((x * x).astype(BF16), same_head) * (1.0 / ATTN_HD)
        return x * lax.rsqrt(ms + EPS) * gain

    def prep(i, c):
        sl = pl.ds(pl.multiple_of(i * blk, blk), blk)
        qn = head_norm(q_ref[0, sl, :].astype(F32), qg_ref[...]) * (ATTN_HD ** -0.5)
        qlo_s[sl, :] = jnp.where(is_lo, qn, 0.0).astype(BF16)
        qhi_s[sl, :] = jnp.where(is_lo, 0.0, qn).astype(BF16)
        kn_s[sl, :] = head_norm(k_ref[0, sl, :].astype(F32), kg_ref[...]).astype(BF16)
        return c

    lax.fori_loop(0, s_len // blk, prep, 0)

    band = WIN_H * GRID_W

    def row_step(i, c):
        ctx = []
        for u in range(ATTN_UNROLL):
            r = i * ATTN_UNROLL + u
            r0 = jnp.clip(r - WIN_H // 2, 0, rows - WIN_H)
            qsl = pl.ds(pl.multiple_of(r * GRID_W, GRID_W), GRID_W)
            ksl = pl.ds(pl.multiple_of(r0 * GRID_W, GRID_W), band)
            q2 = jnp.concatenate([qlo_s[qsl, :], qhi_s[qsl, :]], axis=0)
            ctx.append(dict(d=r - r0, qsl=qsl, ksl=ksl, s=_dot_nt(q2, kn_s[ksl, :])))
        for c_ in ctx:
            s = c_['s'] + bias_ref[0, c_['d']]
            m = jnp.max(s, axis=-1, keepdims=True)
            p = jnp.exp(s - m)
            c_['l'] = jnp.sum(p, axis=-1, keepdims=True)
            c_['p'] = p.astype(BF16)
        for c_ in ctx:
            c_['pv'] = _dot(c_['p'], v_ref[0, c_['ksl'], :])
        for c_ in ctx:
            pv = c_['pv'] / c_['l']
            o = jnp.where(is_lo, pv[:GRID_W], pv[GRID_W:])
            o_ref[0, c_['qsl'], :] = o.astype(o_ref.dtype)
        return c

    lax.fori_loop(0, rows // ATTN_UNROLL, row_step, 0)


def _attention(aqkv, q_g, k_g, rpb):
    b, s_len, _ = aqkv.shape
    rows = s_len // GRID_W
    assert min(WIN_H, rows) == WIN_H
    pairs = ATTN_HEADS // 2
    bias = _attn_bias_table(rpb)
    qg2 = jnp.tile(q_g, 2).reshape(1, LANES).astype(F32)
    kg2 = jnp.tile(k_g, 2).reshape(1, LANES).astype(F32)
    band = WIN_H * GRID_W
    return pl.pallas_call(
        functools.partial(_attn_body, rows=rows),
        grid=(pairs, b),
        in_specs=[
            pl.BlockSpec((1, s_len, LANES), lambda p, i: (i, 0, p)),
            pl.BlockSpec((1, s_len, LANES), lambda p, i: (i, 0, pairs + p)),
            pl.BlockSpec((1, s_len, LANES), lambda p, i: (i, 0, 2 * pairs + p)),
            pl.BlockSpec((1, LANES), lambda p, i: (0, 0)),
            pl.BlockSpec((1, LANES), lambda p, i: (0, 0)),
            pl.BlockSpec((1, WIN_H, 2 * GRID_W, band), lambda p, i: (p, 0, 0, 0)),
        ],
        out_specs=pl.BlockSpec((1, s_len, LANES), lambda p, i: (i, 0, p)),
        out_shape=jax.ShapeDtypeStruct((b, s_len, ATTN_WIDTH), BF16),
        scratch_shapes=[
            pltpu.VMEM((s_len, LANES), BF16),
            pltpu.VMEM((s_len, LANES), BF16),
            pltpu.VMEM((s_len, LANES), BF16),
        ],
        compiler_params=_cparams(("arbitrary", "arbitrary")),
    )(aqkv, aqkv, aqkv, qg2, kg2, bias)


SUPER = 2 * DN_CHUNK
HALO = 16
DN_UNROLL = 8


def _dn_body(q_ref, k_ref, v_ref, z_ref, ab_ref, cwq_ref, cwk_ref, cwv_ref, alog_ref, dtb_ref, og_ref,
             o_ref, pad_s, qs, ks, vs, of_s, ob_s, *, s_len):
    head = pl.program_id(1)
    n_super = s_len // SUPER

    zeros_halo = jnp.zeros((HALO, LANES), F32)
    for t, src in enumerate((q_ref, k_ref, v_ref)):
        pad_s[t, 0:HALO, :] = zeros_halo
        pad_s[t, HALO + s_len:2 * HALO + s_len, :] = zeros_halo

        def stage(i, c, t=t, src=src):
            off = pl.multiple_of(i * 512, 512)
            pad_s[t, pl.ds(HALO + off, 512), :] = src[0, pl.ds(off, 512), :].astype(F32)
            return c

        lax.fori_loop(0, s_len // 512, stage, 0)


    def conv_step(i, c):
        t0 = pl.multiple_of(i * SUPER, SUPER)
        outs = []
        for t, cw_ref in enumerate((cwq_ref, cwk_ref, cwv_ref)):
            acc = jnp.zeros((SUPER, LANES), F32)
            for j in range(DN_CONV):
                xj = pad_s[t, pl.ds(t0 + (HALO - DN_CONV // 2 + j), SUPER), :]
                acc = acc + xj * cw_ref[j:j + 1, :]
            outs.append(acc * jax.nn.sigmoid(acc))
        qc, kc, vc = outs
        sl = pl.ds(t0, SUPER)
        qs[sl, :] = qc * lax.rsqrt(jnp.sum(qc * qc, axis=-1, keepdims=True) + EPS) * (DN_HD ** -0.5)
        ks[sl, :] = kc * lax.rsqrt(jnp.sum(kc * kc, axis=-1, keepdims=True) + EPS)
        vs[sl, :] = vc
        return c

    lax.fori_loop(0, n_super, conv_step, 0)

    ri = lax.broadcasted_iota(I32, (SUPER, SUPER), 0)
    ci = lax.broadcasted_iota(I32, (SUPER, SUPER), 1)
    same = (ri // DN_CHUNK) == (ci // DN_CHUNK)
    eye = (ri == ci).astype(F32)
    posl = lax.broadcasted_iota(I32, (8, LANES), 1) % DN_CHUNK

    def prep(sc, rev):
        t0 = pl.multiple_of(sc * SUPER, SUPER)
        sl = pl.ds(t0, SUPER)
        q = qs[sl, :]
        k = ks[sl, :]
        v = vs[sl, :]
        dcol = 1 if rev else 0
        a_r = jnp.broadcast_to(ab_ref[0, 0, dcol:dcol + 1, sl], (8, LANES))
        b_r = jnp.broadcast_to(ab_ref[0, 0, 2 + dcol:3 + dcol, sl], (8, LANES))
        neg_rate = -jnp.exp(jnp.full((1, LANES), alog_ref[dcol, head], F32))
        g_r = neg_rate * jax.nn.softplus(a_r + dtb_ref[dcol, head])
        beta_r = jax.nn.sigmoid(b_r)
        gc_r = g_r
        for sft in (1, 2, 4, 8, 16, 32):
            if rev:
                gc_r = gc_r + jnp.where(posl < DN_CHUNK - sft, pltpu.roll(gc_r, SUPER - sft, axis=1), 0.0)
            else:
                gc_r = gc_r + jnp.where(posl >= sft, pltpu.roll(gc_r, sft, axis=1), 0.0)
        gct = jnp.broadcast_to(gc_r[0:1], (SUPER, LANES))
        gc = gct.T
        beta = jnp.broadcast_to(beta_r[0:1], (SUPER, LANES)).T
        diff = gc - gct
        if rev:
            incl = same & (ri <= ci)
            strict = same & (ri < ci)
        else:
            incl = same & (ri >= ci)
            strict = same & (ri > ci)
        decay = jnp.where(incl, jnp.exp(jnp.where(incl, diff, 0.0)), 0.0)
        kb = k * beta
        gram = _dot_nt(jnp.concatenate([q, kb], axis=0).astype(BF16), k.astype(BF16))
        qkd = (gram[:SUPER] * decay).astype(BF16)
        low = jnp.where(strict, gram[SUPER:] * decay, 0.0)
        egc = jnp.exp(gc)
        rhs = jnp.concatenate([v * beta, kb * egc], axis=1).astype(BF16)
        return dict(sl=sl, rev=rev, gc=gc, gct=gct, qkd=qkd, rhs=rhs, qe=q * egc, kt=k.T,
                    inv=eye - low, pw=low.astype(BF16))

    def phase1(chains):
        ctx = [prep(sc, rev) for sc, rev in chains]
        for _ in range(5):
            for c in ctx:
                c['pw'] = _dot(c['pw'], c['pw']).astype(BF16)
            for c in ctx:
                c['inv'] = c['inv'] + _dot(c['inv'].astype(BF16), c['pw'])
        for c in ctx:
            c['sol'] = _dot(c['inv'].astype(BF16), c['rhs']).astype(BF16)
        for c in ctx:
            qo = _dot(c['qkd'], c['sol'])
            c['o0'] = qo[:, :LANES]
            c['qt'] = c['qe'] - qo[:, LANES:]
        out = []
        for c in ctx:
            rev, gc, gct = c['rev'], c['gc'], c['gct']
            per_chunk = []
            for cidx in ((1, 0) if rev else (0, 1)):
                lo = cidx * DN_CHUNK
                last = lo if rev else lo + DN_CHUNK - 1
                g_last = gc[last:last + 1, :]
                in_c = (ci // DN_CHUNK) == cidx
                kst = c['kt'] * jnp.where(in_c, jnp.exp(jnp.where(in_c, g_last - gct, 0.0)), 0.0)
                mn = _dot(kst.astype(BF16), c['sol'])
                lhs = jnp.concatenate([-mn[:, LANES:], c['qt'][lo:lo + DN_CHUNK]], axis=0).astype(BF16)
                per_chunk.append((cidx, lhs, mn[:, :LANES], jnp.exp(g_last), c['o0'][lo:lo + DN_CHUNK]))
            out.append((c['sl'], per_chunk))
        return out

    def scan(state, p1, rev):
        sl, per_chunk = p1
        outs = [None, None]
        for cidx, lhs, ku, e_last, o0_c in per_chunk:
            r = _dot(lhs, state.astype(BF16))
            outs[cidx] = r[LANES:] + o0_c
            state = e_last * state + r[:LANES] + ku
        dst = ob_s if rev else of_s
        dst[sl, :] = jnp.concatenate(outs, axis=0)
        return state

    def scan_step(i, carry):
        s_f, s_b = carry
        chains = []
        for u in range(DN_UNROLL):
            chains += [(i * DN_UNROLL + u, False), (n_super - 1 - (i * DN_UNROLL + u), True)]
        p1 = phase1(chains)
        p_f, p_b = p1[0::2], p1[1::2]
        for u in range(DN_UNROLL):
            s_f = scan(s_f, p_f[u], False)
            s_b = scan(s_b, p_b[u], True)
        return s_f, s_b

    z0 = jnp.zeros((DN_HD, DN_HD), F32)
    lax.fori_loop(0, n_super // DN_UNROLL, scan_step, (z0, z0))

    def fin(i, c):
        sl = pl.ds(pl.multiple_of(i * 512, 512), 512)
        o = of_s[sl, :] + ob_s[sl, :]
        o = o * lax.rsqrt(jnp.mean(o * o, axis=-1, keepdims=True) + EPS) * og_ref[...]
        zf = z_ref[0, sl, :].astype(F32)
        o_ref[0, sl, :] = (o * (zf * jax.nn.sigmoid(zf))).astype(o_ref.dtype)
        return c

    lax.fori_loop(0, s_len // 512, fin, 0)


def _deltanet(dqkv, dz, dab, conv_w, a_log, dt_bias, out_g):
    b, s_len, _ = dqkv.shape
    h = DN_HEADS
    ab = dab.reshape(b, s_len, 2, 2, h).transpose(0, 4, 2, 3, 1).reshape(b, h, 4, s_len)
    cw = conv_w.astype(F32)
    tok = lambda off: (lambda i, j: (i, 0, off + j))
    smem = pl.BlockSpec(memory_space=pltpu.SMEM)
    return pl.pallas_call(
        functools.partial(_dn_body, s_len=s_len),
        grid=(b, h),
        in_specs=[
            pl.BlockSpec((1, s_len, LANES), tok(0)),
            pl.BlockSpec((1, s_len, LANES), tok(h)),
            pl.BlockSpec((1, s_len, LANES), tok(2 * h)),
            pl.BlockSpec((1, s_len, LANES), tok(0)),
            pl.BlockSpec((1, 1, 4, s_len), lambda i, j: (i, j, 0, 0)),
            pl.BlockSpec((DN_CONV, LANES), lambda i, j: (0, j)),
            pl.BlockSpec((DN_CONV, LANES), lambda i, j: (0, h + j)),
            pl.BlockSpec((DN_CONV, LANES), lambda i, j: (0, 2 * h + j)),
            smem,
            smem,
            pl.BlockSpec((1, LANES), lambda i, j: (0, 0)),
        ],
        out_specs=pl.BlockSpec((1, s_len, LANES), tok(0)),
        out_shape=jax.ShapeDtypeStruct((b, s_len, DN_WIDTH), BF16),
        scratch_shapes=[
            pltpu.VMEM((3, s_len + 2 * HALO, LANES), F32),
            pltpu.VMEM((s_len, LANES), F32),
            pltpu.VMEM((s_len, LANES), F32),
            pltpu.VMEM((s_len, LANES), F32),
            pltpu.VMEM((s_len, LANES), F32),
            pltpu.VMEM((s_len, LANES), F32),
        ],
        compiler_params=_cparams(("arbitrary", "arbitrary")),
    )(dqkv, dqkv, dqkv, dz, ab, cw, cw, cw, a_log.astype(F32), dt_bias.astype(F32),
      out_g.reshape(1, LANES).astype(F32))


ROUTER_ROWS = 80
TOK_SUB = 8


def _to_token_tiles(x):
    blocks = jnp.stack([x[:, s * LANES:(s + 1) * LANES] for s in range(TOK_SUB)], axis=0)
    return pltpu.einshape("spl->psl", blocks)


def _from_token_tiles(t):
    blocks = pltpu.einshape("psl->spl", t)
    return jnp.concatenate([blocks[s] for s in range(TOK_SUB)], axis=1)


def _outproj_body(x_ref, at_ref, dn_ref, ag_ref, wo_ref, lg_ref, wrh_ref, wrl_ref, rb_ref, su_ref,
                  x1_ref, h2_ref, eid_ref, gate_ref, rank_ref, cnt_ref, base_s):
    step = pl.program_id(0)
    tm = x_ref.shape[0]

    @pl.when(step == 0)
    def _():
        base_s[...] = jnp.zeros_like(base_s)

    a = at_ref[...].astype(F32)
    an = a * lax.rsqrt(jnp.mean(a * a, axis=-1, keepdims=True) + EPS) * ag_ref[...]
    mix = jnp.concatenate([an.astype(BF16), dn_ref[...]], axis=1)
    x1 = x_ref[...] + _dot(mix, wo_ref[...])
    x1_ref[...] = x1
    h2 = x1 * lax.rsqrt(jnp.mean(x1 * x1, axis=-1, keepdims=True) + EPS) * lg_ref[...]
    h2_ref[...] = _to_token_tiles(h2)

    hi, lo = _split_hi_lo(h2)
    wrh = wrh_ref[...]
    lt = _dot_nt(wrh, hi) + _dot_nt(wrh, lo) + _dot_nt(wrl_ref[...], hi) + rb_ref[...]

    sub8 = lax.broadcasted_iota(I32, (N_GROUPS, tm), 0)
    gl = lt[0:N_GROUPS]
    gmax = jnp.max(gl, axis=0, keepdims=True)
    gsel = jnp.min(jnp.where(gl == gmax, sub8, N_GROUPS), axis=0, keepdims=True)
    gp = 1.0 / jnp.sum(jnp.exp(gl - gmax), axis=0, keepdims=True)
    in_group = jnp.zeros((EPG, tm), F32)
    for g in range(N_GROUPS):
        in_group = in_group + jnp.where(gsel == g, lt[N_GROUPS + g * EPG:N_GROUPS + (g + 1) * EPG], 0.0)
    v1 = jnp.max(in_group, axis=0, keepdims=True)
    i1 = jnp.min(jnp.where(in_group == v1, sub8, EPG), axis=0, keepdims=True)
    rest = jnp.where(sub8 == i1, -jnp.inf, in_group)
    v2 = jnp.max(rest, axis=0, keepdims=True)
    i2 = jnp.min(jnp.where(rest == v2, sub8, EPG), axis=0, keepdims=True)
    e21 = jnp.exp(v2 - v1)
    den = 1.0 + e21
    eid1 = gsel * EPG + i1
    eid2 = gsel * EPG + i2
    eid_ref[0:1, :] = eid1
    eid_ref[1:2, :] = eid2
    gate_ref[0:1, :] = gp * (1.0 / den)
    gate_ref[1:2, :] = gp * (e21 / den)

    sub64 = lax.broadcasted_iota(I32, (N_EXPERTS, tm), 0)
    oh1 = sub64 == eid1
    oh2 = sub64 == eid2
    ohs = jnp.where(oh1 | oh2, 1.0, 0.0)
    before = base_s[...] + _dot(ohs.astype(BF16), su_ref[...])
    rank_ref[0:1, :] = jnp.sum(jnp.where(oh1, before, 0.0), axis=0, keepdims=True).astype(I32)
    rank_ref[1:2, :] = jnp.sum(jnp.where(oh2, before, 0.0), axis=0, keepdims=True).astype(I32)
    base_new = base_s[...] + jnp.sum(ohs, axis=1, keepdims=True)
    base_s[...] = base_new
    cnt_ref[...] = jnp.broadcast_to(base_new, cnt_ref.shape)


def _outproj_router(x2, attn_raw, dn, attn_g, w_out, ln2_g, wg_r, bg_r, we_r, be_r, tm=512):
    n, d = x2.shape
    wr = jnp.concatenate([wg_r, we_r], axis=1).T.astype(F32)
    wr = jnp.pad(wr, ((0, ROUTER_ROWS - wr.shape[0]), (0, 0)))
    wrh, wrl = _split_hi_lo(wr)
    rb = jnp.pad(jnp.concatenate([bg_r, be_r]).astype(F32), (0, ROUTER_ROWS - N_GROUPS - N_EXPERTS))
    rb = rb.reshape(ROUTER_ROWS, 1)
    t_i = jnp.arange(tm)
    su = (t_i[:, None] < t_i[None, :]).astype(BF16)
    const = lambda i: (0, 0)
    row = lambda i: (i, 0)
    col = lambda i: (0, i)
    return pl.pallas_call(
        _outproj_body,
        grid=(n // tm,),
        in_specs=[
            pl.BlockSpec((tm, d), row),
            pl.BlockSpec((tm, ATTN_WIDTH), row),
            pl.BlockSpec((tm, DN_WIDTH), row),
            pl.BlockSpec((1, ATTN_WIDTH), const),
            pl.BlockSpec((ATTN_WIDTH + DN_WIDTH, d), const),
            pl.BlockSpec((1, d), const),
            pl.BlockSpec((ROUTER_ROWS, d), const),
            pl.BlockSpec((ROUTER_ROWS, d), const),
            pl.BlockSpec((ROUTER_ROWS, 1), const),
            pl.BlockSpec((tm, tm), const),
        ],
        out_specs=[
            pl.BlockSpec((tm, d), row),
            pl.BlockSpec((tm, TOK_SUB, LANES), lambda i: (i, 0, 0)),
            pl.BlockSpec((2, tm), col),
            pl.BlockSpec((2, tm), col),
            pl.BlockSpec((2, tm), col),
            pl.BlockSpec((N_EXPERTS, LANES), const),
        ],
        out_shape=[
            jax.ShapeDtypeStruct((n, d), F32),
            jax.ShapeDtypeStruct((n, TOK_SUB, LANES), F32),
            jax.ShapeDtypeStruct((2, n), I32),
            jax.ShapeDtypeStruct((2, n), F32),
            jax.ShapeDtypeStruct((2, n), I32),
            jax.ShapeDtypeStruct((N_EXPERTS, LANES), F32),
        ],
        scratch_shapes=[pltpu.VMEM((N_EXPERTS, 1), F32)],
        compiler_params=_cparams(("arbitrary",)),
    )(x2, attn_raw, dn, attn_g.reshape(1, -1).astype(F32), w_out.astype(BF16), ln2_g.reshape(1, d).astype(F32),
      wrh, wrl, rb, su)


def _dispatch_body(dest_ref, zrow_ref, h_ref, xs_hbm, zbuf, zsem, sem, *, n):
    tm = h_ref.shape[0]
    base = pl.program_id(0) * tm

    @pl.when(pl.program_id(0) == 0)
    def _():
        zbuf[...] = jnp.zeros_like(zbuf)

        def zero_copy(e):
            return pltpu.make_async_copy(zbuf, xs_hbm.at[pl.ds(zrow_ref[e], MOE_BLOCK)], zsem)

        def start(e, c):
            @pl.when(zrow_ref[e] >= 0)
            def _():
                zero_copy(e).start()
            return c

        def finish(e, c):
            @pl.when(zrow_ref[e] >= 0)
            def _():
                zero_copy(e).wait()
            return c

        lax.fori_loop(0, N_EXPERTS, start, 0)
        lax.fori_loop(0, N_EXPERTS, finish, 0)

        def tail_copy(b):
            return pltpu.make_async_copy(zbuf, xs_hbm.at[pl.ds(b * MOE_BLOCK, MOE_BLOCK)], zsem)

        def tail_start(b, c):
            tail_copy(b).start()
            return c

        def tail_finish(b, c):
            tail_copy(b).wait()
            return c

        n_blocks = xs_hbm.shape[0] // MOE_BLOCK
        lax.fori_loop(zrow_ref[N_EXPERTS], n_blocks, tail_start, 0)
        lax.fori_loop(zrow_ref[N_EXPERTS], n_blocks, tail_finish, 0)

    def issue(j, c):
        src = h_ref.at[pl.ds(j, 1)]
        pltpu.make_async_copy(src, xs_hbm.at[pl.ds(dest_ref[base + j], 1)], sem).start()
        pltpu.make_async_copy(src, xs_hbm.at[pl.ds(dest_ref[n + base + j], 1)], sem).start()
        return c

    lax.fori_loop(0, tm, issue, 0, unroll=8)
    for _ in range(2):
        pltpu.make_async_copy(h_ref, xs_hbm.at[pl.ds(0, tm)], sem).wait()


def _dispatch(h2t, dest, zrow, p_rows, tm=512):
    n = h2t.shape[0]
    grid_spec = pltpu.PrefetchScalarGridSpec(
        num_scalar_prefetch=2,
        grid=(n // tm,),
        in_specs=[pl.BlockSpec((tm, TOK_SUB, LANES), lambda i, dst, zr: (i, 0, 0))],
        out_specs=pl.BlockSpec(memory_space=pl.ANY),
        scratch_shapes=[pltpu.VMEM((MOE_BLOCK, TOK_SUB, LANES), F32), pltpu.SemaphoreType.DMA(()),
                        pltpu.SemaphoreType.DMA(())],
    )
    return pl.pallas_call(
        functools.partial(_dispatch_body, n=n),
        grid_spec=grid_spec,
        out_shape=jax.ShapeDtypeStruct((p_rows, TOK_SUB, LANES), F32),
        compiler_params=_cparams(("arbitrary",)),
    )(dest.reshape(-1), zrow, h2t)


def _expert_body(be_ref, nu_ref, xs_ref, wg_ref, wu_ref, wd_ref, o_ref):
    used = pl.program_id(0) < nu_ref[0]

    @pl.when(used)
    def _():
        x = _from_token_tiles(xs_ref[...]).astype(BF16)
        gt = _dot(x, wg_ref[0])
        up = _dot(x, wu_ref[0])
        hid = gt * jax.nn.sigmoid(gt) * up
        out = _dot(hid.astype(BF16), wd_ref[0])
        o_ref[...] = _to_token_tiles(out)

    @pl.when(jnp.logical_not(used))
    def _():
        o_ref[...] = jnp.zeros_like(o_ref)


def _experts(xs, block_expert, n_used, w_gate, w_up, w_down):
    p_rows = xs.shape[0]
    nb = p_rows // MOE_BLOCK
    d, de = w_gate.shape[-2:]
    wmap = lambda i, be, nu: (be[i], 0, 0)
    xmap = lambda i, be, nu: (jnp.minimum(i, nu[0] - 1), 0, 0)
    grid_spec = pltpu.PrefetchScalarGridSpec(
        num_scalar_prefetch=2,
        grid=(nb,),
        in_specs=[
            pl.BlockSpec((MOE_BLOCK, TOK_SUB, LANES), xmap),
            pl.BlockSpec((1, d, de), wmap),
            pl.BlockSpec((1, d, de), wmap),
            pl.BlockSpec((1, de, d), wmap),
        ],
        out_specs=pl.BlockSpec((MOE_BLOCK, TOK_SUB, LANES), lambda i, be, nu: (i, 0, 0)),
    )
    return pl.pallas_call(
        _expert_body,
        grid_spec=grid_spec,
        out_shape=jax.ShapeDtypeStruct((p_rows, TOK_SUB, LANES), F32),
        compiler_params=_cparams(("arbitrary",)),
    )(block_expert, n_used, xs, w_gate.astype(BF16), w_up.astype(BF16), w_down.astype(BF16))


def _combine_body(dest_ref, x1_ref, gt_ref, eo_hbm, out_ref, rbuf, sem, *, n):
    tm = x1_ref.shape[0]
    step = pl.program_id(0)
    slot = step % 2

    def gather(tile, slot_):
        base = tile * tm

        def issue(j, c):
            d0 = dest_ref[base + j]
            d1 = dest_ref[n + base + j]
            pltpu.make_async_copy(eo_hbm.at[pl.ds(d0, 1)], rbuf.at[slot_, 0, pl.ds(j, 1)], sem.at[slot_]).start()
            pltpu.make_async_copy(eo_hbm.at[pl.ds(d1, 1)], rbuf.at[slot_, 1, pl.ds(j, 1)], sem.at[slot_]).start()
            return c

        lax.fori_loop(0, tm, issue, 0, unroll=8)

    @pl.when(step == 0)
    def _():
        gather(0, 0)

    @pl.when(step + 1 < pl.num_programs(0))
    def _():
        gather(step + 1, 1 - slot)

    for k in range(2):
        pltpu.make_async_copy(eo_hbm.at[pl.ds(0, tm)], rbuf.at[slot, k], sem.at[slot]).wait()
    g0 = jnp.broadcast_to(gt_ref[:, 0:1], (tm, LANES))
    g1 = jnp.broadcast_to(gt_ref[:, 1:2], (tm, LANES))
    r0 = pltpu.einshape("psl->spl", rbuf[slot, 0])
    r1 = pltpu.einshape("psl->spl", rbuf[slot, 1])
    for s in range(TOK_SUB):
        cols = slice(s * LANES, (s + 1) * LANES)
        out_ref[:, cols] = x1_ref[:, cols] + (g0 * r0[s] + g1 * r1[s])


def _combine(x1, gates_t, dest, eo, tm=256):
    n, d = x1.shape
    grid_spec = pltpu.PrefetchScalarGridSpec(
        num_scalar_prefetch=1,
        grid=(n // tm,),
        in_specs=[
            pl.BlockSpec((tm, d), lambda i, dst: (i, 0)),
            pl.BlockSpec((tm, 2), lambda i, dst: (i, 0)),
            pl.BlockSpec(memory_space=pl.ANY),
        ],
        out_specs=pl.BlockSpec((tm, d), lambda i, dst: (i, 0)),
        scratch_shapes=[pltpu.VMEM((2, 2, tm, TOK_SUB, LANES), F32), pltpu.SemaphoreType.DMA((2,))],
    )
    return pl.pallas_call(
        functools.partial(_combine_body, n=n),
        grid_spec=grid_spec,
        out_shape=jax.ShapeDtypeStruct((n, d), F32),
        compiler_params=_cparams(("arbitrary",)),
    )(dest.reshape(-1), x1, gates_t, eo)


def _dispatch_plan(eid, rank, counts, n):
    nk = 2 * n
    p_rows = -(-nk // MOE_BLOCK) * MOE_BLOCK + N_EXPERTS * MOE_BLOCK
    nb = p_rows // MOE_BLOCK
    padded = -(-counts // MOE_BLOCK) * MOE_BLOCK
    pends = jnp.cumsum(padded)
    pstarts = pends - padded
    experts = jnp.arange(N_EXPERTS, dtype=I32)
    dest = jnp.sum(jnp.where(eid[..., None] == experts, pstarts.astype(I32), 0), axis=-1) + rank
    block_start = jnp.arange(nb, dtype=I32) * MOE_BLOCK
    block_expert = jnp.clip(jnp.sum((pends[None, :] <= block_start[:, None]).astype(I32), axis=-1),
                            0, N_EXPERTS - 1).astype(I32)
    n_used = (pends[-1] // MOE_BLOCK).astype(I32).reshape(1)
    zrow = jnp.concatenate([jnp.where(padded > 0, pends - MOE_BLOCK, -1).astype(I32), n_used])
    return dest.astype(I32), zrow, block_expert, n_used, p_rows


def _layer(x, ln1_g, w_in, q_g, k_g, rpb, attn_g, conv_w, a_log, dt_bias, dn_g, w_out, ln2_g,
           wg_r, bg_r, we_r, be_r, w_gate, w_up, w_down):
    b, s_len, d = x.shape
    n = b * s_len
    x2 = x.reshape(n, d)
    aqkv, dqkv, dz, dab = _inproj(x2, ln1_g, w_in)
    attn_raw = _attention(aqkv.reshape(b, s_len, -1), q_g, k_g, rpb)
    dn = _deltanet(dqkv.reshape(b, s_len, -1), dz.reshape(b, s_len, -1), dab.reshape(b, s_len, -1),
                   conv_w, a_log, dt_bias, dn_g)
    x1, h2t, eid, gates, rank, cnt = _outproj_router(
        x2, attn_raw.reshape(n, -1), dn.reshape(n, -1), attn_g, w_out, ln2_g, wg_r, bg_r, we_r, be_r)
    counts = cnt[:, 0].astype(I32)
    dest, zrow, block_expert, n_used, p_rows = _dispatch_plan(eid, rank, counts, n)
    xs = _dispatch(h2t, dest, zrow, p_rows)
    eo = _experts(xs, block_expert, n_used, w_gate, w_up, w_down)
    out = _combine(x1, gates.T, dest, eo)
    return out.reshape(b, s_len, d)


def kernel(x, ln1_g, w_in, attn_q_norm_g, attn_k_norm_g, attn_rpb, attn_out_norm_g, dn_conv_w, dn_a_log,
           dn_dt_bias, dn_out_norm_g, w_out, ln2_g, router_group_w, router_group_b, router_expert_w,
           router_expert_b, expert_w_gate, expert_w_up, expert_w_down):
    for l in range(ln1_g.shape[0]):
        x = _layer(x, ln1_g[l], w_in[l], attn_q_norm_g[l], attn_k_norm_g[l], attn_rpb[l],
                   attn_out_norm_g[l], dn_conv_w[l], dn_a_log[l], dn_dt_bias[l], dn_out_norm_g[l],
                   w_out[l], ln2_g[l], router_group_w[l], router_group_b[l], router_expert_w[l],
                   router_expert_b[l], expert_w_gate[l], expert_w_up[l], expert_w_down[l])
    return x
```

```python
import functools

import jax
import jax.numpy as jnp
from jax import lax
from jax.experimental import pallas as pl
from jax.experimental.pallas import tpu as pltpu

F32 = jnp.float32
BF16 = jnp.bfloat16
I32 = jnp.int32

EPS = 1e-6
GRID_W = 64
WIN_H = 8
WIN_W = 16
ATTN_HEADS = 8
ATTN_HD = 64
ATTN_WIDTH = ATTN_HEADS * ATTN_HD
DN_HEADS = 4
DN_HD = 128
DN_WIDTH = DN_HEADS * DN_HD
DN_CONV = 5
DN_CHUNK = 64
N_GROUPS = 8
EPG = 8
N_EXPERTS = N_GROUPS * EPG
MOE_BLOCK = 256
NEG = -1e30

ATTN_UNROLL = 8
LANES = 128
VMEM_LIMIT = 56 * 1024 * 1024


def _cparams(sem):
    return pltpu.CompilerParams(dimension_semantics=sem, vmem_limit_bytes=VMEM_LIMIT)


def _dot(a, b):
    return jnp.dot(a, b, preferred_element_type=F32)


def _dot_nt(a, b):
    return lax.dot_general(a, b, (((1,), (1,)), ((), ())), preferred_element_type=F32)


def _split_hi_lo(x):
    hi = x.astype(BF16)
    lo = (x - hi.astype(F32)).astype(BF16)
    return hi, lo


def _inproj_body(x_ref, g_ref, wa_ref, wd_ref, wz_ref, wab_ref, oa_ref, od_ref, oz_ref, oab_ref):
    x = x_ref[...]
    ms = jnp.mean(x * x, axis=-1, keepdims=True)
    h = x * lax.rsqrt(ms + EPS) * g_ref[...]
    hi, lo = _split_hi_lo(h)
    oa_ref[...] = _dot(hi, wa_ref[...]).astype(BF16)
    od_ref[...] = _dot(hi, wd_ref[...]).astype(BF16)
    oz_ref[...] = _dot(hi, wz_ref[...]).astype(BF16)
    nab = oab_ref.shape[-1]
    ab = _dot(hi, wab_ref[...])
    ab_lo = _dot(lo, wab_ref[:, :nab])
    oab_ref[...] = ab[:, :nab] + ab[:, nab:] + ab_lo


def _inproj(x2, ln1_g, w_in, tm=512):
    n, d = x2.shape
    a3 = 3 * ATTN_WIDTH
    d3 = 3 * DN_WIDTH
    wa = w_in[:, :a3].astype(BF16)
    wd = w_in[:, a3:a3 + d3].astype(BF16)
    wz = w_in[:, a3 + d3:a3 + d3 + DN_WIDTH].astype(BF16)
    wab_f = w_in[:, a3 + d3 + DN_WIDTH:]
    nab = wab_f.shape[1]
    wab_hi, wab_lo = _split_hi_lo(wab_f)
    wab = jnp.concatenate([wab_hi, wab_lo], axis=1)
    const = lambda i: (0, 0)
    row = lambda i: (i, 0)
    return pl.pallas_call(
        _inproj_body,
        grid=(n // tm,),
        in_specs=[
            pl.BlockSpec((tm, d), row),
            pl.BlockSpec((1, d), const),
            pl.BlockSpec((d, a3), const),
            pl.BlockSpec((d, d3), const),
            pl.BlockSpec((d, DN_WIDTH), const),
            pl.BlockSpec((d, 2 * nab), const),
        ],
        out_specs=[
            pl.BlockSpec((tm, a3), row),
            pl.BlockSpec((tm, d3), row),
            pl.BlockSpec((tm, DN_WIDTH), row),
            pl.BlockSpec((tm, nab), row),
        ],
        out_shape=[
            jax.ShapeDtypeStruct((n, a3), BF16),
            jax.ShapeDtypeStruct((n, d3), BF16),
            jax.ShapeDtypeStruct((n, DN_WIDTH), BF16),
            jax.ShapeDtypeStruct((n, nab), F32),
        ],
        compiler_params=_cparams(("parallel",)),
    )(x2, ln1_g.reshape(1, d), wa, wd, wz, wab)


def _attn_bias_table(rpb):
    c = jnp.arange(GRID_W)
    c0 = jnp.clip(c - WIN_W // 2, 0, GRID_W - WIN_W)
    kc = jnp.arange(GRID_W)
    inwin = (kc[None, :] >= c0[:, None]) & (kc[None, :] < c0[:, None] + WIN_W)
    h = rpb.shape[0]
    padw = GRID_W - WIN_W
    rp = jnp.pad(rpb.astype(F32), ((0, 0), (0, 0), (padw, padw)))
    by_col = jnp.stack([rp[:, :, GRID_W - 1 - cc:2 * GRID_W - 1 - cc] for cc in range(GRID_W)], axis=2)
    tab = jnp.stack([by_col[:, WIN_H - 1 - dd:2 * WIN_H - 1 - dd] for dd in range(WIN_H)], axis=1)
    tab = tab.transpose(0, 1, 3, 2, 4)
    tab = jnp.where(inwin[None, None, :, None, :], tab, NEG)
    tab = tab.reshape(h // 2, 2, WIN_H, GRID_W, WIN_H * GRID_W)
    return tab.transpose(0, 2, 1, 3, 4).reshape(h // 2, WIN_H, 2 * GRID_W, WIN_H * GRID_W)


def _attn_body(q_ref, k_ref, v_ref, qg_ref, kg_ref, bias_ref, o_ref, qlo_s, qhi_s, kn_s, *, rows):
    s_len = rows * GRID_W
    lane = lax.broadcasted_iota(I32, (1, LANES), 1)
    is_lo = lane < ATTN_HD
    blk = 256

    hr = lax.broadcasted_iota(I32, (LANES, LANES), 0) // ATTN_HD
    hc = lax.broadcasted_iota(I32, (LANES, LANES), 1) // ATTN_HD
    same_head = jnp.where(hr == hc, 1.0, 0.0).astype(BF16)

    def head_norm(x, gain):
        ms = _dot((x * x).astype(BF16), same_head) * (1.0 / ATTN_HD)
        return x * lax.rsqrt(ms + EPS) * gain

    def prep(i, c):
        sl = pl.ds(pl.multiple_of(i * blk, blk), blk)
        qn = head_norm(q_ref[0, sl, :].astype(F32), qg_ref[...]) * (ATTN_HD ** -0.5)
        qlo_s[sl, :] = jnp.where(is_lo, qn, 0.0).astype(BF16)
        qhi_s[sl, :] = jnp.where(is_lo, 0.0, qn).astype(BF16)
        kn_s[sl, :] = head_norm(k_ref[0, sl, :].astype(F32), kg_ref[...]).astype(BF16)
        return c

    lax.fori_loop(0, s_len // blk, prep, 0)

    band = WIN_H * GRID_W

    def row_step(i, c):
        ctx = []
        for u in range(ATTN_UNROLL):
            r = i * ATTN_UNROLL + u
            r0 = jnp.clip(r - WIN_H // 2, 0, rows - WIN_H)
            qsl = pl.ds(pl.multiple_of(r * GRID_W, GRID_W), GRID_W)
            ksl = pl.ds(pl.multiple_of(r0 * GRID_W, GRID_W), band)
            q2 = jnp.concatenate([qlo_s[qsl, :], qhi_s[qsl, :]], axis=0)
            ctx.append(dict(d=r - r0, qsl=qsl, ksl=ksl, s=_dot_nt(q2, kn_s[ksl, :])))
        for c_ in ctx:
            s = c_['s'] + bias_ref[0, c_['d']]
            m = jnp.max(s, axis=-1, keepdims=True)
            p = jnp.exp(s - m)
            c_['l'] = jnp.sum(p, axis=-1, keepdims=True)
            c_['p'] = p.astype(BF16)
        for c_ in ctx:
            c_['pv'] = _dot(c_['p'], v_ref[0, c_['ksl'], :])
        for c_ in ctx:
            pv = c_['pv'] / c_['l']
            o = jnp.where(is_lo, pv[:GRID_W], pv[GRID_W:])
            o_ref[0, c_['qsl'], :] = o.astype(o_ref.dtype)
        return c

    lax.fori_loop(0, rows // ATTN_UNROLL, row_step, 0)


def _attention(aqkv, q_g, k_g, rpb):
    b, s_len, _ = aqkv.shape
    rows = s_len // GRID_W
    assert min(WIN_H, rows) == WIN_H
    pairs = ATTN_HEADS // 2
    bias = _attn_bias_table(rpb)
    qg2 = jnp.tile(q_g, 2).reshape(1, LANES).astype(F32)
    kg2 = jnp.tile(k_g, 2).reshape(1, LANES).astype(F32)
    band = WIN_H * GRID_W
    return pl.pallas_call(
        functools.partial(_attn_body, rows=rows),
        grid=(pairs, b),
        in_specs=[
            pl.BlockSpec((1, s_len, LANES), lambda p, i: (i, 0, p)),
            pl.BlockSpec((1, s_len, LANES), lambda p, i: (i, 0, pairs + p)),
            pl.BlockSpec((1, s_len, LANES), lambda p, i: (i, 0, 2 * pairs + p)),
            pl.BlockSpec((1, LANES), lambda p, i: (0, 0)),
            pl.BlockSpec((1, LANES), lambda p, i: (0, 0)),
            pl.BlockSpec((1, WIN_H, 2 * GRID_W, band), lambda p, i: (p, 0, 0, 0)),
        ],
        out_specs=pl.BlockSpec((1, s_len, LANES), lambda p, i: (i, 0, p)),
        out_shape=jax.ShapeDtypeStruct((b, s_len, ATTN_WIDTH), BF16),
        scratch_shapes=[
            pltpu.VMEM((s_len, LANES), BF16),
            pltpu.VMEM((s_len, LANES), BF16),
            pltpu.VMEM((s_len, LANES), BF16),
        ],
        compiler_params=_cparams(("arbitrary", "arbitrary")),
    )(aqkv, aqkv, aqkv, qg2, kg2, bias)


SUPER = 2 * DN_CHUNK
HALO = 16
DN_UNROLL = 8


def _dn_body(q_ref, k_ref, v_ref, z_ref, ab_ref, cwq_ref, cwk_ref, cwv_ref, alog_ref, dtb_ref, og_ref,
             o_ref, pad_s, qs, ks, vs, of_s, ob_s, *, s_len):
    head = pl.program_id(1)
    n_super = s_len // SUPER

    zeros_halo = jnp.zeros((HALO, LANES), F32)
    for t, src in enumerate((q_ref, k_ref, v_ref)):
        pad_s[t, 0:HALO, :] = zeros_halo
        pad_s[t, HALO + s_len:2 * HALO + s_len, :] = zeros_halo

        def stage(i, c, t=t, src=src):
            off = pl.multiple_of(i * 512, 512)
            pad_s[t, pl.ds(HALO + off, 512), :] = src[0, pl.ds(off, 512), :].astype(F32)
            return c

        lax.fori_loop(0, s_len // 512, stage, 0)


    def conv_step(i, c):
        t0 = pl.multiple_of(i * SUPER, SUPER)
        outs = []
        for t, cw_ref in enumerate((cwq_ref, cwk_ref, cwv_ref)):
            acc = jnp.zeros((SUPER, LANES), F32)
            for j in range(DN_CONV):
                xj = pad_s[t, pl.ds(t0 + (HALO - DN_CONV // 2 + j), SUPER), :]
                acc = acc + xj * cw_ref[j:j + 1, :]
            outs.append(acc * jax.nn.sigmoid(acc))
        qc, kc, vc = outs
        sl = pl.ds(t0, SUPER)
        qs[sl, :] = qc * lax.rsqrt(jnp.sum(qc * qc, axis=-1, keepdims=True) + EPS) * (DN_HD ** -0.5)
        ks[sl, :] = kc * lax.rsqrt(jnp.sum(kc * kc, axis=-1, keepdims=True) + EPS)
        vs[sl, :] = vc
        return c

    lax.fori_loop(0, n_super, conv_step, 0)

    ri = lax.broadcasted_iota(I32, (SUPER, SUPER), 0)
    ci = lax.broadcasted_iota(I32, (SUPER, SUPER), 1)
    same = (ri // DN_CHUNK) == (ci // DN_CHUNK)
    pr = lax.broadcasted_iota(I32, (DN_CHUNK, SUPER), 0)
    pc = lax.broadcasted_iota(I32, (DN_CHUNK, SUPER), 1)
    eye_packed = ((pc % DN_CHUNK) == pr).astype(F32)
    first_blk = pc < DN_CHUNK

    def blockdiag(p):
        zero = jnp.zeros_like(p)
        return jnp.concatenate([jnp.where(first_blk, p, zero), jnp.where(first_blk, zero, p)], axis=0)
    posl = lax.broadcasted_iota(I32, (8, LANES), 1) % DN_CHUNK

    def prep(sc, rev):
        t0 = pl.multiple_of(sc * SUPER, SUPER)
        sl = pl.ds(t0, SUPER)
        q = qs[sl, :]
        k = ks[sl, :]
        v = vs[sl, :]
        dcol = 1 if rev else 0
        a_r = jnp.broadcast_to(ab_ref[0, 0, dcol:dcol + 1, sl], (8, LANES))
        b_r = jnp.broadcast_to(ab_ref[0, 0, 2 + dcol:3 + dcol, sl], (8, LANES))
        neg_rate = -jnp.exp(jnp.full((1, LANES), alog_ref[dcol, head], F32))
        g_r = neg_rate * jax.nn.softplus(a_r + dtb_ref[dcol, head])
        beta_r = jax.nn.sigmoid(b_r)
        gc_r = g_r
        for sft in (1, 2, 4, 8, 16, 32):
            if rev:
                gc_r = gc_r + jnp.where(posl < DN_CHUNK - sft, pltpu.roll(gc_r, SUPER - sft, axis=1), 0.0)
            else:
                gc_r = gc_r + jnp.where(posl >= sft, pltpu.roll(gc_r, sft, axis=1), 0.0)
        gct = jnp.broadcast_to(gc_r[0:1], (SUPER, LANES))
        gc = gct.T
        beta = jnp.broadcast_to(beta_r[0:1], (SUPER, LANES)).T
        diff = gc - gct
        if rev:
            incl = same & (ri <= ci)
            strict = same & (ri < ci)
        else:
            incl = same & (ri >= ci)
            strict = same & (ri > ci)
        decay = jnp.where(incl, jnp.exp(jnp.where(incl, diff, 0.0)), 0.0)
        kb = k * beta
        gram = _dot_nt(jnp.concatenate([q, kb], axis=0).astype(BF16), k.astype(BF16))
        qkd = (gram[:SUPER] * decay).astype(BF16)
        low = jnp.where(strict, gram[SUPER:] * decay, 0.0)
        egc = jnp.exp(gc)
        rhs = jnp.concatenate([v * beta, kb * egc], axis=1).astype(BF16)
        low_packed = low[:DN_CHUNK] + low[DN_CHUNK:]
        return dict(sl=sl, rev=rev, gc=gc, gct=gct, qkd=qkd, rhs=rhs, qe=q * egc, kt=k.T,
                    inv=eye_packed - low_packed, pw=low_packed.astype(BF16))

    def phase1(chains):
        ctx = [prep(sc, rev) for sc, rev in chains]
        for k in range(5):
            for c in ctx:
                c['pw_next'] = _dot(c['pw'], blockdiag(c['pw'])).astype(BF16)
            if k > 0:
                for c in ctx:
                    c['inv'] = c['inv'] + _dot(c['inv'].astype(BF16), blockdiag(c['pw']))
            for c in ctx:
                c['pw'] = c['pw_next']
        for c in ctx:
            c['inv'] = c['inv'] + _dot(c['inv'].astype(BF16), blockdiag(c['pw']))
        for c in ctx:
            c['sol'] = _dot(blockdiag(c['inv'].astype(BF16)), c['rhs']).astype(BF16)
        for c in ctx:
            qo = _dot(c['qkd'], c['sol'])
            c['o0'] = qo[:, :LANES]
            c['qt'] = c['qe'] - qo[:, LANES:]
        out = []
        for c in ctx:
            rev, gc, gct = c['rev'], c['gc'], c['gct']
            per_chunk = []
            for cidx in ((1, 0) if rev else (0, 1)):
                lo = cidx * DN_CHUNK
                last = lo if rev else lo + DN_CHUNK - 1
                g_last = gc[last:last + 1, :]
                in_c = (ci // DN_CHUNK) == cidx
                kst = c['kt'] * jnp.where(in_c, jnp.exp(jnp.where(in_c, g_last - gct, 0.0)), 0.0)
                mn = _dot(kst.astype(BF16), c['sol'])
                lhs = jnp.concatenate([-mn[:, LANES:], c['qt'][lo:lo + DN_CHUNK]], axis=0).astype(BF16)
                per_chunk.append((cidx, lhs, mn[:, :LANES], jnp.exp(g_last), c['o0'][lo:lo + DN_CHUNK]))
            out.append((c['sl'], per_chunk))
        return out

    def scan(state, p1, rev):
        sl, per_chunk = p1
        outs = [None, None]
        for cidx, lhs, ku, e_last, o0_c in per_chunk:
            r = _dot(lhs, state.astype(BF16))
            outs[cidx] = r[LANES:] + o0_c
            state = e_last * state + r[:LANES] + ku
        dst = ob_s if rev else of_s
        dst[sl, :] = jnp.concatenate(outs, axis=0)
        return state

    def scan_step(i, carry):
        s_f, s_b = carry
        chains = []
        for u in range(DN_UNROLL):
            chains += [(i * DN_UNROLL + u, False), (n_super - 1 - (i * DN_UNROLL + u), True)]
        p1 = phase1(chains)
        p_f, p_b = p1[0::2], p1[1::2]
        for u in range(DN_UNROLL):
            s_f = scan(s_f, p_f[u], False)
            s_b = scan(s_b, p_b[u], True)
        return s_f, s_b

    z0 = jnp.zeros((DN_HD, DN_HD), F32)
    lax.fori_loop(0, n_super // DN_UNROLL, scan_step, (z0, z0))

    def fin(i, c):
        sl = pl.ds(pl.multiple_of(i * 512, 512), 512)
        o = of_s[sl, :] + ob_s[sl, :]
        o = o * lax.rsqrt(jnp.mean(o * o, axis=-1, keepdims=True) + EPS) * og_ref[...]
        zf = z_ref[0, sl, :].astype(F32)
        o_ref[0, sl, :] = (o * (zf * jax.nn.sigmoid(zf))).astype(o_ref.dtype)
        return c

    lax.fori_loop(0, s_len // 512, fin, 0)


def _deltanet(dqkv, dz, dab, conv_w, a_log, dt_bias, out_g):
    b, s_len, _ = dqkv.shape
    h = DN_HEADS
    ab = dab.reshape(b, s_len, 2, 2, h).transpose(0, 4, 2, 3, 1).reshape(b, h, 4, s_len)
    cw = conv_w.astype(F32)
    tok = lambda off: (lambda i, j: (i, 0, off + j))
    smem = pl.BlockSpec(memory_space=pltpu.SMEM)
    return pl.pallas_call(
        functools.partial(_dn_body, s_len=s_len),
        grid=(b, h),
        in_specs=[
            pl.BlockSpec((1, s_len, LANES), tok(0)),
            pl.BlockSpec((1, s_len, LANES), tok(h)),
            pl.BlockSpec((1, s_len, LANES), tok(2 * h)),
            pl.BlockSpec((1, s_len, LANES), tok(0)),
            pl.BlockSpec((1, 1, 4, s_len), lambda i, j: (i, j, 0, 0)),
            pl.BlockSpec((DN_CONV, LANES), lambda i, j: (0, j)),
            pl.BlockSpec((DN_CONV, LANES), lambda i, j: (0, h + j)),
            pl.BlockSpec((DN_CONV, LANES), lambda i, j: (0, 2 * h + j)),
            smem,
            smem,
            pl.BlockSpec((1, LANES), lambda i, j: (0, 0)),
        ],
        out_specs=pl.BlockSpec((1, s_len, LANES), tok(0)),
        out_shape=jax.ShapeDtypeStruct((b, s_len, DN_WIDTH), BF16),
        scratch_shapes=[
            pltpu.VMEM((3, s_len + 2 * HALO, LANES), F32),
            pltpu.VMEM((s_len, LANES), F32),
            pltpu.VMEM((s_len, LANES), F32),
            pltpu.VMEM((s_len, LANES), F32),
            pltpu.VMEM((s_len, LANES), F32),
            pltpu.VMEM((s_len, LANES), F32),
        ],
        compiler_params=_cparams(("arbitrary", "arbitrary")),
    )(dqkv, dqkv, dqkv, dz, ab, cw, cw, cw, a_log.astype(F32), dt_bias.astype(F32),
      out_g.reshape(1, LANES).astype(F32))


ROUTER_ROWS = 80
TOK_SUB = 8


def _to_token_tiles(x):
    blocks = jnp.stack([x[:, s * LANES:(s + 1) * LANES] for s in range(TOK_SUB)], axis=0)
    return pltpu.einshape("spl->psl", blocks)


def _from_token_tiles(t):
    blocks = pltpu.einshape("psl->spl", t)
    return jnp.concatenate([blocks[s] for s in range(TOK_SUB)], axis=1)


def _outproj_body(x_ref, at_ref, dn_ref, ag_ref, wo_ref, lg_ref, wrh_ref, wrl_ref, rb_ref, su_ref,
                  x1_ref, h2_ref, eid_ref, gate_ref, rank_ref, cnt_ref, base_s):
    step = pl.program_id(0)
    tm = x_ref.shape[0]

    @pl.when(step == 0)
    def _():
        base_s[...] = jnp.zeros_like(base_s)

    a = at_ref[...].astype(F32)
    an = a * lax.rsqrt(jnp.mean(a * a, axis=-1, keepdims=True) + EPS) * ag_ref[...]
    mix = jnp.concatenate([an.astype(BF16), dn_ref[...]], axis=1)
    x1 = x_ref[...] + _dot(mix, wo_ref[...])
    x1_ref[...] = x1
    h2 = x1 * lax.rsqrt(jnp.mean(x1 * x1, axis=-1, keepdims=True) + EPS) * lg_ref[...]
    h2_ref[...] = _to_token_tiles(h2)

    hi, lo = _split_hi_lo(h2)
    wrh = wrh_ref[...]
    lt = _dot_nt(wrh, hi) + _dot_nt(wrh, lo) + _dot_nt(wrl_ref[...], hi) + rb_ref[...]

    sub8 = lax.broadcasted_iota(I32, (N_GROUPS, tm), 0)
    gl = lt[0:N_GROUPS]
    gmax = jnp.max(gl, axis=0, keepdims=True)
    gsel = jnp.min(jnp.where(gl == gmax, sub8, N_GROUPS), axis=0, keepdims=True)
    gp = 1.0 / jnp.sum(jnp.exp(gl - gmax), axis=0, keepdims=True)
    in_group = jnp.zeros((EPG, tm), F32)
    for g in range(N_GROUPS):
        in_group = in_group + jnp.where(gsel == g, lt[N_GROUPS + g * EPG:N_GROUPS + (g + 1) * EPG], 0.0)
    v1 = jnp.max(in_group, axis=0, keepdims=True)
    i1 = jnp.min(jnp.where(in_group == v1, sub8, EPG), axis=0, keepdims=True)
    rest = jnp.where(sub8 == i1, -jnp.inf, in_group)
    v2 = jnp.max(rest, axis=0, keepdims=True)
    i2 = jnp.min(jnp.where(rest == v2, sub8, EPG), axis=0, keepdims=True)
    e21 = jnp.exp(v2 - v1)
    den = 1.0 + e21
    eid1 = gsel * EPG + i1
    eid2 = gsel * EPG + i2
    eid_ref[0:1, :] = eid1
    eid_ref[1:2, :] = eid2
    gate_ref[0:1, :] = gp * (1.0 / den)
    gate_ref[1:2, :] = gp * (e21 / den)

    sub64 = lax.broadcasted_iota(I32, (N_EXPERTS, tm), 0)
    oh1 = sub64 == eid1
    oh2 = sub64 == eid2
    ohs = jnp.where(oh1 | oh2, 1.0, 0.0)
    before = base_s[...] + _dot(ohs.astype(BF16), su_ref[...])
    rank_ref[0:1, :] = jnp.sum(jnp.where(oh1, before, 0.0), axis=0, keepdims=True).astype(I32)
    rank_ref[1:2, :] = jnp.sum(jnp.where(oh2, before, 0.0), axis=0, keepdims=True).astype(I32)
    base_new = base_s[...] + jnp.sum(ohs, axis=1, keepdims=True)
    base_s[...] = base_new
    cnt_ref[...] = jnp.broadcast_to(base_new, cnt_ref.shape)


def _outproj_router(x2, attn_raw, dn, attn_g, w_out, ln2_g, wg_r, bg_r, we_r, be_r, tm=512):
    n, d = x2.shape
    wr = jnp.concatenate([wg_r, we_r], axis=1).T.astype(F32)
    wr = jnp.pad(wr, ((0, ROUTER_ROWS - wr.shape[0]), (0, 0)))
    wrh, wrl = _split_hi_lo(wr)
    rb = jnp.pad(jnp.concatenate([bg_r, be_r]).astype(F32), (0, ROUTER_ROWS - N_GROUPS - N_EXPERTS))
    rb = rb.reshape(ROUTER_ROWS, 1)
    t_i = jnp.arange(tm)
    su = (t_i[:, None] < t_i[None, :]).astype(BF16)
    const = lambda i: (0, 0)
    row = lambda i: (i, 0)
    col = lambda i: (0, i)
    return pl.pallas_call(
        _outproj_body,
        grid=(n // tm,),
        in_specs=[
            pl.BlockSpec((tm, d), row),
            pl.BlockSpec((tm, ATTN_WIDTH), row),
            pl.BlockSpec((tm, DN_WIDTH), row),
            pl.BlockSpec((1, ATTN_WIDTH), const),
            pl.BlockSpec((ATTN_WIDTH + DN_WIDTH, d), const),
            pl.BlockSpec((1, d), const),
            pl.BlockSpec((ROUTER_ROWS, d), const),
            pl.BlockSpec((ROUTER_ROWS, d), const),
            pl.BlockSpec((ROUTER_ROWS, 1), const),
            pl.BlockSpec((tm, tm), const),
        ],
        out_specs=[
            pl.BlockSpec((tm, d), row),
            pl.BlockSpec((tm, TOK_SUB, LANES), lambda i: (i, 0, 0)),
            pl.BlockSpec((2, tm), col),
            pl.BlockSpec((2, tm), col),
            pl.BlockSpec((2, tm), col),
            pl.BlockSpec((N_EXPERTS, LANES), const),
        ],
        out_shape=[
            jax.ShapeDtypeStruct((n, d), F32),
            jax.ShapeDtypeStruct((n, TOK_SUB, LANES), F32),
            jax.ShapeDtypeStruct((2, n), I32),
            jax.ShapeDtypeStruct((2, n), F32),
            jax.ShapeDtypeStruct((2, n), I32),
            jax.ShapeDtypeStruct((N_EXPERTS, LANES), F32),
        ],
        scratch_shapes=[pltpu.VMEM((N_EXPERTS, 1), F32)],
        compiler_params=_cparams(("arbitrary",)),
    )(x2, attn_raw, dn, attn_g.reshape(1, -1).astype(F32), w_out.astype(BF16), ln2_g.reshape(1, d).astype(F32),
      wrh, wrl, rb, su)


def _dispatch_body(dest_ref, zrow_ref, h_ref, xs_hbm, zbuf, zsem, sem, *, n):
    tm = h_ref.shape[0]
    base = pl.program_id(0) * tm

    @pl.when(pl.program_id(0) == 0)
    def _():
        zbuf[...] = jnp.zeros_like(zbuf)

        def zero_copy(e):
            return pltpu.make_async_copy(zbuf, xs_hbm.at[pl.ds(zrow_ref[e], MOE_BLOCK)], zsem)

        def start(e, c):
            @pl.when(zrow_ref[e] >= 0)
            def _():
                zero_copy(e).start()
            return c

        def finish(e, c):
            @pl.when(zrow_ref[e] >= 0)
            def _():
                zero_copy(e).wait()
            return c

        lax.fori_loop(0, N_EXPERTS, start, 0)
        lax.fori_loop(0, N_EXPERTS, finish, 0)

        def tail_copy(b):
            return pltpu.make_async_copy(zbuf, xs_hbm.at[pl.ds(b * MOE_BLOCK, MOE_BLOCK)], zsem)

        def tail_start(b, c):
            tail_copy(b).start()
            return c

        def tail_finish(b, c):
            tail_copy(b).wait()
            return c

        n_blocks = xs_hbm.shape[0] // MOE_BLOCK
        lax.fori_loop(zrow_ref[N_EXPERTS], n_blocks, tail_start, 0)
        lax.fori_loop(zrow_ref[N_EXPERTS], n_blocks, tail_finish, 0)

    def issue(j, c):
        src = h_ref.at[pl.ds(j, 1)]
        pltpu.make_async_copy(src, xs_hbm.at[pl.ds(dest_ref[base + j], 1)], sem).start()
        pltpu.make_async_copy(src, xs_hbm.at[pl.ds(dest_ref[n + base + j], 1)], sem).start()
        return c

    lax.fori_loop(0, tm, issue, 0, unroll=8)
    for _ in range(2):
        pltpu.make_async_copy(h_ref, xs_hbm.at[pl.ds(0, tm)], sem).wait()


def _dispatch(h2t, dest, zrow, p_rows, tm=512):
    n = h2t.shape[0]
    grid_spec = pltpu.PrefetchScalarGridSpec(
        num_scalar_prefetch=2,
        grid=(n // tm,),
        in_specs=[pl.BlockSpec((tm, TOK_SUB, LANES), lambda i, dst, zr: (i, 0, 0))],
        out_specs=pl.BlockSpec(memory_space=pl.ANY),
        scratch_shapes=[pltpu.VMEM((MOE_BLOCK, TOK_SUB, LANES), F32), pltpu.SemaphoreType.DMA(()),
                        pltpu.SemaphoreType.DMA(())],
    )
    return pl.pallas_call(
        functools.partial(_dispatch_body, n=n),
        grid_spec=grid_spec,
        out_shape=jax.ShapeDtypeStruct((p_rows, TOK_SUB, LANES), F32),
        compiler_params=_cparams(("arbitrary",)),
    )(dest.reshape(-1), zrow, h2t)


def _expert_body(be_ref, nu_ref, xs_ref, wg_ref, wu_ref, wd_ref, o_ref):
    used = pl.program_id(0) < nu_ref[0]

    @pl.when(used)
    def _():
        x = _from_token_tiles(xs_ref[...]).astype(BF16)
        gt = _dot(x, wg_ref[0])
        up = _dot(x, wu_ref[0])
        hid = gt * jax.nn.sigmoid(gt) * up
        out = _dot(hid.astype(BF16), wd_ref[0])
        o_ref[...] = _to_token_tiles(out)

    @pl.when(jnp.logical_not(used))
    def _():
        o_ref[...] = jnp.zeros_like(o_ref)


def _experts(xs, block_expert, n_used, w_gate, w_up, w_down):
    p_rows = xs.shape[0]
    nb = p_rows // MOE_BLOCK
    d, de = w_gate.shape[-2:]
    wmap = lambda i, be, nu: (be[i], 0, 0)
    xmap = lambda i, be, nu: (jnp.minimum(i, nu[0] - 1), 0, 0)
    grid_spec = pltpu.PrefetchScalarGridSpec(
        num_scalar_prefetch=2,
        grid=(nb,),
        in_specs=[
            pl.BlockSpec((MOE_BLOCK, TOK_SUB, LANES), xmap),
            pl.BlockSpec((1, d, de), wmap),
            pl.BlockSpec((1, d, de), wmap),
            pl.BlockSpec((1, de, d), wmap),
        ],
        out_specs=pl.BlockSpec((MOE_BLOCK, TOK_SUB, LANES), lambda i, be, nu: (i, 0, 0)),
    )
    return pl.pallas_call(
        _expert_body,
        grid_spec=grid_spec,
        out_shape=jax.ShapeDtypeStruct((p_rows, TOK_SUB, LANES), F32),
        compiler_params=_cparams(("arbitrary",)),
    )(block_expert, n_used, xs, w_gate.astype(BF16), w_up.astype(BF16), w_down.astype(BF16))


def _combine_body(dest_ref, x1_ref, gt_ref, eo_hbm, out_ref, rbuf, sem, *, n):
    tm = x1_ref.shape[0]
    step = pl.program_id(0)
    slot = step % 2

    def gather(tile, slot_):
        base = tile * tm

        def issue(j, c):
            d0 = dest_ref[base + j]
            d1 = dest_ref[n + base + j]
            pltpu.make_async_copy(eo_hbm.at[pl.ds(d0, 1)], rbuf.at[slot_, 0, pl.ds(j, 1)], sem.at[slot_]).start()
            pltpu.make_async_copy(eo_hbm.at[pl.ds(d1, 1)], rbuf.at[slot_, 1, pl.ds(j, 1)], sem.at[slot_]).start()
            return c

        lax.fori_loop(0, tm, issue, 0, unroll=8)

    @pl.when(step == 0)
    def _():
        gather(0, 0)

    @pl.when(step + 1 < pl.num_programs(0))
    def _():
        gather(step + 1, 1 - slot)

    for k in range(2):
        pltpu.make_async_copy(eo_hbm.at[pl.ds(0, tm)], rbuf.at[slot, k], sem.at[slot]).wait()
    g0 = jnp.broadcast_to(gt_ref[:, 0:1], (tm, LANES))
    g1 = jnp.broadcast_to(gt_ref[:, 1:2], (tm, LANES))
    r0 = pltpu.einshape("psl->spl", rbuf[slot, 0])
    r1 = pltpu.einshape("psl->spl", rbuf[slot, 1])
    for s in range(TOK_SUB):
        cols = slice(s * LANES, (s + 1) * LANES)
        out_ref[:, cols] = x1_ref[:, cols] + (g0 * r0[s] + g1 * r1[s])


def _combine(x1, gates_t, dest, eo, tm=256):
    n, d = x1.shape
    grid_spec = pltpu.PrefetchScalarGridSpec(
        num_scalar_prefetch=1,
        grid=(n // tm,),
        in_specs=[
            pl.BlockSpec((tm, d), lambda i, dst: (i, 0)),
            pl.BlockSpec((tm, 2), lambda i, dst: (i, 0)),
            pl.BlockSpec(memory_space=pl.ANY),
        ],
        out_specs=pl.BlockSpec((tm, d), lambda i, dst: (i, 0)),
        scratch_shapes=[pltpu.VMEM((2, 2, tm, TOK_SUB, LANES), F32), pltpu.SemaphoreType.DMA((2,))],
    )
    return pl.pallas_call(
        functools.partial(_combine_body, n=n),
        grid_spec=grid_spec,
        out_shape=jax.ShapeDtypeStruct((n, d), F32),
        compiler_params=_cparams(("arbitrary",)),
    )(dest.reshape(-1), x1, gates_t, eo)


def _dispatch_plan(eid, rank, counts, n):
    nk = 2 * n
    p_rows = -(-nk // MOE_BLOCK) * MOE_BLOCK + N_EXPERTS * MOE_BLOCK
    nb = p_rows // MOE_BLOCK
    padded = -(-counts // MOE_BLOCK) * MOE_BLOCK
    pends = jnp.cumsum(padded)
    pstarts = pends - padded
    experts = jnp.arange(N_EXPERTS, dtype=I32)
    dest = jnp.sum(jnp.where(eid[..., None] == experts, pstarts.astype(I32), 0), axis=-1) + rank
    block_start = jnp.arange(nb, dtype=I32) * MOE_BLOCK
    block_expert = jnp.clip(jnp.sum((pends[None, :] <= block_start[:, None]).astype(I32), axis=-1),
                            0, N_EXPERTS - 1).astype(I32)
    n_used = (pends[-1] // MOE_BLOCK).astype(I32).reshape(1)
    zrow = jnp.concatenate([jnp.where(padded > 0, pends - MOE_BLOCK, -1).astype(I32), n_used])
    return dest.astype(I32), zrow, block_expert, n_used, p_rows


def _layer(x, ln1_g, w_in, q_g, k_g, rpb, attn_g, conv_w, a_log, dt_bias, dn_g, w_out, ln2_g,
           wg_r, bg_r, we_r, be_r, w_gate, w_up, w_down):
    b, s_len, d = x.shape
    n = b * s_len
    x2 = x.reshape(n, d)
    aqkv, dqkv, dz, dab = _inproj(x2, ln1_g, w_in)
    attn_raw = _attention(aqkv.reshape(b, s_len, -1), q_g, k_g, rpb)
    dn = _deltanet(dqkv.reshape(b, s_len, -1), dz.reshape(b, s_len, -1), dab.reshape(b, s_len, -1),
                   conv_w, a_log, dt_bias, dn_g)
    x1, h2t, eid, gates, rank, cnt = _outproj_router(
        x2, attn_raw.reshape(n, -1), dn.reshape(n, -1), attn_g, w_out, ln2_g, wg_r, bg_r, we_r, be_r)
    counts = cnt[:, 0].astype(I32)
    dest, zrow, block_expert, n_used, p_rows = _dispatch_plan(eid, rank, counts, n)
    xs = _dispatch(h2t, dest, zrow, p_rows)
    eo = _experts(xs, block_expert, n_used, w_gate, w_up, w_down)
    out = _combine(x1, gates.T, dest, eo)
    return out.reshape(b, s_len, d)


def kernel(x, ln1_g, w_in, attn_q_norm_g, attn_k_norm_g, attn_rpb, attn_out_norm_g, dn_conv_w, dn_a_log,
           dn_dt_bias, dn_out_norm_g, w_out, ln2_g, router_group_w, router_group_b, router_expert_w,
           router_expert_b, expert_w_gate, expert_w_up, expert_w_down):
    for l in range(ln1_g.shape[0]):
        x = _layer(x, ln1_g[l], w_in[l], attn_q_norm_g[l], attn_k_norm_g[l], attn_rpb[l],
                   attn_out_norm_g[l], dn_conv_w[l], dn_a_log[l], dn_dt_bias[l], dn_out_norm_g[l],
                   w_out[l], ln2_g[l], router_group_w[l], router_group_b[l], router_expert_w[l],
                   router_expert_b[l], expert_w_gate[l], expert_w_up[l], expert_w_down[l])
    return x
```

```python
import functools

import jax
import jax.numpy as jnp
from jax import lax
from jax.experimental import pallas as pl
from jax.experimental.pallas import tpu as pltpu

F32 = jnp.float32
BF16 = jnp.bfloat16
I32 = jnp.int32

EPS = 1e-6
GRID_W = 64
WIN_H = 8
WIN_W = 16
ATTN_HEADS = 8
ATTN_HD = 64
ATTN_WIDTH = ATTN_HEADS * ATTN_HD
DN_HEADS = 4
DN_HD = 128
DN_WIDTH = DN_HEADS * DN_HD
DN_CONV = 5
DN_CHUNK = 64
N_GROUPS = 8
EPG = 8
N_EXPERTS = N_GROUPS * EPG
MOE_BLOCK = 256
NEG = -1e30

ATTN_UNROLL = 8
LANES = 128
VMEM_LIMIT = 56 * 1024 * 1024


def _cparams(sem):
    return pltpu.CompilerParams(dimension_semantics=sem, vmem_limit_bytes=VMEM_LIMIT)


def _dot(a, b):
    return jnp.dot(a, b, preferred_element_type=F32)


def _dot_nt(a, b):
    return lax.dot_general(a, b, (((1,), (1,)), ((), ())), preferred_element_type=F32)


def _split_hi_lo(x):
    hi = x.astype(BF16)
    lo = (x - hi.astype(F32)).astype(BF16)
    return hi, lo


def _inproj_body(x_ref, g_ref, wa_ref, wd_ref, wz_ref, wab_ref, oa_ref, od_ref, oz_ref, oab_ref):
    x = x_ref[...]
    ms = jnp.mean(x * x, axis=-1, keepdims=True)
    h = x * lax.rsqrt(ms + EPS) * g_ref[...]
    hi, lo = _split_hi_lo(h)
    oa_ref[...] = _dot(hi, wa_ref[...]).astype(BF16)
    od_ref[...] = _dot(hi, wd_ref[...]).astype(BF16)
    oz_ref[...] = _dot(hi, wz_ref[...]).astype(BF16)
    nab = oab_ref.shape[-1]
    ab = _dot(hi, wab_ref[...])
    ab_lo = _dot(lo, wab_ref[:, :nab])
    oab_ref[...] = ab[:, :nab] + ab[:, nab:] + ab_lo


def _inproj(x2, ln1_g, w_in, tm=512):
    n, d = x2.shape
    a3 = 3 * ATTN_WIDTH
    d3 = 3 * DN_WIDTH
    wa = w_in[:, :a3].astype(BF16)
    wd = w_in[:, a3:a3 + d3].astype(BF16)
    wz = w_in[:, a3 + d3:a3 + d3 + DN_WIDTH].astype(BF16)
    wab_f = w_in[:, a3 + d3 + DN_WIDTH:]
    nab = wab_f.shape[1]
    wab_hi, wab_lo = _split_hi_lo(wab_f)
    wab = jnp.concatenate([wab_hi, wab_lo], axis=1)
    const = lambda i: (0, 0)
    row = lambda i: (i, 0)
    return pl.pallas_call(
        _inproj_body,
        grid=(n // tm,),
        in_specs=[
            pl.BlockSpec((tm, d), row),
            pl.BlockSpec((1, d), const),
            pl.BlockSpec((d, a3), const),
            pl.BlockSpec((d, d3), const),
            pl.BlockSpec((d, DN_WIDTH), const),
            pl.BlockSpec((d, 2 * nab), const),
        ],
        out_specs=[
            pl.BlockSpec((tm, a3), row),
            pl.BlockSpec((tm, d3), row),
            pl.BlockSpec((tm, DN_WIDTH), row),
            pl.BlockSpec((tm, nab), row),
        ],
        out_shape=[
            jax.ShapeDtypeStruct((n, a3), BF16),
            jax.ShapeDtypeStruct((n, d3), BF16),
            jax.ShapeDtypeStruct((n, DN_WIDTH), BF16),
            jax.ShapeDtypeStruct((n, nab), F32),
        ],
        compiler_params=_cparams(("parallel",)),
    )(x2, ln1_g.reshape(1, d), wa, wd, wz, wab)


def _attn_bias_table(rpb):
    c = jnp.arange(GRID_W)
    c0 = jnp.clip(c - WIN_W // 2, 0, GRID_W - WIN_W)
    kc = jnp.arange(GRID_W)
    inwin = (kc[None, :] >= c0[:, None]) & (kc[None, :] < c0[:, None] + WIN_W)
    h = rpb.shape[0]
    padw = GRID_W - WIN_W
    rp = jnp.pad(rpb.astype(F32), ((0, 0), (0, 0), (padw, padw)))
    by_col = jnp.stack([rp[:, :, GRID_W - 1 - cc:2 * GRID_W - 1 - cc] for cc in range(GRID_W)], axis=2)
    tab = jnp.stack([by_col[:, WIN_H - 1 - dd:2 * WIN_H - 1 - dd] for dd in range(WIN_H)], axis=1)
    tab = tab.transpose(0, 1, 3, 2, 4)
    tab = jnp.where(inwin[None, None, :, None, :], tab, NEG)
    tab = tab.reshape(h // 2, 2, WIN_H, GRID_W, WIN_H * GRID_W)
    return tab.transpose(0, 2, 1, 3, 4).reshape(h // 2, WIN_H, 2 * GRID_W, WIN_H * GRID_W)


def _attn_body(q_ref, k_ref, v_ref, qg_ref, kg_ref, bias_ref, o_ref, qlo_s, qhi_s, kn_s, *, rows):
    s_len = rows * GRID_W
    lane = lax.broadcasted_iota(I32, (1, LANES), 1)
    is_lo = lane < ATTN_HD
    blk = 256

    hr = lax.broadcasted_iota(I32, (LANES, LANES), 0) // ATTN_HD
    hc = lax.broadcasted_iota(I32, (LANES, LANES), 1) // ATTN_HD
    same_head = jnp.where(hr == hc, 1.0, 0.0).astype(BF16)

    def head_norm(x, gain):
        ms = _dot((x * x).astype(BF16), same_head) * (1.0 / ATTN_HD)
        return x * lax.rsqrt(ms + EPS) * gain

    def prep(i, c):
        sl = pl.ds(pl.multiple_of(i * blk, blk), blk)
        qn = head_norm(q_ref[0, sl, :].astype(F32), qg_ref[...]) * (ATTN_HD ** -0.5)
        qlo_s[sl, :] = jnp.where(is_lo, qn, 0.0).astype(BF16)
        qhi_s[sl, :] = jnp.where(is_lo, 0.0, qn).astype(BF16)
        kn_s[sl, :] = head_norm(k_ref[0, sl, :].astype(F32), kg_ref[...]).astype(BF16)
        return c

    lax.fori_loop(0, s_len // blk, prep, 0)

    band = WIN_H * GRID_W

    def row_step(i, c):
        ctx = []
        for u in range(ATTN_UNROLL):
            r = i * ATTN_UNROLL + u
            r0 = jnp.clip(r - WIN_H // 2, 0, rows - WIN_H)
            qsl = pl.ds(pl.multiple_of(r * GRID_W, GRID_W), GRID_W)
            ksl = pl.ds(pl.multiple_of(r0 * GRID_W, GRID_W), band)
            q2 = jnp.concatenate([qlo_s[qsl, :], qhi_s[qsl, :]], axis=0)
            ctx.append(dict(d=r - r0, qsl=qsl, ksl=ksl, s=_dot_nt(q2, kn_s[ksl, :])))
        for c_ in ctx:
            s = c_['s'] + bias_ref[0, c_['d']]
            m = jnp.max(s, axis=-1, keepdims=True)
            p = jnp.exp(s - m)
            c_['l'] = jnp.sum(p, axis=-1, keepdims=True)
            c_['p'] = p.astype(BF16)
        for c_ in ctx:
            c_['pv'] = _dot(c_['p'], v_ref[0, c_['ksl'], :])
        for c_ in ctx:
            pv = c_['pv'] / c_['l']
            o = jnp.where(is_lo, pv[:GRID_W], pv[GRID_W:])
            o_ref[0, c_['qsl'], :] = o.astype(o_ref.dtype)
        return c

    lax.fori_loop(0, rows // ATTN_UNROLL, row_step, 0)


def _attention(aqkv, q_g, k_g, rpb):
    b, s_len, _ = aqkv.shape
    rows = s_len // GRID_W
    assert min(WIN_H, rows) == WIN_H
    pairs = ATTN_HEADS // 2
    bias = _attn_bias_table(rpb)
    qg2 = jnp.tile(q_g, 2).reshape(1, LANES).astype(F32)
    kg2 = jnp.tile(k_g, 2).reshape(1, LANES).astype(F32)
    band = WIN_H * GRID_W
    return pl.pallas_call(
        functools.partial(_attn_body, rows=rows),
        grid=(pairs, b),
        in_specs=[
            pl.BlockSpec((1, s_len, LANES), lambda p, i: (i, 0, p)),
            pl.BlockSpec((1, s_len, LANES), lambda p, i: (i, 0, pairs + p)),
            pl.BlockSpec((1, s_len, LANES), lambda p, i: (i, 0, 2 * pairs + p)),
            pl.BlockSpec((1, LANES), lambda p, i: (0, 0)),
            pl.BlockSpec((1, LANES), lambda p, i: (0, 0)),
            pl.BlockSpec((1, WIN_H, 2 * GRID_W, band), lambda p, i: (p, 0, 0, 0)),
        ],
        out_specs=pl.BlockSpec((1, s_len, LANES), lambda p, i: (i, 0, p)),
        out_shape=jax.ShapeDtypeStruct((b, s_len, ATTN_WIDTH), BF16),
        scratch_shapes=[
            pltpu.VMEM((s_len, LANES), BF16),
            pltpu.VMEM((s_len, LANES), BF16),
            pltpu.VMEM((s_len, LANES), BF16),
        ],
        compiler_params=_cparams(("arbitrary", "arbitrary")),
    )(aqkv, aqkv, aqkv, qg2, kg2, bias)


SUPER = 2 * DN_CHUNK
HALO = 16
DN_UNROLL = 8


def _dn_body(q_ref, k_ref, v_ref, z_ref, ab_ref, cwq_ref, cwk_ref, cwv_ref, alog_ref, dtb_ref, og_ref,
             o_ref, pad_s, qs, ks, vs, of_s, ob_s, *, s_len):
    head = pl.program_id(1)
    n_super = s_len // SUPER

    zeros_halo = jnp.zeros((HALO, LANES), F32)
    for t, src in enumerate((q_ref, k_ref, v_ref)):
        pad_s[t, 0:HALO, :] = zeros_halo
        pad_s[t, HALO + s_len:2 * HALO + s_len, :] = zeros_halo

        def stage(i, c, t=t, src=src):
            off = pl.multiple_of(i * 512, 512)
            pad_s[t, pl.ds(HALO + off, 512), :] = src[0, pl.ds(off, 512), :].astype(F32)
            return c

        lax.fori_loop(0, s_len // 512, stage, 0)


    def conv_step(i, c):
        t0 = pl.multiple_of(i * SUPER, SUPER)
        outs = []
        for t, cw_ref in enumerate((cwq_ref, cwk_ref, cwv_ref)):
            acc = jnp.zeros((SUPER, LANES), F32)
            for j in range(DN_CONV):
                xj = pad_s[t, pl.ds(t0 + (HALO - DN_CONV // 2 + j), SUPER), :]
                acc = acc + xj * cw_ref[j:j + 1, :]
            outs.append(acc * jax.nn.sigmoid(acc))
        qc, kc, vc = outs
        sl = pl.ds(t0, SUPER)
        qs[sl, :] = qc * lax.rsqrt(jnp.sum(qc * qc, axis=-1, keepdims=True) + EPS) * (DN_HD ** -0.5)
        ks[sl, :] = kc * lax.rsqrt(jnp.sum(kc * kc, axis=-1, keepdims=True) + EPS)
        vs[sl, :] = vc
        return c

    lax.fori_loop(0, n_super, conv_step, 0)

    ri = lax.broadcasted_iota(I32, (SUPER, SUPER), 0)
    ci = lax.broadcasted_iota(I32, (SUPER, SUPER), 1)
    same = (ri // DN_CHUNK) == (ci // DN_CHUNK)
    pr = lax.broadcasted_iota(I32, (DN_CHUNK, SUPER), 0)
    pc = lax.broadcasted_iota(I32, (DN_CHUNK, SUPER), 1)
    eye_packed = ((pc % DN_CHUNK) == pr).astype(F32)
    first_blk = pc < DN_CHUNK

    def blockdiag(p):
        zero = jnp.zeros_like(p)
        return jnp.concatenate([jnp.where(first_blk, p, zero), jnp.where(first_blk, zero, p)], axis=0)
    posl = lax.broadcasted_iota(I32, (8, LANES), 1) % DN_CHUNK

    def prep(sc, rev):
        t0 = pl.multiple_of(sc * SUPER, SUPER)
        sl = pl.ds(t0, SUPER)
        q = qs[sl, :]
        k = ks[sl, :]
        v = vs[sl, :]
        dcol = 1 if rev else 0
        a_r = jnp.broadcast_to(ab_ref[0, 0, dcol:dcol + 1, sl], (8, LANES))
        b_r = jnp.broadcast_to(ab_ref[0, 0, 2 + dcol:3 + dcol, sl], (8, LANES))
        neg_rate = -jnp.exp(jnp.full((1, LANES), alog_ref[dcol, head], F32))
        g_r = neg_rate * jax.nn.softplus(a_r + dtb_ref[dcol, head])
        beta_r = jax.nn.sigmoid(b_r)
        gc_r = g_r
        for sft in (1, 2, 4, 8, 16, 32):
            if rev:
                gc_r = gc_r + jnp.where(posl < DN_CHUNK - sft, pltpu.roll(gc_r, SUPER - sft, axis=1), 0.0)
            else:
                gc_r = gc_r + jnp.where(posl >= sft, pltpu.roll(gc_r, sft, axis=1), 0.0)
        gct = jnp.broadcast_to(gc_r[0:1], (SUPER, LANES))
        gc = gct.T
        beta = jnp.broadcast_to(beta_r[0:1], (SUPER, LANES)).T
        diff = gc - gct
        if rev:
            incl = same & (ri <= ci)
            strict = same & (ri < ci)
        else:
            incl = same & (ri >= ci)
            strict = same & (ri > ci)
        decay = jnp.where(incl, jnp.exp(jnp.where(incl, diff, 0.0)), 0.0)
        kb = k * beta
        gram = _dot_nt(jnp.concatenate([q, kb], axis=0).astype(BF16), k.astype(BF16))
        qkd = (gram[:SUPER] * decay).astype(BF16)
        low = jnp.where(strict, gram[SUPER:] * decay, 0.0)
        egc = jnp.exp(gc)
        rhs = jnp.concatenate([v * beta, kb * egc], axis=1).astype(BF16)
        low_packed = low[:DN_CHUNK] + low[DN_CHUNK:]
        return dict(sl=sl, rev=rev, gc=gc, gct=gct, qkd=qkd, rhs=rhs, qe=q * egc, kt=k.T,
                    inv=eye_packed - low_packed, pw=low_packed.astype(BF16))

    def phase1(chains):
        ctx = [prep(sc, rev) for sc, rev in chains]
        for k in range(5):
            for c in ctx:
                c['pw_next'] = _dot(c['pw'], blockdiag(c['pw'])).astype(BF16)
            if k > 0:
                for c in ctx:
                    c['inv'] = c['inv'] + _dot(c['inv'].astype(BF16), blockdiag(c['pw']))
            for c in ctx:
                c['pw'] = c['pw_next']
        for c in ctx:
            c['inv'] = c['inv'] + _dot(c['inv'].astype(BF16), blockdiag(c['pw']))
        for c in ctx:
            c['sol'] = _dot(blockdiag(c['inv'].astype(BF16)), c['rhs']).astype(BF16)
        for c in ctx:
            qo = _dot(c['qkd'], c['sol'])
            c['o0'] = qo[:, :LANES]
            c['qt'] = c['qe'] - qo[:, LANES:]
        out = []
        for c in ctx:
            rev, gc, gct = c['rev'], c['gc'], c['gct']
            per_chunk = []
            for cidx in ((1, 0) if rev else (0, 1)):
                lo = cidx * DN_CHUNK
                last = lo if rev else lo + DN_CHUNK - 1
                g_last = gc[last:last + 1, :]
                in_c = (ci // DN_CHUNK) == cidx
                kst = c['kt'] * jnp.where(in_c, jnp.exp(jnp.where(in_c, g_last - gct, 0.0)), 0.0)
                mn = _dot(kst.astype(BF16), c['sol'])
                lhs = jnp.concatenate([-mn[:, LANES:], c['qt'][lo:lo + DN_CHUNK]], axis=0).astype(BF16)
                per_chunk.append((cidx, lhs, mn[:, :LANES], jnp.exp(g_last), c['o0'][lo:lo + DN_CHUNK]))
            out.append((c['sl'], per_chunk))
        return out

    def scan(state, p1, rev):
        sl, per_chunk = p1
        outs = [None, None]
        for cidx, lhs, ku, e_last, o0_c in per_chunk:
            r = _dot(lhs, state.astype(BF16))
            outs[cidx] = r[LANES:] + o0_c
            state = e_last * state + r[:LANES] + ku
        dst = ob_s if rev else of_s
        dst[sl, :] = jnp.concatenate(outs, axis=0)
        return state

    def scan_step(i, carry):
        s_f, s_b = carry
        chains = []
        for u in range(DN_UNROLL):
            chains += [(i * DN_UNROLL + u, False), (n_super - 1 - (i * DN_UNROLL + u), True)]
        p1 = phase1(chains)
        p_f, p_b = p1[0::2], p1[1::2]
        for u in range(DN_UNROLL):
            s_f = scan(s_f, p_f[u], False)
            s_b = scan(s_b, p_b[u], True)
        return s_f, s_b

    z0 = jnp.zeros((DN_HD, DN_HD), F32)
    lax.fori_loop(0, n_super // DN_UNROLL, scan_step, (z0, z0))

    def fin(i, c):
        sl = pl.ds(pl.multiple_of(i * 512, 512), 512)
        o = of_s[sl, :] + ob_s[sl, :]
        o = o * lax.rsqrt(jnp.mean(o * o, axis=-1, keepdims=True) + EPS) * og_ref[...]
        zf = z_ref[0, sl, :].astype(F32)
        o_ref[0, sl, :] = (o * (zf * jax.nn.sigmoid(zf))).astype(o_ref.dtype)
        return c

    lax.fori_loop(0, s_len // 512, fin, 0)


def _deltanet(dqkv, dz, dab, conv_w, a_log, dt_bias, out_g):
    b, s_len, _ = dqkv.shape
    h = DN_HEADS
    ab = dab.reshape(b, s_len, 2, 2, h).transpose(0, 4, 2, 3, 1).reshape(b, h, 4, s_len)
    cw = conv_w.astype(F32)
    tok = lambda off: (lambda i, j: (i, 0, off + j))
    smem = pl.BlockSpec(memory_space=pltpu.SMEM)
    return pl.pallas_call(
        functools.partial(_dn_body, s_len=s_len),
        grid=(b, h),
        in_specs=[
            pl.BlockSpec((1, s_len, LANES), tok(0)),
            pl.BlockSpec((1, s_len, LANES), tok(h)),
            pl.BlockSpec((1, s_len, LANES), tok(2 * h)),
            pl.BlockSpec((1, s_len, LANES), tok(0)),
            pl.BlockSpec((1, 1, 4, s_len), lambda i, j: (i, j, 0, 0)),
            pl.BlockSpec((DN_CONV, LANES), lambda i, j: (0, j)),
            pl.BlockSpec((DN_CONV, LANES), lambda i, j: (0, h + j)),
            pl.BlockSpec((DN_CONV, LANES), lambda i, j: (0, 2 * h + j)),
            smem,
            smem,
            pl.BlockSpec((1, LANES), lambda i, j: (0, 0)),
        ],
        out_specs=pl.BlockSpec((1, s_len, LANES), tok(0)),
        out_shape=jax.ShapeDtypeStruct((b, s_len, DN_WIDTH), BF16),
        scratch_shapes=[
            pltpu.VMEM((3, s_len + 2 * HALO, LANES), F32),
            pltpu.VMEM((s_len, LANES), F32),
            pltpu.VMEM((s_len, LANES), F32),
            pltpu.VMEM((s_len, LANES), F32),
            pltpu.VMEM((s_len, LANES), F32),
            pltpu.VMEM((s_len, LANES), F32),
        ],
        compiler_params=_cparams(("arbitrary", "arbitrary")),
    )(dqkv, dqkv, dqkv, dz, ab, cw, cw, cw, a_log.astype(F32), dt_bias.astype(F32),
      out_g.reshape(1, LANES).astype(F32))


ROUTER_ROWS = 80
TOK_SUB = 8


def _to_token_tiles(x):
    blocks = jnp.stack([x[:, s * LANES:(s + 1) * LANES] for s in range(TOK_SUB)], axis=0)
    return pltpu.einshape("spl->psl", blocks)


def _from_token_tiles(t):
    blocks = pltpu.einshape("psl->spl", t)
    return jnp.concatenate([blocks[s] for s in range(TOK_SUB)], axis=1)


def _outproj_body(x_ref, at_ref, dn_ref, ag_ref, wo_ref, lg_ref, wrh_ref, wrl_ref, rb_ref, su_ref,
                  x1_ref, h2_ref, eid_ref, gate_ref, rank_ref, cnt_ref, base_s):
    step = pl.program_id(0)
    tm = x_ref.shape[0]

    @pl.when(step == 0)
    def _():
        base_s[...] = jnp.zeros_like(base_s)

    a = at_ref[...].astype(F32)
    an = a * lax.rsqrt(jnp.mean(a * a, axis=-1, keepdims=True) + EPS) * ag_ref[...]
    mix = jnp.concatenate([an.astype(BF16), dn_ref[...]], axis=1)
    x1 = x_ref[...] + _dot(mix, wo_ref[...])
    x1_ref[...] = x1
    h2 = x1 * lax.rsqrt(jnp.mean(x1 * x1, axis=-1, keepdims=True) + EPS) * lg_ref[...]
    h2_ref[...] = _to_token_tiles(h2)

    hi, lo = _split_hi_lo(h2)
    wrh = wrh_ref[...]
    lt = _dot_nt(wrh, hi) + _dot_nt(wrh, lo) + _dot_nt(wrl_ref[...], hi) + rb_ref[...]

    sub8 = lax.broadcasted_iota(I32, (N_GROUPS, tm), 0)
    gl = lt[0:N_GROUPS]
    gmax = jnp.max(gl, axis=0, keepdims=True)
    gsel = jnp.min(jnp.where(gl == gmax, sub8, N_GROUPS), axis=0, keepdims=True)
    gp = 1.0 / jnp.sum(jnp.exp(gl - gmax), axis=0, keepdims=True)
    in_group = jnp.zeros((EPG, tm), F32)
    for g in range(N_GROUPS):
        in_group = in_group + jnp.where(gsel == g, lt[N_GROUPS + g * EPG:N_GROUPS + (g + 1) * EPG], 0.0)
    v1 = jnp.max(in_group, axis=0, keepdims=True)
    i1 = jnp.min(jnp.where(in_group == v1, sub8, EPG), axis=0, keepdims=True)
    rest = jnp.where(sub8 == i1, -jnp.inf, in_group)
    v2 = jnp.max(rest, axis=0, keepdims=True)
    i2 = jnp.min(jnp.where(rest == v2, sub8, EPG), axis=0, keepdims=True)
    e21 = jnp.exp(v2 - v1)
    den = 1.0 + e21
    eid1 = gsel * EPG + i1
    eid2 = gsel * EPG + i2
    eid_ref[0:1, :] = eid1
    eid_ref[1:2, :] = eid2
    gate_ref[0:1, :] = gp * (1.0 / den)
    gate_ref[1:2, :] = gp * (e21 / den)

    sub64 = lax.broadcasted_iota(I32, (N_EXPERTS, tm), 0)
    oh1 = sub64 == eid1
    oh2 = sub64 == eid2
    ohs = jnp.where(oh1 | oh2, 1.0, 0.0)
    before = base_s[...] + _dot(ohs.astype(BF16), su_ref[...])
    rank_ref[0:1, :] = jnp.sum(jnp.where(oh1, before, 0.0), axis=0, keepdims=True).astype(I32)
    rank_ref[1:2, :] = jnp.sum(jnp.where(oh2, before, 0.0), axis=0, keepdims=True).astype(I32)
    base_new = base_s[...] + jnp.sum(ohs, axis=1, keepdims=True)
    base_s[...] = base_new
    cnt_ref[...] = jnp.broadcast_to(base_new, cnt_ref.shape)


def _outproj_router(x2, attn_raw, dn, attn_g, w_out, ln2_g, wg_r, bg_r, we_r, be_r, tm=512):
    n, d = x2.shape
    wr = jnp.concatenate([wg_r, we_r], axis=1).T.astype(F32)
    wr = jnp.pad(wr, ((0, ROUTER_ROWS - wr.shape[0]), (0, 0)))
    wrh, wrl = _split_hi_lo(wr)
    rb = jnp.pad(jnp.concatenate([bg_r, be_r]).astype(F32), (0, ROUTER_ROWS - N_GROUPS - N_EXPERTS))
    rb = rb.reshape(ROUTER_ROWS, 1)
    t_i = jnp.arange(tm)
    su = (t_i[:, None] < t_i[None, :]).astype(BF16)
    const = lambda i: (0, 0)
    row = lambda i: (i, 0)
    col = lambda i: (0, i)
    return pl.pallas_call(
        _outproj_body,
        grid=(n // tm,),
        in_specs=[
            pl.BlockSpec((tm, d), row),
            pl.BlockSpec((tm, ATTN_WIDTH), row),
            pl.BlockSpec((tm, DN_WIDTH), row),
            pl.BlockSpec((1, ATTN_WIDTH), const),
            pl.BlockSpec((ATTN_WIDTH + DN_WIDTH, d), const),
            pl.BlockSpec((1, d), const),
            pl.BlockSpec((ROUTER_ROWS, d), const),
            pl.BlockSpec((ROUTER_ROWS, d), const),
            pl.BlockSpec((ROUTER_ROWS, 1), const),
            pl.BlockSpec((tm, tm), const),
        ],
        out_specs=[
            pl.BlockSpec((tm, d), row),
            pl.BlockSpec((tm, TOK_SUB, LANES), lambda i: (i, 0, 0)),
            pl.BlockSpec((2, tm), col),
            pl.BlockSpec((2, tm), col),
            pl.BlockSpec((2, tm), col),
            pl.BlockSpec((N_EXPERTS, LANES), const),
        ],
        out_shape=[
            jax.ShapeDtypeStruct((n, d), F32),
            jax.ShapeDtypeStruct((n, TOK_SUB, LANES), F32),
            jax.ShapeDtypeStruct((2, n), I32),
            jax.ShapeDtypeStruct((2, n), F32),
            jax.ShapeDtypeStruct((2, n), I32),
            jax.ShapeDtypeStruct((N_EXPERTS, LANES), F32),
        ],
        scratch_shapes=[pltpu.VMEM((N_EXPERTS, 1), F32)],
        compiler_params=_cparams(("arbitrary",)),
    )(x2, attn_raw, dn, attn_g.reshape(1, -1).astype(F32), w_out.astype(BF16), ln2_g.reshape(1, d).astype(F32),
      wrh, wrl, rb, su)


def _dispatch_body(dest_ref, zrow_ref, h_ref, xs_hbm, zbuf, zsem, sem, *, n):
    tm = h_ref.shape[0]
    base = pl.program_id(0) * tm

    @pl.when(pl.program_id(0) == 0)
    def _():
        zbuf[...] = jnp.zeros_like(zbuf)

        def zero_copy(e):
            return pltpu.make_async_copy(zbuf, xs_hbm.at[pl.ds(zrow_ref[e], MOE_BLOCK)], zsem)

        def start(e, c):
            @pl.when(zrow_ref[e] >= 0)
            def _():
                zero_copy(e).start()
            return c

        def finish(e, c):
            @pl.when(zrow_ref[e] >= 0)
            def _():
                zero_copy(e).wait()
            return c

        lax.fori_loop(0, N_EXPERTS, start, 0)
        lax.fori_loop(0, N_EXPERTS, finish, 0)

        def tail_copy(b):
            return pltpu.make_async_copy(zbuf, xs_hbm.at[pl.ds(b * MOE_BLOCK, MOE_BLOCK)], zsem)

        def tail_start(b, c):
            tail_copy(b).start()
            return c

        def tail_finish(b, c):
            tail_copy(b).wait()
            return c

        n_blocks = xs_hbm.shape[0] // MOE_BLOCK
        lax.fori_loop(zrow_ref[N_EXPERTS], n_blocks, tail_start, 0)
        lax.fori_loop(zrow_ref[N_EXPERTS], n_blocks, tail_finish, 0)

    def issue(j, c):
        src = h_ref.at[pl.ds(j, 1)]
        pltpu.make_async_copy(src, xs_hbm.at[pl.ds(dest_ref[base + j], 1)], sem).start()
        pltpu.make_async_copy(src, xs_hbm.at[pl.ds(dest_ref[n + base + j], 1)], sem).start()
        return c

    lax.fori_loop(0, tm, issue, 0, unroll=8)
    for _ in range(2):
        pltpu.make_async_copy(h_ref, xs_hbm.at[pl.ds(0, tm)], sem).wait()


def _dispatch(h2t, dest, zrow, p_rows, tm=512):
    n = h2t.shape[0]
    grid_spec = pltpu.PrefetchScalarGridSpec(
        num_scalar_prefetch=2,
        grid=(n // tm,),
        in_specs=[pl.BlockSpec((tm, TOK_SUB, LANES), lambda i, dst, zr: (i, 0, 0))],
        out_specs=pl.BlockSpec(memory_space=pl.ANY),
        scratch_shapes=[pltpu.VMEM((MOE_BLOCK, TOK_SUB, LANES), F32), pltpu.SemaphoreType.DMA(()),
                        pltpu.SemaphoreType.DMA(())],
    )
    return pl.pallas_call(
        functools.partial(_dispatch_body, n=n),
        grid_spec=grid_spec,
        out_shape=jax.ShapeDtypeStruct((p_rows, TOK_SUB, LANES), F32),
        compiler_params=_cparams(("arbitrary",)),
    )(dest.reshape(-1), zrow, h2t)


EXPERT_STEP_BLOCKS = 2


def _expert_body(be_ref, nu_ref, xs_ref, *refs):
    w_refs, o_ref = refs[:-1], refs[-1]
    for h in range(EXPERT_STEP_BLOCKS):
        wg_ref, wu_ref, wd_ref = w_refs[3 * h:3 * h + 3]
        rows = pl.ds(h * MOE_BLOCK, MOE_BLOCK)
        used = pl.program_id(0) * EXPERT_STEP_BLOCKS + h < nu_ref[0]

        @pl.when(used)
        def _():
            x = _from_token_tiles(xs_ref[rows]).astype(BF16)
            gt = _dot(x, wg_ref[0])
            up = _dot(x, wu_ref[0])
            hid = gt * jax.nn.sigmoid(gt) * up
            out = _dot(hid.astype(BF16), wd_ref[0])
            o_ref[rows] = _to_token_tiles(out)

        @pl.when(jnp.logical_not(used))
        def _():
            o_ref[rows] = jnp.zeros((MOE_BLOCK, TOK_SUB, LANES), o_ref.dtype)


def _experts(xs, block_expert, n_used, w_gate, w_up, w_down):
    p_rows = xs.shape[0]
    step_rows = EXPERT_STEP_BLOCKS * MOE_BLOCK
    d, de = w_gate.shape[-2:]
    wmap = lambda h: (lambda i, be, nu: (be[i * EXPERT_STEP_BLOCKS + h], 0, 0))
    xmap = lambda i, be, nu: (jnp.minimum(i, (nu[0] - 1) // EXPERT_STEP_BLOCKS), 0, 0)
    w_specs = []
    for h in range(EXPERT_STEP_BLOCKS):
        w_specs += [pl.BlockSpec((1, d, de), wmap(h)), pl.BlockSpec((1, d, de), wmap(h)),
                    pl.BlockSpec((1, de, d), wmap(h))]
    grid_spec = pltpu.PrefetchScalarGridSpec(
        num_scalar_prefetch=2,
        grid=(p_rows // step_rows,),
        in_specs=[pl.BlockSpec((step_rows, TOK_SUB, LANES), xmap)] + w_specs,
        out_specs=pl.BlockSpec((step_rows, TOK_SUB, LANES), lambda i, be, nu: (i, 0, 0)),
    )
    wg, wu, wd = w_gate.astype(BF16), w_up.astype(BF16), w_down.astype(BF16)
    return pl.pallas_call(
        _expert_body,
        grid_spec=grid_spec,
        out_shape=jax.ShapeDtypeStruct((p_rows, TOK_SUB, LANES), F32),
        compiler_params=_cparams(("arbitrary",)),
    )(block_expert, n_used, xs, *([wg, wu, wd] * EXPERT_STEP_BLOCKS))


def _combine_body(dest_ref, x1_ref, gt_ref, eo_hbm, out_ref, rbuf, sem, *, n):
    tm = x1_ref.shape[0]
    step = pl.program_id(0)
    slot = step % 2

    def gather(tile, slot_):
        base = tile * tm

        def issue(j, c):
            d0 = dest_ref[base + j]
            d1 = dest_ref[n + base + j]
            pltpu.make_async_copy(eo_hbm.at[pl.ds(d0, 1)], rbuf.at[slot_, 0, pl.ds(j, 1)], sem.at[slot_]).start()
            pltpu.make_async_copy(eo_hbm.at[pl.ds(d1, 1)], rbuf.at[slot_, 1, pl.ds(j, 1)], sem.at[slot_]).start()
            return c

        lax.fori_loop(0, tm, issue, 0, unroll=8)

    @pl.when(step == 0)
    def _():
        gather(0, 0)

    @pl.when(step + 1 < pl.num_programs(0))
    def _():
        gather(step + 1, 1 - slot)

    for k in range(2):
        pltpu.make_async_copy(eo_hbm.at[pl.ds(0, tm)], rbuf.at[slot, k], sem.at[slot]).wait()
    g0 = jnp.broadcast_to(gt_ref[:, 0:1], (tm, LANES))
    g1 = jnp.broadcast_to(gt_ref[:, 1:2], (tm, LANES))
    r0 = pltpu.einshape("psl->spl", rbuf[slot, 0])
    r1 = pltpu.einshape("psl->spl", rbuf[slot, 1])
    for s in range(TOK_SUB):
        cols = slice(s * LANES, (s + 1) * LANES)
        out_ref[:, cols] = x1_ref[:, cols] + (g0 * r0[s] + g1 * r1[s])


def _combine(x1, gates_t, dest, eo, tm=256):
    n, d = x1.shape
    grid_spec = pltpu.PrefetchScalarGridSpec(
        num_scalar_prefetch=1,
        grid=(n // tm,),
        in_specs=[
            pl.BlockSpec((tm, d), lambda i, dst: (i, 0)),
            pl.BlockSpec((tm, 2), lambda i, dst: (i, 0)),
            pl.BlockSpec(memory_space=pl.ANY),
        ],
        out_specs=pl.BlockSpec((tm, d), lambda i, dst: (i, 0)),
        scratch_shapes=[pltpu.VMEM((2, 2, tm, TOK_SUB, LANES), F32), pltpu.SemaphoreType.DMA((2,))],
    )
    return pl.pallas_call(
        functools.partial(_combine_body, n=n),
        grid_spec=grid_spec,
        out_shape=jax.ShapeDtypeStruct((n, d), F32),
        compiler_params=_cparams(("arbitrary",)),
    )(dest.reshape(-1), x1, gates_t, eo)


def _dispatch_plan(eid, rank, counts, n):
    nk = 2 * n
    p_rows = -(-nk // MOE_BLOCK) * MOE_BLOCK + N_EXPERTS * MOE_BLOCK
    nb = p_rows // MOE_BLOCK
    padded = -(-counts // MOE_BLOCK) * MOE_BLOCK
    pends = jnp.cumsum(padded)
    pstarts = pends - padded
    experts = jnp.arange(N_EXPERTS, dtype=I32)
    dest = jnp.sum(jnp.where(eid[..., None] == experts, pstarts.astype(I32), 0), axis=-1) + rank
    block_start = jnp.arange(nb, dtype=I32) * MOE_BLOCK
    block_expert = jnp.clip(jnp.sum((pends[None, :] <= block_start[:, None]).astype(I32), axis=-1),
                            0, N_EXPERTS - 1).astype(I32)
    n_used = (pends[-1] // MOE_BLOCK).astype(I32).reshape(1)
    zrow = jnp.concatenate([jnp.where(padded > 0, pends - MOE_BLOCK, -1).astype(I32), n_used])
    return dest.astype(I32), zrow, block_expert, n_used, p_rows


def _layer(x, ln1_g, w_in, q_g, k_g, rpb, attn_g, conv_w, a_log, dt_bias, dn_g, w_out, ln2_g,
           wg_r, bg_r, we_r, be_r, w_gate, w_up, w_down):
    b, s_len, d = x.shape
    n = b * s_len
    x2 = x.reshape(n, d)
    aqkv, dqkv, dz, dab = _inproj(x2, ln1_g, w_in)
    attn_raw = _attention(aqkv.reshape(b, s_len, -1), q_g, k_g, rpb)
    dn = _deltanet(dqkv.reshape(b, s_len, -1), dz.reshape(b, s_len, -1), dab.reshape(b, s_len, -1),
                   conv_w, a_log, dt_bias, dn_g)
    x1, h2t, eid, gates, rank, cnt = _outproj_router(
        x2, attn_raw.reshape(n, -1), dn.reshape(n, -1), attn_g, w_out, ln2_g, wg_r, bg_r, we_r, be_r)
    counts = cnt[:, 0].astype(I32)
    dest, zrow, block_expert, n_used, p_rows = _dispatch_plan(eid, rank, counts, n)
    xs = _dispatch(h2t, dest, zrow, p_rows)
    eo = _experts(xs, block_expert, n_used, w_gate, w_up, w_down)
    out = _combine(x1, gates.T, dest, eo)
    return out.reshape(b, s_len, d)


def kernel(x, ln1_g, w_in, attn_q_norm_g, attn_k_norm_g, attn_rpb, attn_out_norm_g, dn_conv_w, dn_a_log,
           dn_dt_bias, dn_out_norm_g, w_out, ln2_g, router_group_w, router_group_b, router_expert_w,
           router_expert_b, expert_w_gate, expert_w_up, expert_w_down):
    for l in range(ln1_g.shape[0]):
        x = _layer(x, ln1_g[l], w_in[l], attn_q_norm_g[l], attn_k_norm_g[l], attn_rpb[l],
                   attn_out_norm_g[l], dn_conv_w[l], dn_a_log[l], dn_dt_bias[l], dn_out_norm_g[l],
                   w_out[l], ln2_g[l], router_group_w[l], router_group_b[l], router_expert_w[l],
                   router_expert_b[l], expert_w_gate[l], expert_w_up[l], expert_w_down[l])
    return x
```

```python
import functools

import jax
import jax.numpy as jnp
from jax import lax
from jax.experimental import pallas as pl
from jax.experimental.pallas import tpu as pltpu

F32 = jnp.float32
BF16 = jnp.bfloat16
I32 = jnp.int32

EPS = 1e-6
GRID_W = 64
WIN_H = 8
WIN_W = 16
ATTN_HEADS = 8
ATTN_HD = 64
ATTN_WIDTH = ATTN_HEADS * ATTN_HD
DN_HEADS = 4
DN_HD = 128
DN_WIDTH = DN_HEADS * DN_HD
DN_CONV = 5
DN_CHUNK = 64
N_GROUPS = 8
EPG = 8
N_EXPERTS = N_GROUPS * EPG
MOE_BLOCK = 256
NEG = -1e30

ATTN_UNROLL = 8
LANES = 128
VMEM_LIMIT = 56 * 1024 * 1024


def _cparams(sem):
    return pltpu.CompilerParams(dimension_semantics=sem, vmem_limit_bytes=VMEM_LIMIT)


def _dot(a, b):
    return jnp.dot(a, b, preferred_element_type=F32)


def _dot_nt(a, b):
    return lax.dot_general(a, b, (((1,), (1,)), ((), ())), preferred_element_type=F32)


def _split_hi_lo(x):
    hi = x.astype(BF16)
    lo = (x - hi.astype(F32)).astype(BF16)
    return hi, lo


def _inproj_body(x_ref, g_ref, wa_ref, wd_ref, wz_ref, wab_ref, oa_ref, od_ref, oz_ref, oab_ref):
    x = x_ref[...]
    ms = jnp.mean(x * x, axis=-1, keepdims=True)
    h = x * lax.rsqrt(ms + EPS) * g_ref[...]
    hi, lo = _split_hi_lo(h)
    oa_ref[...] = _dot(hi, wa_ref[...]).astype(BF16)
    od_ref[...] = _dot(hi, wd_ref[...]).astype(BF16)
    oz_ref[...] = _dot(hi, wz_ref[...]).astype(BF16)
    nab = oab_ref.shape[-1]
    ab = _dot(hi, wab_ref[...])
    ab_lo = _dot(lo, wab_ref[:, :nab])
    oab_ref[...] = ab[:, :nab] + ab[:, nab:] + ab_lo


def _inproj(x2, ln1_g, w_in, tm=512):
    n, d = x2.shape
    a3 = 3 * ATTN_WIDTH
    d3 = 3 * DN_WIDTH
    wa = w_in[:, :a3].astype(BF16)
    wd = w_in[:, a3:a3 + d3].astype(BF16)
    wz = w_in[:, a3 + d3:a3 + d3 + DN_WIDTH].astype(BF16)
    wab_f = w_in[:, a3 + d3 + DN_WIDTH:]
    nab = wab_f.shape[1]
    wab_hi, wab_lo = _split_hi_lo(wab_f)
    wab = jnp.concatenate([wab_hi, wab_lo], axis=1)
    const = lambda i: (0, 0)
    row = lambda i: (i, 0)
    return pl.pallas_call(
        _inproj_body,
        grid=(n // tm,),
        in_specs=[
            pl.BlockSpec((tm, d), row),
            pl.BlockSpec((1, d), const),
            pl.BlockSpec((d, a3), const),
            pl.BlockSpec((d, d3), const),
            pl.BlockSpec((d, DN_WIDTH), const),
            pl.BlockSpec((d, 2 * nab), const),
        ],
        out_specs=[
            pl.BlockSpec((tm, a3), row),
            pl.BlockSpec((tm, d3), row),
            pl.BlockSpec((tm, DN_WIDTH), row),
            pl.BlockSpec((tm, nab), row),
        ],
        out_shape=[
            jax.ShapeDtypeStruct((n, a3), BF16),
            jax.ShapeDtypeStruct((n, d3), BF16),
            jax.ShapeDtypeStruct((n, DN_WIDTH), BF16),
            jax.ShapeDtypeStruct((n, nab), F32),
        ],
        compiler_params=_cparams(("parallel",)),
    )(x2, ln1_g.reshape(1, d), wa, wd, wz, wab)


def _attn_bias_table(rpb):
    c = jnp.arange(GRID_W)
    c0 = jnp.clip(c - WIN_W // 2, 0, GRID_W - WIN_W)
    kc = jnp.arange(GRID_W)
    inwin = (kc[None, :] >= c0[:, None]) & (kc[None, :] < c0[:, None] + WIN_W)
    h = rpb.shape[0]
    padw = GRID_W - WIN_W
    rp = jnp.pad(rpb.astype(F32), ((0, 0), (0, 0), (padw, padw)))
    by_col = jnp.stack([rp[:, :, GRID_W - 1 - cc:2 * GRID_W - 1 - cc] for cc in range(GRID_W)], axis=2)
    tab = jnp.stack([by_col[:, WIN_H - 1 - dd:2 * WIN_H - 1 - dd] for dd in range(WIN_H)], axis=1)
    tab = tab.transpose(0, 1, 3, 2, 4)
    tab = jnp.where(inwin[None, None, :, None, :], tab, NEG)
    tab = tab.reshape(h // 2, 2, WIN_H, GRID_W, WIN_H * GRID_W)
    return tab.transpose(0, 2, 1, 3, 4).reshape(h // 2, WIN_H, 2 * GRID_W, WIN_H * GRID_W)


def _attn_body(q_ref, k_ref, v_ref, qg_ref, kg_ref, bias_ref, o_ref, qlo_s, qhi_s, kn_s, *, rows):
    s_len = rows * GRID_W
    lane = lax.broadcasted_iota(I32, (1, LANES), 1)
    is_lo = lane < ATTN_HD
    blk = 256

    hr = lax.broadcasted_iota(I32, (LANES, LANES), 0) // ATTN_HD
    hc = lax.broadcasted_iota(I32, (LANES, LANES), 1) // ATTN_HD
    same_head = jnp.where(hr == hc, 1.0, 0.0).astype(BF16)

    def head_norm(x, gain):
        ms = _dot((x * x).astype(BF16), same_head) * (1.0 / ATTN_HD)
        return x * lax.rsqrt(ms + EPS) * gain

    def prep(i, c):
        sl = pl.ds(pl.multiple_of(i * blk, blk), blk)
        qn = head_norm(q_ref[0, sl, :].astype(F32), qg_ref[...]) * (ATTN_HD ** -0.5)
        qlo_s[sl, :] = jnp.where(is_lo, qn, 0.0).astype(BF16)
        qhi_s[sl, :] = jnp.where(is_lo, 0.0, qn).astype(BF16)
        kn_s[sl, :] = head_norm(k_ref[0, sl, :].astype(F32), kg_ref[...]).astype(BF16)
        return c

    lax.fori_loop(0, s_len // blk, prep, 0)

    band = WIN_H * GRID_W

    def row_step(i, c):
        ctx = []
        for u in range(ATTN_UNROLL):
            r = i * ATTN_UNROLL + u
            r0 = jnp.clip(r - WIN_H // 2, 0, rows - WIN_H)
            qsl = pl.ds(pl.multiple_of(r * GRID_W, GRID_W), GRID_W)
            ksl = pl.ds(pl.multiple_of(r0 * GRID_W, GRID_W), band)
            q2 = jnp.concatenate([qlo_s[qsl, :], qhi_s[qsl, :]], axis=0)
            ctx.append(dict(d=r - r0, qsl=qsl, ksl=ksl, s=_dot_nt(q2, kn_s[ksl, :])))
        for c_ in ctx:
            s = c_['s'] + bias_ref[0, c_['d']]
            m = jnp.max(s, axis=-1, keepdims=True)
            p = jnp.exp(s - m)
            c_['l'] = jnp.sum(p, axis=-1, keepdims=True)
            c_['p'] = p.astype(BF16)
        for c_ in ctx:
            c_['pv'] = _dot(c_['p'], v_ref[0, c_['ksl'], :])
        for c_ in ctx:
            pv = c_['pv'] / c_['l']
            o = jnp.where(is_lo, pv[:GRID_W], pv[GRID_W:])
            o_ref[0, c_['qsl'], :] = o.astype(o_ref.dtype)
        return c

    lax.fori_loop(0, rows // ATTN_UNROLL, row_step, 0)


def _attention(aqkv, q_g, k_g, rpb):
    b, s_len, _ = aqkv.shape
    rows = s_len // GRID_W
    assert min(WIN_H, rows) == WIN_H
    pairs = ATTN_HEADS // 2
    bias = _attn_bias_table(rpb)
    qg2 = jnp.tile(q_g, 2).reshape(1, LANES).astype(F32)
    kg2 = jnp.tile(k_g, 2).reshape(1, LANES).astype(F32)
    band = WIN_H * GRID_W
    return pl.pallas_call(
        functools.partial(_attn_body, rows=rows),
        grid=(pairs, b),
        in_specs=[
            pl.BlockSpec((1, s_len, LANES), lambda p, i: (i, 0, p)),
            pl.BlockSpec((1, s_len, LANES), lambda p, i: (i, 0, pairs + p)),
            pl.BlockSpec((1, s_len, LANES), lambda p, i: (i, 0, 2 * pairs + p)),
            pl.BlockSpec((1, LANES), lambda p, i: (0, 0)),
            pl.BlockSpec((1, LANES), lambda p, i: (0, 0)),
            pl.BlockSpec((1, WIN_H, 2 * GRID_W, band), lambda p, i: (p, 0, 0, 0)),
        ],
        out_specs=pl.BlockSpec((1, s_len, LANES), lambda p, i: (i, 0, p)),
        out_shape=jax.ShapeDtypeStruct((b, s_len, ATTN_WIDTH), BF16),
        scratch_shapes=[
            pltpu.VMEM((s_len, LANES), BF16),
            pltpu.VMEM((s_len, LANES), BF16),
            pltpu.VMEM((s_len, LANES), BF16),
        ],
        compiler_params=_cparams(("arbitrary", "arbitrary")),
    )(aqkv, aqkv, aqkv, qg2, kg2, bias)


SUPER = 2 * DN_CHUNK
HALO = 16
DN_UNROLL = 8


def _dn_body(q_ref, k_ref, v_ref, z_ref, ab_ref, cwq_ref, cwk_ref, cwv_ref, alog_ref, dtb_ref, og_ref,
             o_ref, pad_s, qs, ks, vs, of_s, ob_s, *, s_len):
    head = pl.program_id(1)
    n_super = s_len // SUPER

    zeros_halo = jnp.zeros((HALO, LANES), F32)
    for t, src in enumerate((q_ref, k_ref, v_ref)):
        pad_s[t, 0:HALO, :] = zeros_halo
        pad_s[t, HALO + s_len:2 * HALO + s_len, :] = zeros_halo

        def stage(i, c, t=t, src=src):
            off = pl.multiple_of(i * 512, 512)
            pad_s[t, pl.ds(HALO + off, 512), :] = src[0, pl.ds(off, 512), :].astype(F32)
            return c

        lax.fori_loop(0, s_len // 512, stage, 0)


    def conv_step(i, c):
        t0 = pl.multiple_of(i * SUPER, SUPER)
        outs = []
        for t, cw_ref in enumerate((cwq_ref, cwk_ref, cwv_ref)):
            acc = jnp.zeros((SUPER, LANES), F32)
            for j in range(DN_CONV):
                xj = pad_s[t, pl.ds(t0 + (HALO - DN_CONV // 2 + j), SUPER), :]
                acc = acc + xj * cw_ref[j:j + 1, :]
            outs.append(acc * jax.nn.sigmoid(acc))
        qc, kc, vc = outs
        sl = pl.ds(t0, SUPER)
        qs[sl, :] = qc * lax.rsqrt(jnp.sum(qc * qc, axis=-1, keepdims=True) + EPS) * (DN_HD ** -0.5)
        ks[sl, :] = kc * lax.rsqrt(jnp.sum(kc * kc, axis=-1, keepdims=True) + EPS)
        vs[sl, :] = vc
        return c

    lax.fori_loop(0, n_super, conv_step, 0)

    ri = lax.broadcasted_iota(I32, (SUPER, SUPER), 0)
    ci = lax.broadcasted_iota(I32, (SUPER, SUPER), 1)
    same = (ri // DN_CHUNK) == (ci // DN_CHUNK)
    pr = lax.broadcasted_iota(I32, (DN_CHUNK, SUPER), 0)
    pc = lax.broadcasted_iota(I32, (DN_CHUNK, SUPER), 1)
    eye_packed = ((pc % DN_CHUNK) == pr).astype(F32)
    first_blk = pc < DN_CHUNK

    def blockdiag(p):
        zero = jnp.zeros_like(p)
        return jnp.concatenate([jnp.where(first_blk, p, zero), jnp.where(first_blk, zero, p)], axis=0)
    posl = lax.broadcasted_iota(I32, (8, LANES), 1) % DN_CHUNK

    def prep(sc, rev):
        t0 = pl.multiple_of(sc * SUPER, SUPER)
        sl = pl.ds(t0, SUPER)
        q = qs[sl, :]
        k = ks[sl, :]
        v = vs[sl, :]
        dcol = 1 if rev else 0
        a_r = jnp.broadcast_to(ab_ref[0, 0, dcol:dcol + 1, sl], (8, LANES))
        b_r = jnp.broadcast_to(ab_ref[0, 0, 2 + dcol:3 + dcol, sl], (8, LANES))
        neg_rate = -jnp.exp(jnp.full((1, LANES), alog_ref[dcol, head], F32))
        g_r = neg_rate * jax.nn.softplus(a_r + dtb_ref[dcol, head])
        beta_r = jax.nn.sigmoid(b_r)
        gc_r = g_r
        for sft in (1, 2, 4, 8, 16, 32):
            if rev:
                gc_r = gc_r + jnp.where(posl < DN_CHUNK - sft, pltpu.roll(gc_r, SUPER - sft, axis=1), 0.0)
            else:
                gc_r = gc_r + jnp.where(posl >= sft, pltpu.roll(gc_r, sft, axis=1), 0.0)
        gct = jnp.broadcast_to(gc_r[0:1], (SUPER, LANES))
        gc = gct.T
        beta = jnp.broadcast_to(beta_r[0:1], (SUPER, LANES)).T
        diff = gc - gct
        if rev:
            incl = same & (ri <= ci)
            strict = same & (ri < ci)
        else:
            incl = same & (ri >= ci)
            strict = same & (ri > ci)
        decay = jnp.where(incl, jnp.exp(jnp.where(incl, diff, 0.0)), 0.0)
        kb = k * beta
        gram = _dot_nt(jnp.concatenate([q, kb], axis=0).astype(BF16), k.astype(BF16))
        qkd = (gram[:SUPER] * decay).astype(BF16)
        low = jnp.where(strict, gram[SUPER:] * decay, 0.0)
        egc = jnp.exp(gc)
        rhs = jnp.concatenate([v * beta, kb * egc], axis=1).astype(BF16)
        low_packed = low[:DN_CHUNK] + low[DN_CHUNK:]
        return dict(sl=sl, rev=rev, gc=gc, gct=gct, qkd=qkd, rhs=rhs, qe=q * egc, kt=k.T,
                    inv=eye_packed - low_packed, pw=low_packed.astype(BF16))

    def phase1(chains):
        ctx = [prep(sc, rev) for sc, rev in chains]
        for k in range(5):
            for c in ctx:
                c['pw_next'] = _dot(c['pw'], blockdiag(c['pw'])).astype(BF16)
            if k > 0:
                for c in ctx:
                    c['inv'] = c['inv'] + _dot(c['inv'].astype(BF16), blockdiag(c['pw']))
            for c in ctx:
                c['pw'] = c['pw_next']
        for c in ctx:
            c['inv'] = c['inv'] + _dot(c['inv'].astype(BF16), blockdiag(c['pw']))
        for c in ctx:
            c['sol'] = _dot(blockdiag(c['inv'].astype(BF16)), c['rhs']).astype(BF16)
        for c in ctx:
            qo = _dot(c['qkd'], c['sol'])
            c['o0'] = qo[:, :LANES]
            c['qt'] = c['qe'] - qo[:, LANES:]
        out = []
        for c in ctx:
            rev, gc, gct = c['rev'], c['gc'], c['gct']
            per_chunk = []
            for cidx in ((1, 0) if rev else (0, 1)):
                lo = cidx * DN_CHUNK
                last = lo if rev else lo + DN_CHUNK - 1
                g_last = gc[last:last + 1, :]
                in_c = (ci // DN_CHUNK) == cidx
                kst = c['kt'] * jnp.where(in_c, jnp.exp(jnp.where(in_c, g_last - gct, 0.0)), 0.0)
                mn = _dot(kst.astype(BF16), c['sol'])
                lhs = jnp.concatenate([-mn[:, LANES:], c['qt'][lo:lo + DN_CHUNK]], axis=0).astype(BF16)
                per_chunk.append((cidx, lhs, mn[:, :LANES], jnp.exp(g_last), c['o0'][lo:lo + DN_CHUNK]))
            out.append((c['sl'], per_chunk))
        return out

    def scan(state, p1, rev):
        sl, per_chunk = p1
        outs = [None, None]
        for cidx, lhs, ku, e_last, o0_c in per_chunk:
            r = _dot(lhs, state.astype(BF16))
            outs[cidx] = r[LANES:] + o0_c
            state = e_last * state + r[:LANES] + ku
        dst = ob_s if rev else of_s
        dst[sl, :] = jnp.concatenate(outs, axis=0)
        return state

    def scan_step(i, carry):
        s_f, s_b = carry
        chains = []
        for u in range(DN_UNROLL):
            chains += [(i * DN_UNROLL + u, False), (n_super - 1 - (i * DN_UNROLL + u), True)]
        p1 = phase1(chains)
        p_f, p_b = p1[0::2], p1[1::2]
        for u in range(DN_UNROLL):
            s_f = scan(s_f, p_f[u], False)
            s_b = scan(s_b, p_b[u], True)
        return s_f, s_b

    z0 = jnp.zeros((DN_HD, DN_HD), F32)
    lax.fori_loop(0, n_super // DN_UNROLL, scan_step, (z0, z0))

    def fin(i, c):
        sl = pl.ds(pl.multiple_of(i * 512, 512), 512)
        o = of_s[sl, :] + ob_s[sl, :]
        o = o * lax.rsqrt(jnp.mean(o * o, axis=-1, keepdims=True) + EPS) * og_ref[...]
        zf = z_ref[0, sl, :].astype(F32)
        o_ref[0, sl, :] = (o * (zf * jax.nn.sigmoid(zf))).astype(o_ref.dtype)
        return c

    lax.fori_loop(0, s_len // 512, fin, 0)


def _deltanet(dqkv, dz, dab, conv_w, a_log, dt_bias, out_g):
    b, s_len, _ = dqkv.shape
    h = DN_HEADS
    ab = dab.reshape(b, s_len, 2, 2, h).transpose(0, 4, 2, 3, 1).reshape(b, h, 4, s_len)
    cw = conv_w.astype(F32)
    tok = lambda off: (lambda i, j: (i, 0, off + j))
    smem = pl.BlockSpec(memory_space=pltpu.SMEM)
    return pl.pallas_call(
        functools.partial(_dn_body, s_len=s_len),
        grid=(b, h),
        in_specs=[
            pl.BlockSpec((1, s_len, LANES), tok(0)),
            pl.BlockSpec((1, s_len, LANES), tok(h)),
            pl.BlockSpec((1, s_len, LANES), tok(2 * h)),
            pl.BlockSpec((1, s_len, LANES), tok(0)),
            pl.BlockSpec((1, 1, 4, s_len), lambda i, j: (i, j, 0, 0)),
            pl.BlockSpec((DN_CONV, LANES), lambda i, j: (0, j)),
            pl.BlockSpec((DN_CONV, LANES), lambda i, j: (0, h + j)),
            pl.BlockSpec((DN_CONV, LANES), lambda i, j: (0, 2 * h + j)),
            smem,
            smem,
            pl.BlockSpec((1, LANES), lambda i, j: (0, 0)),
        ],
        out_specs=pl.BlockSpec((1, s_len, LANES), tok(0)),
        out_shape=jax.ShapeDtypeStruct((b, s_len, DN_WIDTH), BF16),
        scratch_shapes=[
            pltpu.VMEM((3, s_len + 2 * HALO, LANES), F32),
            pltpu.VMEM((s_len, LANES), F32),
            pltpu.VMEM((s_len, LANES), F32),
            pltpu.VMEM((s_len, LANES), F32),
            pltpu.VMEM((s_len, LANES), F32),
            pltpu.VMEM((s_len, LANES), F32),
        ],
        compiler_params=_cparams(("arbitrary", "arbitrary")),
    )(dqkv, dqkv, dqkv, dz, ab, cw, cw, cw, a_log.astype(F32), dt_bias.astype(F32),
      out_g.reshape(1, LANES).astype(F32))


ROUTER_ROWS = 80
TOK_SUB = 8


def _to_token_tiles(x):
    blocks = jnp.stack([x[:, s * LANES:(s + 1) * LANES] for s in range(TOK_SUB)], axis=0)
    return pltpu.einshape("spl->psl", blocks)


def _from_token_tiles(t):
    blocks = pltpu.einshape("psl->spl", t)
    return jnp.concatenate([blocks[s] for s in range(TOK_SUB)], axis=1)


def _outproj_body(x_ref, at_ref, dn_ref, ag_ref, wo_ref, lg_ref, wrh_ref, wrl_ref, rb_ref, su_ref,
                  x1_ref, h2_ref, eid_ref, gate_ref, rank_ref, cnt_ref, base_s):
    step = pl.program_id(0)
    tm = x_ref.shape[0]

    @pl.when(step == 0)
    def _():
        base_s[...] = jnp.zeros_like(base_s)

    a = at_ref[...].astype(F32)
    an = a * lax.rsqrt(jnp.mean(a * a, axis=-1, keepdims=True) + EPS) * ag_ref[...]
    mix = jnp.concatenate([an.astype(BF16), dn_ref[...]], axis=1)
    x1 = x_ref[...] + _dot(mix, wo_ref[...])
    x1_ref[...] = x1
    h2 = x1 * lax.rsqrt(jnp.mean(x1 * x1, axis=-1, keepdims=True) + EPS) * lg_ref[...]
    h2_ref[...] = _to_token_tiles(h2)

    hi, lo = _split_hi_lo(h2)
    wrh = wrh_ref[...]
    lt = _dot_nt(wrh, hi) + _dot_nt(wrh, lo) + _dot_nt(wrl_ref[...], hi) + rb_ref[...]

    sub8 = lax.broadcasted_iota(I32, (N_GROUPS, tm), 0)
    gl = lt[0:N_GROUPS]
    gmax = jnp.max(gl, axis=0, keepdims=True)
    gsel = jnp.min(jnp.where(gl == gmax, sub8, N_GROUPS), axis=0, keepdims=True)
    gp = 1.0 / jnp.sum(jnp.exp(gl - gmax), axis=0, keepdims=True)
    in_group = jnp.zeros((EPG, tm), F32)
    for g in range(N_GROUPS):
        in_group = in_group + jnp.where(gsel == g, lt[N_GROUPS + g * EPG:N_GROUPS + (g + 1) * EPG], 0.0)
    v1 = jnp.max(in_group, axis=0, keepdims=True)
    i1 = jnp.min(jnp.where(in_group == v1, sub8, EPG), axis=0, keepdims=True)
    rest = jnp.where(sub8 == i1, -jnp.inf, in_group)
    v2 = jnp.max(rest, axis=0, keepdims=True)
    i2 = jnp.min(jnp.where(rest == v2, sub8, EPG), axis=0, keepdims=True)
    e21 = jnp.exp(v2 - v1)
    den = 1.0 + e21
    eid1 = gsel * EPG + i1
    eid2 = gsel * EPG + i2
    eid_ref[0:1, :] = eid1
    eid_ref[1:2, :] = eid2
    gate_ref[0:1, :] = gp * (1.0 / den)
    gate_ref[1:2, :] = gp * (e21 / den)

    sub64 = lax.broadcasted_iota(I32, (N_EXPERTS, tm), 0)
    oh1 = sub64 == eid1
    oh2 = sub64 == eid2
    ohs = jnp.where(oh1 | oh2, 1.0, 0.0)
    before = base_s[...] + _dot(ohs.astype(BF16), su_ref[...])
    rank_ref[0:1, :] = jnp.sum(jnp.where(oh1, before, 0.0), axis=0, keepdims=True).astype(I32)
    rank_ref[1:2, :] = jnp.sum(jnp.where(oh2, before, 0.0), axis=0, keepdims=True).astype(I32)
    base_new = base_s[...] + jnp.sum(ohs, axis=1, keepdims=True)
    base_s[...] = base_new
    cnt_ref[...] = jnp.broadcast_to(base_new, cnt_ref.shape)


def _outproj_router(x2, attn_raw, dn, attn_g, w_out, ln2_g, wg_r, bg_r, we_r, be_r, tm=512):
    n, d = x2.shape
    wr = jnp.concatenate([wg_r, we_r], axis=1).T.astype(F32)
    wr = jnp.pad(wr, ((0, ROUTER_ROWS - wr.shape[0]), (0, 0)))
    wrh, wrl = _split_hi_lo(wr)
    rb = jnp.pad(jnp.concatenate([bg_r, be_r]).astype(F32), (0, ROUTER_ROWS - N_GROUPS - N_EXPERTS))
    rb = rb.reshape(ROUTER_ROWS, 1)
    t_i = jnp.arange(tm)
    su = (t_i[:, None] < t_i[None, :]).astype(BF16)
    const = lambda i: (0, 0)
    row = lambda i: (i, 0)
    col = lambda i: (0, i)
    return pl.pallas_call(
        _outproj_body,
        grid=(n // tm,),
        in_specs=[
            pl.BlockSpec((tm, d), row),
            pl.BlockSpec((tm, ATTN_WIDTH), row),
            pl.BlockSpec((tm, DN_WIDTH), row),
            pl.BlockSpec((1, ATTN_WIDTH), const),
            pl.BlockSpec((ATTN_WIDTH + DN_WIDTH, d), const),
            pl.BlockSpec((1, d), const),
            pl.BlockSpec((ROUTER_ROWS, d), const),
            pl.BlockSpec((ROUTER_ROWS, d), const),
            pl.BlockSpec((ROUTER_ROWS, 1), const),
            pl.BlockSpec((tm, tm), const),
        ],
        out_specs=[
            pl.BlockSpec((tm, d), row),
            pl.BlockSpec((tm, TOK_SUB, LANES), lambda i: (i, 0, 0)),
            pl.BlockSpec((2, tm), col),
            pl.BlockSpec((2, tm), col),
            pl.BlockSpec((2, tm), col),
            pl.BlockSpec((N_EXPERTS, LANES), const),
        ],
        out_shape=[
            jax.ShapeDtypeStruct((n, d), F32),
            jax.ShapeDtypeStruct((n, TOK_SUB, LANES), F32),
            jax.ShapeDtypeStruct((2, n), I32),
            jax.ShapeDtypeStruct((2, n), F32),
            jax.ShapeDtypeStruct((2, n), I32),
            jax.ShapeDtypeStruct((N_EXPERTS, LANES), F32),
        ],
        scratch_shapes=[pltpu.VMEM((N_EXPERTS, 1), F32)],
        compiler_params=_cparams(("arbitrary",)),
    )(x2, attn_raw, dn, attn_g.reshape(1, -1).astype(F32), w_out.astype(BF16), ln2_g.reshape(1, d).astype(F32),
      wrh, wrl, rb, su)


def _dispatch_body(dest_ref, zrow_ref, h_ref, xs_hbm, zbuf, zsem, sem, *, n):
    tm = h_ref.shape[0]
    base = pl.program_id(0) * tm

    @pl.when(pl.program_id(0) == 0)
    def _():
        zbuf[...] = jnp.zeros_like(zbuf)

        def zero_copy(e):
            return pltpu.make_async_copy(zbuf, xs_hbm.at[pl.ds(zrow_ref[e], MOE_BLOCK)], zsem)

        def start(e, c):
            @pl.when(zrow_ref[e] >= 0)
            def _():
                zero_copy(e).start()
            return c

        def finish(e, c):
            @pl.when(zrow_ref[e] >= 0)
            def _():
                zero_copy(e).wait()
            return c

        lax.fori_loop(0, N_EXPERTS, start, 0)
        lax.fori_loop(0, N_EXPERTS, finish, 0)

        def tail_copy(b):
            return pltpu.make_async_copy(zbuf, xs_hbm.at[pl.ds(b * MOE_BLOCK, MOE_BLOCK)], zsem)

        def tail_start(b, c):
            tail_copy(b).start()
            return c

        def tail_finish(b, c):
            tail_copy(b).wait()
            return c

        n_blocks = xs_hbm.shape[0] // MOE_BLOCK
        lax.fori_loop(zrow_ref[N_EXPERTS], n_blocks, tail_start, 0)
        lax.fori_loop(zrow_ref[N_EXPERTS], n_blocks, tail_finish, 0)

    def issue(j, c):
        src = h_ref.at[pl.ds(j, 1)]
        pltpu.make_async_copy(src, xs_hbm.at[pl.ds(dest_ref[base + j], 1)], sem).start()
        pltpu.make_async_copy(src, xs_hbm.at[pl.ds(dest_ref[n + base + j], 1)], sem).start()
        return c

    lax.fori_loop(0, tm, issue, 0, unroll=8)
    for _ in range(2):
        pltpu.make_async_copy(h_ref, xs_hbm.at[pl.ds(0, tm)], sem).wait()


def _dispatch(h2t, dest, zrow, p_rows, tm=1024):
    n = h2t.shape[0]
    grid_spec = pltpu.PrefetchScalarGridSpec(
        num_scalar_prefetch=2,
        grid=(n // tm,),
        in_specs=[pl.BlockSpec((tm, TOK_SUB, LANES), lambda i, dst, zr: (i, 0, 0))],
        out_specs=pl.BlockSpec(memory_space=pl.ANY),
        scratch_shapes=[pltpu.VMEM((MOE_BLOCK, TOK_SUB, LANES), F32), pltpu.SemaphoreType.DMA(()),
                        pltpu.SemaphoreType.DMA(())],
    )
    return pl.pallas_call(
        functools.partial(_dispatch_body, n=n),
        grid_spec=grid_spec,
        out_shape=jax.ShapeDtypeStruct((p_rows, TOK_SUB, LANES), F32),
        compiler_params=_cparams(("arbitrary",)),
    )(dest.reshape(-1), zrow, h2t)


EXPERT_STEP_BLOCKS = 4


def _expert_body(be_ref, nu_ref, xs_ref, *refs):
    w_refs, o_ref = refs[:-1], refs[-1]
    for h in range(EXPERT_STEP_BLOCKS):
        wg_ref, wu_ref, wd_ref = w_refs[3 * h:3 * h + 3]
        rows = pl.ds(h * MOE_BLOCK, MOE_BLOCK)
        used = pl.program_id(0) * EXPERT_STEP_BLOCKS + h < nu_ref[0]

        @pl.when(used)
        def _():
            x = _from_token_tiles(xs_ref[rows]).astype(BF16)
            gt = _dot(x, wg_ref[0])
            up = _dot(x, wu_ref[0])
            hid = gt * jax.nn.sigmoid(gt) * up
            out = _dot(hid.astype(BF16), wd_ref[0])
            o_ref[rows] = _to_token_tiles(out)

        @pl.when(jnp.logical_not(used))
        def _():
            o_ref[rows] = jnp.zeros((MOE_BLOCK, TOK_SUB, LANES), o_ref.dtype)


def _experts(xs, block_expert, n_used, w_gate, w_up, w_down):
    p_rows = xs.shape[0]
    step_rows = EXPERT_STEP_BLOCKS * MOE_BLOCK
    d, de = w_gate.shape[-2:]
    wmap = lambda h: (lambda i, be, nu: (be[i * EXPERT_STEP_BLOCKS + h], 0, 0))
    xmap = lambda i, be, nu: (jnp.minimum(i, (nu[0] - 1) // EXPERT_STEP_BLOCKS), 0, 0)
    w_specs = []
    for h in range(EXPERT_STEP_BLOCKS):
        w_specs += [pl.BlockSpec((1, d, de), wmap(h)), pl.BlockSpec((1, d, de), wmap(h)),
                    pl.BlockSpec((1, de, d), wmap(h))]
    grid_spec = pltpu.PrefetchScalarGridSpec(
        num_scalar_prefetch=2,
        grid=(p_rows // step_rows,),
        in_specs=[pl.BlockSpec((step_rows, TOK_SUB, LANES), xmap)] + w_specs,
        out_specs=pl.BlockSpec((step_rows, TOK_SUB, LANES), lambda i, be, nu: (i, 0, 0)),
    )
    wg, wu, wd = w_gate.astype(BF16), w_up.astype(BF16), w_down.astype(BF16)
    return pl.pallas_call(
        _expert_body,
        grid_spec=grid_spec,
        out_shape=jax.ShapeDtypeStruct((p_rows, TOK_SUB, LANES), F32),
        compiler_params=_cparams(("arbitrary",)),
    )(block_expert, n_used, xs, *([wg, wu, wd] * EXPERT_STEP_BLOCKS))


def _combine_body(dest_ref, x1_ref, gt_ref, eo_hbm, out_ref, rbuf, sem, *, n):
    tm = x1_ref.shape[0]
    step = pl.program_id(0)
    slot = step % 2

    def gather(tile, slot_):
        base = tile * tm

        def issue(j, c):
            d0 = dest_ref[base + j]
            d1 = dest_ref[n + base + j]
            pltpu.make_async_copy(eo_hbm.at[pl.ds(d0, 1)], rbuf.at[slot_, 0, pl.ds(j, 1)], sem.at[slot_]).start()
            pltpu.make_async_copy(eo_hbm.at[pl.ds(d1, 1)], rbuf.at[slot_, 1, pl.ds(j, 1)], sem.at[slot_]).start()
            return c

        lax.fori_loop(0, tm, issue, 0, unroll=8)

    @pl.when(step == 0)
    def _():
        gather(0, 0)

    @pl.when(step + 1 < pl.num_programs(0))
    def _():
        gather(step + 1, 1 - slot)

    for k in range(2):
        pltpu.make_async_copy(eo_hbm.at[pl.ds(0, tm)], rbuf.at[slot, k], sem.at[slot]).wait()
    g0 = jnp.broadcast_to(gt_ref[:, 0:1], (tm, LANES))
    g1 = jnp.broadcast_to(gt_ref[:, 1:2], (tm, LANES))
    r0 = pltpu.einshape("psl->spl", rbuf[slot, 0])
    r1 = pltpu.einshape("psl->spl", rbuf[slot, 1])
    for s in range(TOK_SUB):
        cols = slice(s * LANES, (s + 1) * LANES)
        out_ref[:, cols] = x1_ref[:, cols] + (g0 * r0[s] + g1 * r1[s])


def _combine(x1, gates_t, dest, eo, tm=256):
    n, d = x1.shape
    grid_spec = pltpu.PrefetchScalarGridSpec(
        num_scalar_prefetch=1,
        grid=(n // tm,),
        in_specs=[
            pl.BlockSpec((tm, d), lambda i, dst: (i, 0)),
            pl.BlockSpec((tm, 2), lambda i, dst: (i, 0)),
            pl.BlockSpec(memory_space=pl.ANY),
        ],
        out_specs=pl.BlockSpec((tm, d), lambda i, dst: (i, 0)),
        scratch_shapes=[pltpu.VMEM((2, 2, tm, TOK_SUB, LANES), F32), pltpu.SemaphoreType.DMA((2,))],
    )
    return pl.pallas_call(
        functools.partial(_combine_body, n=n),
        grid_spec=grid_spec,
        out_shape=jax.ShapeDtypeStruct((n, d), F32),
        compiler_params=_cparams(("arbitrary",)),
    )(dest.reshape(-1), x1, gates_t, eo)


def _dispatch_plan(eid, rank, counts, n):
    nk = 2 * n
    p_rows = -(-nk // MOE_BLOCK) * MOE_BLOCK + N_EXPERTS * MOE_BLOCK
    nb = p_rows // MOE_BLOCK
    padded = -(-counts // MOE_BLOCK) * MOE_BLOCK
    pends = jnp.cumsum(padded)
    pstarts = pends - padded
    experts = jnp.arange(N_EXPERTS, dtype=I32)
    dest = jnp.sum(jnp.where(eid[..., None] == experts, pstarts.astype(I32), 0), axis=-1) + rank
    block_start = jnp.arange(nb, dtype=I32) * MOE_BLOCK
    block_expert = jnp.clip(jnp.sum((pends[None, :] <= block_start[:, None]).astype(I32), axis=-1),
                            0, N_EXPERTS - 1).astype(I32)
    n_used = (pends[-1] // MOE_BLOCK).astype(I32).reshape(1)
    zrow = jnp.concatenate([jnp.where(padded > 0, pends - MOE_BLOCK, -1).astype(I32), n_used])
    return dest.astype(I32), zrow, block_expert, n_used, p_rows


def _layer(x, ln1_g, w_in, q_g, k_g, rpb, attn_g, conv_w, a_log, dt_bias, dn_g, w_out, ln2_g,
           wg_r, bg_r, we_r, be_r, w_gate, w_up, w_down):
    b, s_len, d = x.shape
    n = b * s_len
    x2 = x.reshape(n, d)
    aqkv, dqkv, dz, dab = _inproj(x2, ln1_g, w_in)
    attn_raw = _attention(aqkv.reshape(b, s_len, -1), q_g, k_g, rpb)
    dn = _deltanet(dqkv.reshape(b, s_len, -1), dz.reshape(b, s_len, -1), dab.reshape(b, s_len, -1),
                   conv_w, a_log, dt_bias, dn_g)
    x1, h2t, eid, gates, rank, cnt = _outproj_router(
        x2, attn_raw.reshape(n, -1), dn.reshape(n, -1), attn_g, w_out, ln2_g, wg_r, bg_r, we_r, be_r)
    counts = cnt[:, 0].astype(I32)
    dest, zrow, block_expert, n_used, p_rows = _dispatch_plan(eid, rank, counts, n)
    xs = _dispatch(h2t, dest, zrow, p_rows)
    eo = _experts(xs, block_expert, n_used, w_gate, w_up, w_down)
    out = _combine(x1, gates.T, dest, eo)
    return out.reshape(b, s_len, d)


def kernel(x, ln1_g, w_in, attn_q_norm_g, attn_k_norm_g, attn_rpb, attn_out_norm_g, dn_conv_w, dn_a_log,
           dn_dt_bias, dn_out_norm_g, w_out, ln2_g, router_group_w, router_group_b, router_expert_w,
           router_expert_b, expert_w_gate, expert_w_up, expert_w_down):
    for l in range(ln1_g.shape[0]):
        x = _layer(x, ln1_g[l], w_in[l], attn_q_norm_g[l], attn_k_norm_g[l], attn_rpb[l],
                   attn_out_norm_g[l], dn_conv_w[l], dn_a_log[l], dn_dt_bias[l], dn_out_norm_g[l],
                   w_out[l], ln2_g[l], router_group_w[l], router_group_b[l], router_expert_w[l],
                   router_expert_b[l], expert_w_gate[l], expert_w_up[l], expert_w_down[l])
    return x
```

```python
import functools

import jax
import jax.numpy as jnp
from jax import lax
from jax.experimental import pallas as pl
from jax.experimental.pallas import tpu as pltpu

F32 = jnp.float32
BF16 = jnp.bfloat16
I32 = jnp.int32

EPS = 1e-6
GRID_W = 64
WIN_H = 8
WIN_W = 16
ATTN_HEADS = 8
ATTN_HD = 64
ATTN_WIDTH = ATTN_HEADS * ATTN_HD
DN_HEADS = 4
DN_HD = 128
DN_WIDTH = DN_HEADS * DN_HD
DN_CONV = 5
DN_CHUNK = 64
N_GROUPS = 8
EPG = 8
N_EXPERTS = N_GROUPS * EPG
MOE_BLOCK = 256
NEG = -1e30

ATTN_UNROLL = 8
LANES = 128
VMEM_LIMIT = 56 * 1024 * 1024


def _cparams(sem):
    return pltpu.CompilerParams(dimension_semantics=sem, vmem_limit_bytes=VMEM_LIMIT)


def _dot(a, b):
    return jnp.dot(a, b, preferred_element_type=F32)


def _dot_nt(a, b):
    return lax.dot_general(a, b, (((1,), (1,)), ((), ())), preferred_element_type=F32)


def _split_hi_lo(x):
    hi = x.astype(BF16)
    lo = (x - hi.astype(F32)).astype(BF16)
    return hi, lo


def _inproj_body(x_ref, g_ref, wa_ref, wd_ref, wz_ref, wab_ref, oa_ref, od_ref, oz_ref, oab_ref):
    x = x_ref[...]
    ms = jnp.mean(x * x, axis=-1, keepdims=True)
    h = x * lax.rsqrt(ms + EPS) * g_ref[...]
    hi, lo = _split_hi_lo(h)
    oa_ref[...] = _dot(hi, wa_ref[...]).astype(BF16)
    od_ref[...] = _dot(hi, wd_ref[...]).astype(BF16)
    oz_ref[...] = _dot(hi, wz_ref[...]).astype(BF16)
    nab = oab_ref.shape[-1]
    ab = _dot(hi, wab_ref[...])
    ab_lo = _dot(lo, wab_ref[:, :nab])
    oab_ref[...] = ab[:, :nab] + ab[:, nab:] + ab_lo


def _inproj(x2, ln1_g, w_in, tm=512):
    n, d = x2.shape
    a3 = 3 * ATTN_WIDTH
    d3 = 3 * DN_WIDTH
    wa = w_in[:, :a3].astype(BF16)
    wd = w_in[:, a3:a3 + d3].astype(BF16)
    wz = w_in[:, a3 + d3:a3 + d3 + DN_WIDTH].astype(BF16)
    wab_f = w_in[:, a3 + d3 + DN_WIDTH:]
    nab = wab_f.shape[1]
    wab_hi, wab_lo = _split_hi_lo(wab_f)
    wab = jnp.concatenate([wab_hi, wab_lo], axis=1)
    const = lambda i: (0, 0)
    row = lambda i: (i, 0)
    return pl.pallas_call(
        _inproj_body,
        grid=(n // tm,),
        in_specs=[
            pl.BlockSpec((tm, d), row),
            pl.BlockSpec((1, d), const),
            pl.BlockSpec((d, a3), const),
            pl.BlockSpec((d, d3), const),
            pl.BlockSpec((d, DN_WIDTH), const),
            pl.BlockSpec((d, 2 * nab), const),
        ],
        out_specs=[
            pl.BlockSpec((tm, a3), row),
            pl.BlockSpec((tm, d3), row),
            pl.BlockSpec((tm, DN_WIDTH), row),
            pl.BlockSpec((tm, nab), row),
        ],
        out_shape=[
            jax.ShapeDtypeStruct((n, a3), BF16),
            jax.ShapeDtypeStruct((n, d3), BF16),
            jax.ShapeDtypeStruct((n, DN_WIDTH), BF16),
            jax.ShapeDtypeStruct((n, nab), F32),
        ],
        compiler_params=_cparams(("parallel",)),
    )(x2, ln1_g.reshape(1, d), wa, wd, wz, wab)


def _attn_bias_table(rpb):
    c = jnp.arange(GRID_W)
    c0 = jnp.clip(c - WIN_W // 2, 0, GRID_W - WIN_W)
    kc = jnp.arange(GRID_W)
    inwin = (kc[None, :] >= c0[:, None]) & (kc[None, :] < c0[:, None] + WIN_W)
    h = rpb.shape[0]
    padw = GRID_W - WIN_W
    rp = jnp.pad(rpb.astype(F32), ((0, 0), (0, 0), (padw, padw)))
    by_col = jnp.stack([rp[:, :, GRID_W - 1 - cc:2 * GRID_W - 1 - cc] for cc in range(GRID_W)], axis=2)
    tab = jnp.stack([by_col[:, WIN_H - 1 - dd:2 * WIN_H - 1 - dd] for dd in range(WIN_H)], axis=1)
    tab = tab.transpose(0, 1, 3, 2, 4)
    tab = jnp.where(inwin[None, None, :, None, :], tab, NEG)
    tab = tab.reshape(h // 2, 2, WIN_H, GRID_W, WIN_H * GRID_W)
    return tab.transpose(0, 2, 1, 3, 4).reshape(h // 2, WIN_H, 2 * GRID_W, WIN_H * GRID_W)


def _attn_body(q_ref, k_ref, v_ref, qg_ref, kg_ref, bias_ref, o_ref, qlo_s, qhi_s, kn_s, *, rows):
    s_len = rows * GRID_W
    lane = lax.broadcasted_iota(I32, (1, LANES), 1)
    is_lo = lane < ATTN_HD
    blk = 256

    hr = lax.broadcasted_iota(I32, (LANES, LANES), 0) // ATTN_HD
    hc = lax.broadcasted_iota(I32, (LANES, LANES), 1) // ATTN_HD
    same_head = jnp.where(hr == hc, 1.0, 0.0).astype(BF16)

    def head_norm(x, gain):
        ms = _dot((x * x).astype(BF16), same_head) * (1.0 / ATTN_HD)
        return x * lax.rsqrt(ms + EPS) * gain

    def prep(i, c):
        sl = pl.ds(pl.multiple_of(i * blk, blk), blk)
        qn = head_norm(q_ref[0, sl, :].astype(F32), qg_ref[...]) * (ATTN_HD ** -0.5)
        qlo_s[sl, :] = jnp.where(is_lo, qn, 0.0).astype(BF16)
        qhi_s[sl, :] = jnp.where(is_lo, 0.0, qn).astype(BF16)
        kn_s[sl, :] = head_norm(k_ref[0, sl, :].astype(F32), kg_ref[...]).astype(BF16)
        return c

    lax.fori_loop(0, s_len // blk, prep, 0)

    band = WIN_H * GRID_W

    def row_step(i, c):
        ctx = []
        for u in range(ATTN_UNROLL):
            r = i * ATTN_UNROLL + u
            r0 = jnp.clip(r - WIN_H // 2, 0, rows - WIN_H)
            qsl = pl.ds(pl.multiple_of(r * GRID_W, GRID_W), GRID_W)
            ksl = pl.ds(pl.multiple_of(r0 * GRID_W, GRID_W), band)
            q2 = jnp.concatenate([qlo_s[qsl, :], qhi_s[qsl, :]], axis=0)
            ctx.append(dict(d=r - r0, qsl=qsl, ksl=ksl, s=_dot_nt(q2, kn_s[ksl, :])))
        for c_ in ctx:
            s = c_['s'] + bias_ref[0, c_['d']]
            m = jnp.max(s, axis=-1, keepdims=True)
            p = jnp.exp(s - m)
            c_['l'] = jnp.sum(p, axis=-1, keepdims=True)
            c_['p'] = p.astype(BF16)
        for c_ in ctx:
            c_['pv'] = _dot(c_['p'], v_ref[0, c_['ksl'], :])
        for c_ in ctx:
            pv = c_['pv'] / c_['l']
            o = jnp.where(is_lo, pv[:GRID_W], pv[GRID_W:])
            o_ref[0, c_['qsl'], :] = o.astype(o_ref.dtype)
        return c

    lax.fori_loop(0, rows // ATTN_UNROLL, row_step, 0)


def _attention(aqkv, q_g, k_g, rpb):
    b, s_len, _ = aqkv.shape
    rows = s_len // GRID_W
    assert min(WIN_H, rows) == WIN_H
    pairs = ATTN_HEADS // 2
    bias = _attn_bias_table(rpb)
    qg2 = jnp.tile(q_g, 2).reshape(1, LANES).astype(F32)
    kg2 = jnp.tile(k_g, 2).reshape(1, LANES).astype(F32)
    band = WIN_H * GRID_W
    return pl.pallas_call(
        functools.partial(_attn_body, rows=rows),
        grid=(pairs, b),
        in_specs=[
            pl.BlockSpec((1, s_len, LANES), lambda p, i: (i, 0, p)),
            pl.BlockSpec((1, s_len, LANES), lambda p, i: (i, 0, pairs + p)),
            pl.BlockSpec((1, s_len, LANES), lambda p, i: (i, 0, 2 * pairs + p)),
            pl.BlockSpec((1, LANES), lambda p, i: (0, 0)),
            pl.BlockSpec((1, LANES), lambda p, i: (0, 0)),
            pl.BlockSpec((1, WIN_H, 2 * GRID_W, band), lambda p, i: (p, 0, 0, 0)),
        ],
        out_specs=pl.BlockSpec((1, s_len, LANES), lambda p, i: (i, 0, p)),
        out_shape=jax.ShapeDtypeStruct((b, s_len, ATTN_WIDTH), BF16),
        scratch_shapes=[
            pltpu.VMEM((s_len, LANES), BF16),
            pltpu.VMEM((s_len, LANES), BF16),
            pltpu.VMEM((s_len, LANES), BF16),
        ],
        compiler_params=_cparams(("arbitrary", "arbitrary")),
    )(aqkv, aqkv, aqkv, qg2, kg2, bias)


SUPER = 2 * DN_CHUNK
HALO = 16
DN_UNROLL = 8


def _dn_body(q_ref, k_ref, v_ref, z_ref, ab_ref, cwq_ref, cwk_ref, cwv_ref, alog_ref, dtb_ref, og_ref,
             o_ref, pad_s, qs, ks, vs, of_s, ob_s, *, s_len):
    head = pl.program_id(1)
    n_super = s_len // SUPER

    zeros_halo = jnp.zeros((HALO, LANES), F32)
    for t, src in enumerate((q_ref, k_ref, v_ref)):
        pad_s[t, 0:HALO, :] = zeros_halo
        pad_s[t, HALO + s_len:2 * HALO + s_len, :] = zeros_halo

        def stage(i, c, t=t, src=src):
            off = pl.multiple_of(i * 512, 512)
            pad_s[t, pl.ds(HALO + off, 512), :] = src[0, pl.ds(off, 512), :].astype(F32)
            return c

        lax.fori_loop(0, s_len // 512, stage, 0)


    def conv_step(i, c):
        t0 = pl.multiple_of(i * SUPER, SUPER)
        outs = []
        for t, cw_ref in enumerate((cwq_ref, cwk_ref, cwv_ref)):
            acc = jnp.zeros((SUPER, LANES), F32)
            for j in range(DN_CONV):
                xj = pad_s[t, pl.ds(t0 + (HALO - DN_CONV // 2 + j), SUPER), :]
                acc = acc + xj * cw_ref[j:j + 1, :]
            outs.append(acc * jax.nn.sigmoid(acc))
        qc, kc, vc = outs
        sl = pl.ds(t0, SUPER)
        qs[sl, :] = qc * lax.rsqrt(jnp.sum(qc * qc, axis=-1, keepdims=True) + EPS) * (DN_HD ** -0.5)
        ks[sl, :] = kc * lax.rsqrt(jnp.sum(kc * kc, axis=-1, keepdims=True) + EPS)
        vs[sl, :] = vc
        return c

    lax.fori_loop(0, n_super, conv_step, 0)

    ri = lax.broadcasted_iota(I32, (SUPER, SUPER), 0)
    ci = lax.broadcasted_iota(I32, (SUPER, SUPER), 1)
    same = (ri // DN_CHUNK) == (ci // DN_CHUNK)
    pr = lax.broadcasted_iota(I32, (DN_CHUNK, SUPER), 0)
    pc = lax.broadcasted_iota(I32, (DN_CHUNK, SUPER), 1)
    eye_packed = ((pc % DN_CHUNK) == pr).astype(F32)
    first_blk = pc < DN_CHUNK

    def blockdiag(p):
        zero = jnp.zeros_like(p)
        return jnp.concatenate([jnp.where(first_blk, p, zero), jnp.where(first_blk, zero, p)], axis=0)
    posl = lax.broadcasted_iota(I32, (8, LANES), 1) % DN_CHUNK

    def prep(sc, rev):
        t0 = pl.multiple_of(sc * SUPER, SUPER)
        sl = pl.ds(t0, SUPER)
        q = qs[sl, :]
        k = ks[sl, :]
        v = vs[sl, :]
        dcol = 1 if rev else 0
        a_r = jnp.broadcast_to(ab_ref[0, 0, dcol:dcol + 1, sl], (8, LANES))
        b_r = jnp.broadcast_to(ab_ref[0, 0, 2 + dcol:3 + dcol, sl], (8, LANES))
        neg_rate = -jnp.exp(jnp.full((1, LANES), alog_ref[dcol, head], F32))
        g_r = neg_rate * jax.nn.softplus(a_r + dtb_ref[dcol, head])
        beta_r = jax.nn.sigmoid(b_r)
        gc_r = g_r
        for sft in (1, 2, 4, 8, 16, 32):
            if rev:
                gc_r = gc_r + jnp.where(posl < DN_CHUNK - sft, pltpu.roll(gc_r, SUPER - sft, axis=1), 0.0)
            else:
                gc_r = gc_r + jnp.where(posl >= sft, pltpu.roll(gc_r, sft, axis=1), 0.0)
        gct = jnp.broadcast_to(gc_r[0:1], (SUPER, LANES))
        gc = gct.T
        beta = jnp.broadcast_to(beta_r[0:1], (SUPER, LANES)).T
        diff = gc - gct
        if rev:
            incl = same & (ri <= ci)
            strict = same & (ri < ci)
        else:
            incl = same & (ri >= ci)
            strict = same & (ri > ci)
        decay = jnp.where(incl, jnp.exp(jnp.where(incl, diff, 0.0)), 0.0)
        kb = k * beta
        gram = _dot_nt(jnp.concatenate([q, kb], axis=0).astype(BF16), k.astype(BF16))
        qkd = (gram[:SUPER] * decay).astype(BF16)
        low = jnp.where(strict, gram[SUPER:] * decay, 0.0)
        egc = jnp.exp(gc)
        rhs = jnp.concatenate([v * beta, kb * egc], axis=1).astype(BF16)
        low_packed = low[:DN_CHUNK] + low[DN_CHUNK:]
        return dict(sl=sl, rev=rev, gc=gc, gct=gct, qkd=qkd, rhs=rhs, qe=q * egc, kt=k.T,
                    inv=eye_packed - low_packed, pw=low_packed.astype(BF16))

    def phase1(chains):
        ctx = [prep(sc, rev) for sc, rev in chains]
        for k in range(5):
            for c in ctx:
                c['pw_next'] = _dot(c['pw'], blockdiag(c['pw'])).astype(BF16)
            if k > 0:
                for c in ctx:
                    c['inv'] = c['inv'] + _dot(c['inv'].astype(BF16), blockdiag(c['pw']))
            for c in ctx:
                c['pw'] = c['pw_next']
        for c in ctx:
            c['inv'] = c['inv'] + _dot(c['inv'].astype(BF16), blockdiag(c['pw']))
        for c in ctx:
            c['sol'] = _dot(blockdiag(c['inv'].astype(BF16)), c['rhs']).astype(BF16)
        for c in ctx:
            qo = _dot(c['qkd'], c['sol'])
            c['o0'] = qo[:, :LANES]
            c['qt'] = c['qe'] - qo[:, LANES:]
        out = []
        for c in ctx:
            rev, gc, gct = c['rev'], c['gc'], c['gct']
            per_chunk = []
            for cidx in ((1, 0) if rev else (0, 1)):
                lo = cidx * DN_CHUNK
                last = lo if rev else lo + DN_CHUNK - 1
                g_last = gc[last:last + 1, :]
                in_c = (ci // DN_CHUNK) == cidx
                kst = c['kt'] * jnp.where(in_c, jnp.exp(jnp.where(in_c, g_last - gct, 0.0)), 0.0)
                mn = _dot(kst.astype(BF16), c['sol'])
                lhs = jnp.concatenate([-mn[:, LANES:], c['qt'][lo:lo + DN_CHUNK]], axis=0).astype(BF16)
                per_chunk.append((cidx, lhs, mn[:, :LANES], jnp.exp(g_last), c['o0'][lo:lo + DN_CHUNK]))
            out.append((c['sl'], per_chunk))
        return out

    def scan(state, p1, rev):
        sl, per_chunk = p1
        outs = [None, None]
        for cidx, lhs, ku, e_last, o0_c in per_chunk:
            r = _dot(lhs, state.astype(BF16))
            outs[cidx] = r[LANES:] + o0_c
            state = e_last * state + r[:LANES] + ku
        dst = ob_s if rev else of_s
        dst[sl, :] = jnp.concatenate(outs, axis=0)
        return state

    def scan_step(i, carry):
        s_f, s_b = carry
        chains = []
        for u in range(DN_UNROLL):
            chains += [(i * DN_UNROLL + u, False), (n_super - 1 - (i * DN_UNROLL + u), True)]
        p1 = phase1(chains)
        p_f, p_b = p1[0::2], p1[1::2]
        for u in range(DN_UNROLL):
            s_f = scan(s_f, p_f[u], False)
            s_b = scan(s_b, p_b[u], True)
        return s_f, s_b

    z0 = jnp.zeros((DN_HD, DN_HD), F32)
    lax.fori_loop(0, n_super // DN_UNROLL, scan_step, (z0, z0))

    def fin(i, c):
        sl = pl.ds(pl.multiple_of(i * 512, 512), 512)
        o = of_s[sl, :] + ob_s[sl, :]
        o = o * lax.rsqrt(jnp.mean(o * o, axis=-1, keepdims=True) + EPS) * og_ref[...]
        zf = z_ref[0, sl, :].astype(F32)
        o_ref[0, sl, :] = (o * (zf * jax.nn.sigmoid(zf))).astype(o_ref.dtype)
        return c

    lax.fori_loop(0, s_len // 512, fin, 0)


def _deltanet(dqkv, dz, dab, conv_w, a_log, dt_bias, out_g):
    b, s_len, _ = dqkv.shape
    h = DN_HEADS
    ab = dab.reshape(b, s_len, 2, 2, h).transpose(0, 4, 2, 3, 1).reshape(b, h, 4, s_len)
    cw = conv_w.astype(F32)
    tok = lambda off: (lambda i, j: (i, 0, off + j))
    smem = pl.BlockSpec(memory_space=pltpu.SMEM)
    return pl.pallas_call(
        functools.partial(_dn_body, s_len=s_len),
        grid=(b, h),
        in_specs=[
            pl.BlockSpec((1, s_len, LANES), tok(0)),
            pl.BlockSpec((1, s_len, LANES), tok(h)),
            pl.BlockSpec((1, s_len, LANES), tok(2 * h)),
            pl.BlockSpec((1, s_len, LANES), tok(0)),
            pl.BlockSpec((1, 1, 4, s_len), lambda i, j: (i, j, 0, 0)),
            pl.BlockSpec((DN_CONV, LANES), lambda i, j: (0, j)),
            pl.BlockSpec((DN_CONV, LANES), lambda i, j: (0, h + j)),
            pl.BlockSpec((DN_CONV, LANES), lambda i, j: (0, 2 * h + j)),
            smem,
            smem,
            pl.BlockSpec((1, LANES), lambda i, j: (0, 0)),
        ],
        out_specs=pl.BlockSpec((1, s_len, LANES), tok(0)),
        out_shape=jax.ShapeDtypeStruct((b, s_len, DN_WIDTH), BF16),
        scratch_shapes=[
            pltpu.VMEM((3, s_len + 2 * HALO, LANES), F32),
            pltpu.VMEM((s_len, LANES), F32),
            pltpu.VMEM((s_len, LANES), F32),
            pltpu.VMEM((s_len, LANES), F32),
            pltpu.VMEM((s_len, LANES), F32),
            pltpu.VMEM((s_len, LANES), F32),
        ],
        compiler_params=_cparams(("arbitrary", "arbitrary")),
    )(dqkv, dqkv, dqkv, dz, ab, cw, cw, cw, a_log.astype(F32), dt_bias.astype(F32),
      out_g.reshape(1, LANES).astype(F32))


ROUTER_ROWS = 80
TOK_SUB = 8


def _to_token_tiles(x):
    blocks = jnp.stack([x[:, s * LANES:(s + 1) * LANES] for s in range(TOK_SUB)], axis=0)
    return pltpu.einshape("spl->psl", blocks)


def _from_token_tiles(t):
    blocks = pltpu.einshape("psl->spl", t)
    return jnp.concatenate([blocks[s] for s in range(TOK_SUB)], axis=1)


def _outproj_body(x_ref, at_ref, dn_ref, ag_ref, wo_ref, lg_ref, wrh_ref, wrl_ref, rb_ref, su_ref,
                  x1_ref, h2_ref, eid_ref, gate_ref, rank_ref, cnt_ref, base_s):
    step = pl.program_id(0)
    tm = x_ref.shape[0]

    @pl.when(step == 0)
    def _():
        base_s[...] = jnp.zeros_like(base_s)

    a = at_ref[...].astype(F32)
    an = a * lax.rsqrt(jnp.mean(a * a, axis=-1, keepdims=True) + EPS) * ag_ref[...]
    mix = jnp.concatenate([an.astype(BF16), dn_ref[...]], axis=1)
    x1 = x_ref[...] + _dot(mix, wo_ref[...])
    x1_ref[...] = x1
    h2 = x1 * lax.rsqrt(jnp.mean(x1 * x1, axis=-1, keepdims=True) + EPS) * lg_ref[...]
    h2_ref[...] = _to_token_tiles(h2)

    hi, lo = _split_hi_lo(h2)
    wrh = wrh_ref[...]
    lt = _dot_nt(wrh, hi) + _dot_nt(wrh, lo) + _dot_nt(wrl_ref[...], hi) + rb_ref[...]

    sub8 = lax.broadcasted_iota(I32, (N_GROUPS, tm), 0)
    gl = lt[0:N_GROUPS]
    gmax = jnp.max(gl, axis=0, keepdims=True)
    gsel = jnp.min(jnp.where(gl == gmax, sub8, N_GROUPS), axis=0, keepdims=True)
    gp = 1.0 / jnp.sum(jnp.exp(gl - gmax), axis=0, keepdims=True)
    in_group = jnp.zeros((EPG, tm), F32)
    for g in range(N_GROUPS):
        in_group = in_group + jnp.where(gsel == g, lt[N_GROUPS + g * EPG:N_GROUPS + (g + 1) * EPG], 0.0)
    v1 = jnp.max(in_group, axis=0, keepdims=True)
    i1 = jnp.min(jnp.where(in_group == v1, sub8, EPG), axis=0, keepdims=True)
    rest = jnp.where(sub8 == i1, -jnp.inf, in_group)
    v2 = jnp.max(rest, axis=0, keepdims=True)
    i2 = jnp.min(jnp.where(rest == v2, sub8, EPG), axis=0, keepdims=True)
    e21 = jnp.exp(v2 - v1)
    den = 1.0 + e21
    eid1 = gsel * EPG + i1
    eid2 = gsel * EPG + i2
    eid_ref[0:1, :] = eid1
    eid_ref[1:2, :] = eid2
    gate_ref[0:1, :] = gp * (1.0 / den)
    gate_ref[1:2, :] = gp * (e21 / den)

    sub64 = lax.broadcasted_iota(I32, (N_EXPERTS, tm), 0)
    oh1 = sub64 == eid1
    oh2 = sub64 == eid2
    ohs = jnp.where(oh1 | oh2, 1.0, 0.0)
    before = base_s[...] + _dot(ohs.astype(BF16), su_ref[...])
    rank_ref[0:1, :] = jnp.sum(jnp.where(oh1, before, 0.0), axis=0, keepdims=True).astype(I32)
    rank_ref[1:2, :] = jnp.sum(jnp.where(oh2, before, 0.0), axis=0, keepdims=True).astype(I32)
    base_new = base_s[...] + jnp.sum(ohs, axis=1, keepdims=True)
    base_s[...] = base_new
    cnt_ref[...] = jnp.broadcast_to(base_new, cnt_ref.shape)


def _outproj_router(x2, attn_raw, dn, attn_g, w_out, ln2_g, wg_r, bg_r, we_r, be_r, tm=512):
    n, d = x2.shape
    wr = jnp.concatenate([wg_r, we_r], axis=1).T.astype(F32)
    wr = jnp.pad(wr, ((0, ROUTER_ROWS - wr.shape[0]), (0, 0)))
    wrh, wrl = _split_hi_lo(wr)
    rb = jnp.pad(jnp.concatenate([bg_r, be_r]).astype(F32), (0, ROUTER_ROWS - N_GROUPS - N_EXPERTS))
    rb = rb.reshape(ROUTER_ROWS, 1)
    t_i = jnp.arange(tm)
    su = (t_i[:, None] < t_i[None, :]).astype(BF16)
    const = lambda i: (0, 0)
    row = lambda i: (i, 0)
    col = lambda i: (0, i)
    return pl.pallas_call(
        _outproj_body,
        grid=(n // tm,),
        in_specs=[
            pl.BlockSpec((tm, d), row),
            pl.BlockSpec((tm, ATTN_WIDTH), row),
            pl.BlockSpec((tm, DN_WIDTH), row),
            pl.BlockSpec((1, ATTN_WIDTH), const),
            pl.BlockSpec((ATTN_WIDTH + DN_WIDTH, d), const),
            pl.BlockSpec((1, d), const),
            pl.BlockSpec((ROUTER_ROWS, d), const),
            pl.BlockSpec((ROUTER_ROWS, d), const),
            pl.BlockSpec((ROUTER_ROWS, 1), const),
            pl.BlockSpec((tm, tm), const),
        ],
        out_specs=[
            pl.BlockSpec((tm, d), row),
            pl.BlockSpec((tm, TOK_SUB, LANES), lambda i: (i, 0, 0)),
            pl.BlockSpec((2, tm), col),
            pl.BlockSpec((2, tm), col),
            pl.BlockSpec((2, tm), col),
            pl.BlockSpec((N_EXPERTS, LANES), const),
        ],
        out_shape=[
            jax.ShapeDtypeStruct((n, d), F32),
            jax.ShapeDtypeStruct((n, TOK_SUB, LANES), F32),
            jax.ShapeDtypeStruct((2, n), I32),
            jax.ShapeDtypeStruct((2, n), F32),
            jax.ShapeDtypeStruct((2, n), I32),
            jax.ShapeDtypeStruct((N_EXPERTS, LANES), F32),
        ],
        scratch_shapes=[pltpu.VMEM((N_EXPERTS, 1), F32)],
        compiler_params=_cparams(("arbitrary",)),
    )(x2, attn_raw, dn, attn_g.reshape(1, -1).astype(F32), w_out.astype(BF16), ln2_g.reshape(1, d).astype(F32),
      wrh, wrl, rb, su)


def _dispatch_body(dest_ref, zrow_ref, h_ref, xs_hbm, zbuf, zsem, sem, *, n):
    tm = h_ref.shape[0]
    base = pl.program_id(0) * tm

    @pl.when(pl.program_id(0) == 0)
    def _():
        zbuf[...] = jnp.zeros_like(zbuf)

        def zero_copy(e):
            return pltpu.make_async_copy(zbuf, xs_hbm.at[pl.ds(zrow_ref[e], MOE_BLOCK)], zsem)

        def start(e, c):
            @pl.when(zrow_ref[e] >= 0)
            def _():
                zero_copy(e).start()
            return c

        def finish(e, c):
            @pl.when(zrow_ref[e] >= 0)
            def _():
                zero_copy(e).wait()
            return c

        lax.fori_loop(0, N_EXPERTS, start, 0)
        lax.fori_loop(0, N_EXPERTS, finish, 0)

        def tail_copy(b):
            return pltpu.make_async_copy(zbuf, xs_hbm.at[pl.ds(b * MOE_BLOCK, MOE_BLOCK)], zsem)

        def tail_start(b, c):
            tail_copy(b).start()
            return c

        def tail_finish(b, c):
            tail_copy(b).wait()
            return c

        n_blocks = xs_hbm.shape[0] // MOE_BLOCK
        lax.fori_loop(zrow_ref[N_EXPERTS], n_blocks, tail_start, 0)
        lax.fori_loop(zrow_ref[N_EXPERTS], n_blocks, tail_finish, 0)

    def issue(j, c):
        src = h_ref.at[pl.ds(j, 1)]
        pltpu.make_async_copy(src, xs_hbm.at[pl.ds(dest_ref[base + j], 1)], sem).start()
        pltpu.make_async_copy(src, xs_hbm.at[pl.ds(dest_ref[n + base + j], 1)], sem).start()
        return c

    lax.fori_loop(0, tm, issue, 0, unroll=8)
    for _ in range(2):
        pltpu.make_async_copy(h_ref, xs_hbm.at[pl.ds(0, tm)], sem).wait()


def _dispatch(h2t, dest, zrow, p_rows, tm=1024):
    n = h2t.shape[0]
    grid_spec = pltpu.PrefetchScalarGridSpec(
        num_scalar_prefetch=2,
        grid=(n // tm,),
        in_specs=[pl.BlockSpec((tm, TOK_SUB, LANES), lambda i, dst, zr: (i, 0, 0))],
        out_specs=pl.BlockSpec(memory_space=pl.ANY),
        scratch_shapes=[pltpu.VMEM((MOE_BLOCK, TOK_SUB, LANES), F32), pltpu.SemaphoreType.DMA(()),
                        pltpu.SemaphoreType.DMA(())],
    )
    return pl.pallas_call(
        functools.partial(_dispatch_body, n=n),
        grid_spec=grid_spec,
        out_shape=jax.ShapeDtypeStruct((p_rows, TOK_SUB, LANES), F32),
        compiler_params=_cparams(("arbitrary",)),
    )(dest.reshape(-1), zrow, h2t)


EXPERT_STEP_BLOCKS = 4


def _expert_body(be_ref, nu_ref, xs_ref, *refs):
    w_refs, o_ref = refs[:-1], refs[-1]
    for h in range(EXPERT_STEP_BLOCKS):
        wg_ref, wu_ref, wd_ref = w_refs[3 * h:3 * h + 3]
        rows = pl.ds(h * MOE_BLOCK, MOE_BLOCK)
        used = pl.program_id(0) * EXPERT_STEP_BLOCKS + h < nu_ref[0]

        @pl.when(used)
        def _():
            x = _from_token_tiles(xs_ref[rows]).astype(BF16)
            gt = _dot(x, wg_ref[0])
            up = _dot(x, wu_ref[0])
            hid = gt * jax.nn.sigmoid(gt) * up
            out = _dot(hid.astype(BF16), wd_ref[0])
            o_ref[rows] = _to_token_tiles(out)

        @pl.when(jnp.logical_not(used))
        def _():
            o_ref[rows] = jnp.zeros((MOE_BLOCK, TOK_SUB, LANES), o_ref.dtype)


def _experts(xs, block_expert, n_used, w_gate, w_up, w_down):
    p_rows = xs.shape[0]
    step_rows = EXPERT_STEP_BLOCKS * MOE_BLOCK
    d, de = w_gate.shape[-2:]
    wmap = lambda h: (lambda i, be, nu: (be[i * EXPERT_STEP_BLOCKS + h], 0, 0))
    xmap = lambda i, be, nu: (jnp.minimum(i, (nu[0] - 1) // EXPERT_STEP_BLOCKS), 0, 0)
    w_specs = []
    for h in range(EXPERT_STEP_BLOCKS):
        w_specs += [pl.BlockSpec((1, d, de), wmap(h)), pl.BlockSpec((1, d, de), wmap(h)),
                    pl.BlockSpec((1, de, d), wmap(h))]
    grid_spec = pltpu.PrefetchScalarGridSpec(
        num_scalar_prefetch=2,
        grid=(p_rows // step_rows,),
        in_specs=[pl.BlockSpec((step_rows, TOK_SUB, LANES), xmap)] + w_specs,
        out_specs=pl.BlockSpec((step_rows, TOK_SUB, LANES), lambda i, be, nu: (i, 0, 0)),
    )
    wg, wu, wd = w_gate.astype(BF16), w_up.astype(BF16), w_down.astype(BF16)
    return pl.pallas_call(
        _expert_body,
        grid_spec=grid_spec,
        out_shape=jax.ShapeDtypeStruct((p_rows, TOK_SUB, LANES), F32),
        compiler_params=_cparams(("arbitrary",)),
    )(block_expert, n_used, xs, *([wg, wu, wd] * EXPERT_STEP_BLOCKS))


def _combine_body(dest_ref, x1_ref, gt_ref, eo_hbm, out_ref, rbuf, sem, *, n):
    tm = x1_ref.shape[0]
    step = pl.program_id(0)
    slot = step % 2

    def gather(tile, slot_):
        base = tile * tm

        def issue(j, c):
            d0 = dest_ref[base + j]
            d1 = dest_ref[n + base + j]
            pltpu.make_async_copy(eo_hbm.at[pl.ds(d0, 1)], rbuf.at[slot_, 0, pl.ds(j, 1)], sem.at[slot_]).start()
            pltpu.make_async_copy(eo_hbm.at[pl.ds(d1, 1)], rbuf.at[slot_, 1, pl.ds(j, 1)], sem.at[slot_]).start()
            return c

        lax.fori_loop(0, tm, issue, 0, unroll=8)

    @pl.when(step == 0)
    def _():
        gather(0, 0)

    @pl.when(step + 1 < pl.num_programs(0))
    def _():
        gather(step + 1, 1 - slot)

    for k in range(2):
        pltpu.make_async_copy(eo_hbm.at[pl.ds(0, tm)], rbuf.at[slot, k], sem.at[slot]).wait()
    g0 = jnp.broadcast_to(gt_ref[:, 0:1], (tm, LANES))
    g1 = jnp.broadcast_to(gt_ref[:, 1:2], (tm, LANES))
    r0 = pltpu.einshape("psl->spl", rbuf[slot, 0])
    r1 = pltpu.einshape("psl->spl", rbuf[slot, 1])
    for s in range(TOK_SUB):
        cols = slice(s * LANES, (s + 1) * LANES)
        out_ref[:, cols] = x1_ref[:, cols] + (g0 * r0[s] + g1 * r1[s])


def _combine(x1, gates_t, dest, eo, tm=512):
    n, d = x1.shape
    grid_spec = pltpu.PrefetchScalarGridSpec(
        num_scalar_prefetch=1,
        grid=(n // tm,),
        in_specs=[
            pl.BlockSpec((tm, d), lambda i, dst: (i, 0)),
            pl.BlockSpec((tm, 2), lambda i, dst: (i, 0)),
            pl.BlockSpec(memory_space=pl.ANY),
        ],
        out_specs=pl.BlockSpec((tm, d), lambda i, dst: (i, 0)),
        scratch_shapes=[pltpu.VMEM((2, 2, tm, TOK_SUB, LANES), F32), pltpu.SemaphoreType.DMA((2,))],
    )
    return pl.pallas_call(
        functools.partial(_combine_body, n=n),
        grid_spec=grid_spec,
        out_shape=jax.ShapeDtypeStruct((n, d), F32),
        compiler_params=_cparams(("arbitrary",)),
    )(dest.reshape(-1), x1, gates_t, eo)


def _dispatch_plan(eid, rank, counts, n):
    nk = 2 * n
    p_rows = -(-nk // MOE_BLOCK) * MOE_BLOCK + N_EXPERTS * MOE_BLOCK
    nb = p_rows // MOE_BLOCK
    padded = -(-counts // MOE_BLOCK) * MOE_BLOCK
    pends = jnp.cumsum(padded)
    pstarts = pends - padded
    experts = jnp.arange(N_EXPERTS, dtype=I32)
    dest = jnp.sum(jnp.where(eid[..., None] == experts, pstarts.astype(I32), 0), axis=-1) + rank
    block_start = jnp.arange(nb, dtype=I32) * MOE_BLOCK
    block_expert = jnp.clip(jnp.sum((pends[None, :] <= block_start[:, None]).astype(I32), axis=-1),
                            0, N_EXPERTS - 1).astype(I32)
    n_used = (pends[-1] // MOE_BLOCK).astype(I32).reshape(1)
    zrow = jnp.concatenate([jnp.where(padded > 0, pends - MOE_BLOCK, -1).astype(I32), n_used])
    return dest.astype(I32), zrow, block_expert, n_used, p_rows


def _layer(x, ln1_g, w_in, q_g, k_g, rpb, attn_g, conv_w, a_log, dt_bias, dn_g, w_out, ln2_g,
           wg_r, bg_r, we_r, be_r, w_gate, w_up, w_down):
    b, s_len, d = x.shape
    n = b * s_len
    x2 = x.reshape(n, d)
    aqkv, dqkv, dz, dab = _inproj(x2, ln1_g, w_in)
    attn_raw = _attention(aqkv.reshape(b, s_len, -1), q_g, k_g, rpb)
    dn = _deltanet(dqkv.reshape(b, s_len, -1), dz.reshape(b, s_len, -1), dab.reshape(b, s_len, -1),
                   conv_w, a_log, dt_bias, dn_g)
    x1, h2t, eid, gates, rank, cnt = _outproj_router(
        x2, attn_raw.reshape(n, -1), dn.reshape(n, -1), attn_g, w_out, ln2_g, wg_r, bg_r, we_r, be_r)
    counts = cnt[:, 0].astype(I32)
    dest, zrow, block_expert, n_used, p_rows = _dispatch_plan(eid, rank, counts, n)
    xs = _dispatch(h2t, dest, zrow, p_rows)
    eo = _experts(xs, block_expert, n_used, w_gate, w_up, w_down)
    out = _combine(x1, gates.T, dest, eo)
    return out.reshape(b, s_len, d)


def kernel(x, ln1_g, w_in, attn_q_norm_g, attn_k_norm_g, attn_rpb, attn_out_norm_g, dn_conv_w, dn_a_log,
           dn_dt_bias, dn_out_norm_g, w_out, ln2_g, router_group_w, router_group_b, router_expert_w,
           router_expert_b, expert_w_gate, expert_w_up, expert_w_down):
    for l in range(ln1_g.shape[0]):
        x = _layer(x, ln1_g[l], w_in[l], attn_q_norm_g[l], attn_k_norm_g[l], attn_rpb[l],
                   attn_out_norm_g[l], dn_conv_w[l], dn_a_log[l], dn_dt_bias[l], dn_out_norm_g[l],
                   w_out[l], ln2_g[l], router_group_w[l], router_group_b[l], router_expert_w[l],
                   router_expert_b[l], expert_w_gate[l], expert_w_up[l], expert_w_down[l])
    return x
```
